```python
import jax, jax.numpy as jnp
from jax import lax
import numpy as np

D_MODEL = 1024
BATCH = 16
SEQ = 2048
DEPTH = 2

GRID_W = 64
CTX_LEN = 256
N_HEADS_A = 8
N_KV_A = 2
HEAD_DIM_A = 64
WINDOW = 128
MLA_HEADS = 8
MLA_Q_RANK = 384
MLA_KV_RANK = 256
MLA_NOPE = 64
MLA_ROPE = 32
MLA_V = 64
N_HEADS_C = 8
N_KV_C = 2
HEAD_DIM_C = 128
N_EXPERTS = 32
TOP_K = 4
D_FF = 1024
SWIGLU_ALPHA = 1.702
SWIGLU_LIMIT = 7.0
Q_BLOCK = 128
EXPERT_BLOCK = 128
ROPE_THETA = 10000.0
RMS_EPS = 1e-6
NEG_INF = -1e30
N_EVEN = (DEPTH + 1) // 2
N_ODD = DEPTH // 2
AB_SPLIT_SIZES = [N_HEADS_A * HEAD_DIM_A, N_KV_A * HEAD_DIM_A, N_KV_A * HEAD_DIM_A,
                  MLA_Q_RANK, MLA_KV_RANK, MLA_ROPE]
AB_IN = sum(AB_SPLIT_SIZES)
AB_OUT = N_HEADS_A * HEAD_DIM_A + MLA_HEADS * MLA_V
C_IN = (N_HEADS_C + 2 * N_KV_C) * HEAD_DIM_C
C_OUT = N_HEADS_C * HEAD_DIM_C

kernel_name = "hybrid_swa_mla_gqa2d_moe_dit"


def rms_norm(x, g):
    xf = x.astype(jnp.float32)
    y = xf * lax.rsqrt(jnp.mean(xf * xf, axis=-1, keepdims=True) + RMS_EPS)
    return (y * g.astype(jnp.float32)).astype(x.dtype)


def grid_positions(n_tokens):
    n_rows = n_tokens // GRID_W
    rows = jnp.broadcast_to(jnp.arange(n_rows, dtype=jnp.int32)[:, None], (n_rows, GRID_W)).reshape(-1)
    cols = jnp.broadcast_to(jnp.arange(GRID_W, dtype=jnp.int32)[None, :], (n_rows, GRID_W)).reshape(-1)
    return rows, cols


def axial_rope_table(rows, cols, rot_dim):
    quarter = rot_dim // 4
    inv = ROPE_THETA ** (-jnp.arange(quarter, dtype=jnp.float32) / quarter)
    ang = jnp.concatenate([rows.astype(jnp.float32)[:, None] * inv,
                           cols.astype(jnp.float32)[:, None] * inv], axis=-1)
    return jnp.cos(ang), jnp.sin(ang)


def apply_rope(x, table):
    cos, sin = table
    cos = cos.astype(x.dtype)
    sin = sin.astype(x.dtype)
    x1, x2 = jnp.split(x, 2, axis=-1)
    return jnp.concatenate([x1 * cos - x2 * sin, x2 * cos + x1 * sin], axis=-1)


def heads_to_tokens(o):
    b, hk, g, t, d = o.shape
    return o.transpose(0, 3, 1, 2, 4).reshape(b, t, hk * g * d)


def sdpa(q, k, v, mask=None, sink=None):
    s = jnp.einsum('bhgqd,bhkd->bhgqk', q, k, preferred_element_type=jnp.float32)
    if mask is not None:
        s = jnp.where(mask, s, NEG_INF)
    if sink is None:
        p = jax.nn.softmax(s, axis=-1)
    else:
        sk = sink.astype(jnp.float32)[None, :, :, None, None]
        m = jnp.maximum(jnp.max(s, axis=-1, keepdims=True), sk)
        e = jnp.exp(s - m)
        p = e / (jnp.sum(e, axis=-1, keepdims=True) + jnp.exp(sk - m))
    return jnp.einsum('bhgqk,bhkd->bhgqd', p.astype(v.dtype), v)


def dense_block_attention(q, k, v):
    b, hk, g, t, _ = q.shape
    nb = t // Q_BLOCK

    def one(i):
        qb = lax.dynamic_slice_in_dim(q, i * Q_BLOCK, Q_BLOCK, axis=3)
        return sdpa(qb, k, v)

    out = lax.map(one, jnp.arange(nb))
    return jnp.moveaxis(out, 0, 3).reshape(b, hk, g, t, v.shape[-1])


def window_block_attention(q, k, v, k_ctx, v_ctx, sink):
    b, hk, g, t, _ = q.shape
    nb = t // Q_BLOCK
    pad = ((0, 0), (0, 0), (Q_BLOCK, Q_BLOCK), (0, 0))
    kp = jnp.pad(k, pad)
    vp = jnp.pad(v, pad)
    qi = jnp.arange(Q_BLOCK, dtype=jnp.int32)[:, None]
    kj = jnp.arange(3 * Q_BLOCK, dtype=jnp.int32)[None, :]
    ctx_ok = jnp.ones((Q_BLOCK, k_ctx.shape[2]), dtype=bool)

    def one(i):
        qb = lax.dynamic_slice_in_dim(q, i * Q_BLOCK, Q_BLOCK, axis=3)
        kb = lax.dynamic_slice_in_dim(kp, i * Q_BLOCK, 3 * Q_BLOCK, axis=2)
        vb = lax.dynamic_slice_in_dim(vp, i * Q_BLOCK, 3 * Q_BLOCK, axis=2)
        qpos = i * Q_BLOCK + qi
        kpos = (i - 1) * Q_BLOCK + kj
        ok = (jnp.abs(kpos - qpos) <= WINDOW) & (kpos >= 0) & (kpos < t)
        mask = jnp.concatenate([ok, ctx_ok], axis=-1)
        return sdpa(qb, jnp.concatenate([kb, k_ctx], axis=2), jnp.concatenate([vb, v_ctx], axis=2), mask, sink)

    out = lax.map(one, jnp.arange(nb))
    return jnp.moveaxis(out, 0, 3).reshape(b, hk, g, t, v.shape[-1])


def ab_heads(h, w_in, q_g, wq_b, kv_g, wkv_b, rope_a, rope_b):
    bn, t, _ = h.shape
    splits = np.cumsum(AB_SPLIT_SIZES)[:-1].tolist()
    qa, ka, va, cq, ckv, kr = jnp.split(h @ w_in, splits, axis=-1)
    qa = qa.reshape(bn, t, N_KV_A, N_HEADS_A // N_KV_A, HEAD_DIM_A).transpose(0, 2, 3, 1, 4)
    ka = ka.reshape(bn, t, N_KV_A, HEAD_DIM_A).transpose(0, 2, 1, 3)
    va = va.reshape(bn, t, N_KV_A, HEAD_DIM_A).transpose(0, 2, 1, 3)
    qb = (rms_norm(cq, q_g) @ wq_b).reshape(bn, t, MLA_HEADS, MLA_NOPE + MLA_ROPE).transpose(0, 2, 1, 3)
    kvb = (rms_norm(ckv, kv_g) @ wkv_b).reshape(bn, t, MLA_HEADS, MLA_NOPE + MLA_V).transpose(0, 2, 1, 3)
    q_nope, q_rope = qb[..., :MLA_NOPE], qb[..., MLA_NOPE:]
    k_nope, vb = kvb[..., :MLA_NOPE], kvb[..., MLA_NOPE:]
    kr = kr[:, None]
    if rope_a is not None:
        qa = apply_rope(qa, rope_a)
        ka = apply_rope(ka, rope_a)
        q_rope = apply_rope(q_rope, rope_b)
        kr = apply_rope(kr, rope_b)
    qa = qa * HEAD_DIM_A ** -0.5
    qb = jnp.concatenate([q_nope, q_rope], axis=-1)[:, :, None] * (MLA_NOPE + MLA_ROPE) ** -0.5
    kb = jnp.concatenate([k_nope, jnp.broadcast_to(kr, k_nope.shape[:-1] + (MLA_ROPE,))], axis=-1)
    return qa, ka, va, qb, kb, vb


def mixer_ab(h, hc, w_in, q_g, wq_b, kv_g, wkv_b, sink, w_out, rope_a, rope_b, with_ctx_out):
    qa, ka, va, qb, kb, vb = ab_heads(h, w_in, q_g, wq_b, kv_g, wkv_b, rope_a, rope_b)
    qa_c, ka_c, va_c, qb_c, kb_c, vb_c = ab_heads(hc, w_in, q_g, wq_b, kv_g, wkv_b, None, None)
    sink = sink.reshape(N_KV_A, N_HEADS_A // N_KV_A)
    oa = window_block_attention(qa, ka, va, ka_c, va_c, sink)
    ob = dense_block_attention(qb, jnp.concatenate([kb, kb_c], axis=2), jnp.concatenate([vb, vb_c], axis=2))
    y = jnp.concatenate([heads_to_tokens(oa), heads_to_tokens(ob)], axis=-1) @ w_out
    yc = None
    if with_ctx_out:
        oa_c = sdpa(qa_c, ka_c, va_c, None, sink)
        ob_c = sdpa(qb_c, kb_c, vb_c)
        yc = jnp.concatenate([heads_to_tokens(oa_c), heads_to_tokens(ob_c)], axis=-1) @ w_out
    return y, yc


def c_heads(h, w_in, qn_g, kn_g, rope):
    bn, t, _ = h.shape
    q, k, v = jnp.split(h @ w_in, [N_HEADS_C * HEAD_DIM_C, (N_HEADS_C + N_KV_C) * HEAD_DIM_C], axis=-1)
    q = rms_norm(q.reshape(bn, t, N_KV_C, N_HEADS_C // N_KV_C, HEAD_DIM_C), qn_g).transpose(0, 2, 3, 1, 4)
    k = rms_norm(k.reshape(bn, t, N_KV_C, HEAD_DIM_C), kn_g).transpose(0, 2, 1, 3)
    v = v.reshape(bn, t, N_KV_C, HEAD_DIM_C).transpose(0, 2, 1, 3)
    if rope is not None:
        q = apply_rope(q, rope)
        k = apply_rope(k, rope)
    return q * HEAD_DIM_C ** -0.5, k, v


def mixer_c(h, hc, w_in, qn_g, kn_g, w_out, rope_c, with_ctx_out):
    q, k, v = c_heads(h, w_in, qn_g, kn_g, rope_c)
    q_c, k_c, v_c = c_heads(hc, w_in, qn_g, kn_g, None)
    o = dense_block_attention(q, jnp.concatenate([k, k_c], axis=2), jnp.concatenate([v, v_c], axis=2))
    y = heads_to_tokens(o) @ w_out
    yc = None
    if with_ctx_out:
        yc = heads_to_tokens(sdpa(q_c, k_c, v_c)) @ w_out
    return y, yc


def clamped_swiglu(u):
    glu = jnp.minimum(u[..., ::2], SWIGLU_LIMIT)
    lin = jnp.clip(u[..., 1::2], -SWIGLU_LIMIT, SWIGLU_LIMIT)
    return glu * jax.nn.sigmoid(SWIGLU_ALPHA * glu) * (lin + 1)


def moe(h, router_w, router_b, w_in, b_in, w_out, b_out):
    n_tok, d = h.shape
    logits = (h @ router_w + router_b).astype(jnp.float32)
    top_vals, top_idx = lax.top_k(logits, TOP_K)
    gates = jax.nn.softmax(top_vals, axis=-1)
    n_assign = n_tok * TOP_K
    e_flat = top_idx.reshape(-1).astype(jnp.int32)
    tok_flat = jnp.arange(n_assign, dtype=jnp.int32) // TOP_K
    g_flat = gates.reshape(-1)
    order = jnp.argsort(e_flat)
    e_s, tok_s, g_s = e_flat[order], tok_flat[order], g_flat[order]
    counts = jnp.bincount(e_flat, length=N_EXPERTS).astype(jnp.int32)
    starts = jnp.cumsum(counts) - counts
    padded = (counts + EXPERT_BLOCK - 1) // EXPERT_BLOCK * EXPERT_BLOCK
    pends = jnp.cumsum(padded)
    pstarts = pends - padded
    dest = pstarts[e_s] + (jnp.arange(n_assign, dtype=jnp.int32) - starts[e_s])
    n_blocks = (n_assign + N_EXPERTS * (EXPERT_BLOCK - 1) + EXPERT_BLOCK - 1) // EXPERT_BLOCK
    cap = n_blocks * EXPERT_BLOCK
    tok_buf = jnp.zeros((cap,), jnp.int32).at[dest].set(tok_s)
    g_buf = jnp.zeros((cap,), h.dtype).at[dest].set(g_s.astype(h.dtype))
    blk_start = jnp.arange(n_blocks, dtype=jnp.int32) * EXPERT_BLOCK
    blk_expert = jnp.minimum(jnp.searchsorted(pends, blk_start, side='right'), N_EXPERTS - 1)

    def one(args):
        toks, e = args
        u = h[toks] @ w_in[e] + b_in[e]
        return clamped_swiglu(u) @ w_out[e] + b_out[e]

    y = lax.map(one, (tok_buf.reshape(n_blocks, EXPERT_BLOCK), blk_expert)).reshape(cap, d)
    return jnp.zeros_like(h).at[tok_buf].add(y * g_buf[:, None])


def setup_inputs(seed: int = 0) -> dict:
    key = jax.random.key(seed)
    ks = iter(jax.random.split(key, 32))

    def nrm(shape, scale):
        return jax.random.normal(next(ks), shape, jnp.float32) * scale

    def gain(shape):
        return 1.0 + nrm(shape, 0.05)

    return {
        "x": nrm((BATCH, SEQ, D_MODEL), 1.0),
        "c": nrm((BATCH, D_MODEL), 1.0),
        "ctx": nrm((BATCH, CTX_LEN, D_MODEL), 1.0),
        "c_ctx": nrm((D_MODEL,), 1.0),
        "ada_w": nrm((DEPTH, D_MODEL, 6 * D_MODEL), 0.5 * D_MODEL ** -0.5),
        "ada_b": nrm((DEPTH, 6 * D_MODEL), 0.02),
        "norm_mix_g": gain((DEPTH, D_MODEL)),
        "norm_ffn_g": gain((DEPTH, D_MODEL)),
        "ab_w_in": nrm((N_EVEN, D_MODEL, AB_IN), D_MODEL ** -0.5),
        "mla_q_norm_g": gain((N_EVEN, MLA_Q_RANK)),
        "mla_wq_b": nrm((N_EVEN, MLA_Q_RANK, MLA_HEADS * (MLA_NOPE + MLA_ROPE)), MLA_Q_RANK ** -0.5),
        "mla_kv_norm_g": gain((N_EVEN, MLA_KV_RANK)),
        "mla_wkv_b": nrm((N_EVEN, MLA_KV_RANK, MLA_HEADS * (MLA_NOPE + MLA_V)), MLA_KV_RANK ** -0.5),
        "swa_sink": nrm((N_EVEN, N_HEADS_A), 0.5),
        "ab_w_out": nrm((N_EVEN, AB_OUT, D_MODEL), AB_OUT ** -0.5),
        "c_w_in": nrm((N_ODD, D_MODEL, C_IN), D_MODEL ** -0.5),
        "c_q_norm_g": gain((N_ODD, HEAD_DIM_C)),
        "c_k_norm_g": gain((N_ODD, HEAD_DIM_C)),
        "c_w_out": nrm((N_ODD, C_OUT, D_MODEL), C_OUT ** -0.5),
        "router_w": nrm((DEPTH, D_MODEL, N_EXPERTS), D_MODEL ** -0.5),
        "router_b": nrm((DEPTH, N_EXPERTS), 0.01),
        "moe_w_in": nrm((DEPTH, N_EXPERTS, D_MODEL, 2 * D_FF), D_MODEL ** -0.5),
        "moe_b_in": nrm((DEPTH, N_EXPERTS, 2 * D_FF), 0.02),
        "moe_w_out": nrm((DEPTH, N_EXPERTS, D_FF, D_MODEL), D_FF ** -0.5),
        "moe_b_out": nrm((DEPTH, N_EXPERTS, D_MODEL), 0.02),
        "final_norm_g": gain((D_MODEL,)),
    }


def reference(x, c, ctx, c_ctx, ada_w, ada_b, norm_mix_g, norm_ffn_g, ab_w_in, mla_q_norm_g, mla_wq_b,
              mla_kv_norm_g, mla_wkv_b, swa_sink, ab_w_out, c_w_in, c_q_norm_g, c_k_norm_g, c_w_out,
              router_w, router_b, moe_w_in, moe_b_in, moe_w_out, moe_b_out, final_norm_g):
    bn, s, d = x.shape
    n_ctx = ctx.shape[1]
    rows, cols = grid_positions(s)
    rope_a = axial_rope_table(rows, cols, HEAD_DIM_A)
    rope_b = axial_rope_table(rows, cols, MLA_ROPE)
    rope_c = axial_rope_table(rows, cols, HEAD_DIM_C)
    c_act = jax.nn.silu(c)
    cc_act = jax.nn.silu(c_ctx)
    for i in range(DEPTH):
        with_ctx = i < DEPTH - 1
        j = i // 2
        mod = c_act @ ada_w[i] + ada_b[i]
        mod_c = cc_act @ ada_w[i] + ada_b[i]
        sh1, sc1, g1, sh2, sc2, g2 = [m[:, None] for m in jnp.split(mod, 6, axis=-1)]
        sh1c, sc1c, g1c, sh2c, sc2c, g2c = jnp.split(mod_c, 6, axis=-1)
        h = rms_norm(x, norm_mix_g[i]) * (1 + sc1) + sh1
        hc = rms_norm(ctx, norm_mix_g[i]) * (1 + sc1c) + sh1c
        if i % 2 == 0:
            y, yc = mixer_ab(h, hc, ab_w_in[j], mla_q_norm_g[j], mla_wq_b[j], mla_kv_norm_g[j], mla_wkv_b[j],
                             swa_sink[j], ab_w_out[j], rope_a, rope_b, with_ctx)
        else:
            y, yc = mixer_c(h, hc, c_w_in[j], c_q_norm_g[j], c_k_norm_g[j], c_w_out[j], rope_c, with_ctx)
        x = x + g1 * y
        tokens = (rms_norm(x, norm_ffn_g[i]) * (1 + sc2) + sh2).reshape(bn * s, d)
        if with_ctx:
            ctx = ctx + g1c * yc
            hc2 = rms_norm(ctx, norm_ffn_g[i]) * (1 + sc2c) + sh2c
            tokens = jnp.concatenate([tokens, hc2.reshape(bn * n_ctx, d)], axis=0)
        f = moe(tokens, router_w[i], router_b[i], moe_w_in[i], moe_b_in[i], moe_w_out[i], moe_b_out[i])
        x = x + g2 * f[:bn * s].reshape(bn, s, d)
        if with_ctx:
            ctx = ctx + g2c * f[bn * s:].reshape(bn, n_ctx, d)
    return rms_norm(x, final_norm_g)
```

```python
import functools

import jax
import jax.numpy as jnp
from jax import lax
from jax.experimental import pallas as pl
from jax.experimental.pallas import tpu as pltpu

GRID_W = 64
N_HEADS_A, N_KV_A, HEAD_DIM_A, WINDOW = 8, 2, 64, 128
MLA_HEADS, MLA_Q_RANK, MLA_KV_RANK, MLA_NOPE, MLA_ROPE, MLA_V = 8, 384, 256, 64, 32, 64
N_HEADS_C, N_KV_C, HEAD_DIM_C = 8, 2, 128
N_EXPERTS, TOP_K = 32, 4
SWIGLU_ALPHA, SWIGLU_LIMIT = 1.702, 7.0
ROPE_THETA, RMS_EPS, NEG_INF = 10000.0, 1e-6, -1e30

LANES = 128
SUBLANES = 8
VMEM_LIMIT_BYTES = 56 * 1024 * 1024

F32 = jnp.float32
BF16 = jnp.bfloat16
HIGHEST = lax.Precision.HIGHEST


def _params(*sem):
    return pltpu.CompilerParams(dimension_semantics=sem, vmem_limit_bytes=VMEM_LIMIT_BYTES)


def _dot(a, b):
    return jnp.dot(a, b, preferred_element_type=F32)


def _dot_t(a, b):
    return lax.dot_general(a, b, (((1,), (1,)), ((), ())), preferred_element_type=F32)


def _rms(x, g):
    return x * lax.rsqrt(jnp.mean(x * x, axis=-1, keepdims=True) + RMS_EPS) * g


def _modnorm(x, g, sc, sh):
    return _rms(x, g) * (1.0 + sc) + sh


def _rope_group(x, cos, sin, half, lo):
    lane = lax.broadcasted_iota(jnp.int32, x.shape, 1)
    first = (lane >= lo) & (lane < lo + half)
    rot = jnp.where(first, pltpu.roll(x, LANES - half, 1), pltpu.roll(x, half, 1))
    return x * cos + rot * sin


def _groups(x):
    return [x[:, i * LANES:(i + 1) * LANES] for i in range(x.shape[1] // LANES)]


def _mod_kernel(c_ref, w_ref, b_ref, o_ref):
    c = c_ref[...]
    a = c * jax.nn.sigmoid(c)
    o_ref[0] = jnp.dot(a, w_ref[0], preferred_element_type=F32, precision=HIGHEST) + b_ref[0]


def _modulation(c_all, ada_w, ada_b):
    depth, d, n = ada_w.shape
    r = c_all.shape[0]
    nt = n // 4
    return pl.pallas_call(
        _mod_kernel,
        out_shape=jax.ShapeDtypeStruct((depth, r, n), F32),
        grid=(depth, n // nt),
        in_specs=[
            pl.BlockSpec((r, d), lambda i, j: (0, 0)),
            pl.BlockSpec((1, d, nt), lambda i, j: (i, 0, j)),
            pl.BlockSpec((1, 1, nt), lambda i, j: (i, 0, j)),
        ],
        out_specs=pl.BlockSpec((1, r, nt), lambda i, j: (i, 0, j)),
        compiler_params=_params("arbitrary", "arbitrary"),
        name="modulation",
    )(c_all, ada_w, ada_b.reshape(depth, 1, n))


def _ab_in_kernel(*refs, rope):
    if rope:
        (x_ref, g_ref, sc_ref, sh_ref, w1_ref, qg_ref, wq_ref, kvg_ref, wk_ref, wv_ref,
         ca_ref, sa_ref, cb_ref, sb_ref, qa_ref, ka_ref, va_ref, qb_ref, kb_ref, vb_ref) = refs
    else:
        (x_ref, g_ref, sc_ref, sh_ref, w1_ref, qg_ref, wq_ref, kvg_ref, wk_ref, wv_ref,
         qa_ref, ka_ref, va_ref, qb_ref, kb_ref, vb_ref) = refs
    h = _modnorm(x_ref[...], g_ref[...], sc_ref[0], sh_ref[0]).astype(BF16)
    p = _dot(h, w1_ref[...])
    nqa = N_HEADS_A * LANES
    nka = N_KV_A * LANES
    o = 0
    qa = p[:, o:o + nqa]; o += nqa
    ka = p[:, o:o + nka]; o += nka
    va = p[:, o:o + nka]; o += nka
    cq = p[:, o:o + MLA_Q_RANK]; o += MLA_Q_RANK
    ckv = p[:, o:o + MLA_KV_RANK]; o += MLA_KV_RANK
    krg = p[:, o:o + LANES]

    qb = _dot(_rms(cq, qg_ref[...]).astype(BF16), wq_ref[...])
    ckv_n = _rms(ckv, kvg_ref[...]).astype(BF16)
    kb = _dot(ckv_n, wk_ref[...])
    vb = _dot(ckv_n, wv_ref[...])

    scale_a = HEAD_DIM_A ** -0.5
    scale_b = (MLA_NOPE + MLA_ROPE) ** -0.5
    if rope:
        ca, sa, cb, sb = ca_ref[...], sa_ref[...], cb_ref[...], sb_ref[...]
        half_a, half_b = HEAD_DIM_A // 2, MLA_ROPE // 2
        qa_g = [_rope_group(t, ca, sa, half_a, 0) * scale_a for t in _groups(qa)]
        ka_g = [_rope_group(t, ca, sa, half_a, 0) for t in _groups(ka)]
        qb_g = [_rope_group(t, cb, sb, half_b, MLA_NOPE) * scale_b for t in _groups(qb)]
        krg = _rope_group(krg, cb, sb, half_b, MLA_NOPE)
    else:
        qa_g = [t * scale_a for t in _groups(qa)]
        ka_g = _groups(ka)
        qb_g = [t * scale_b for t in _groups(qb)]
    kb_g = [t + krg for t in _groups(kb)]
    qa_ref[...] = jnp.concatenate(qa_g, axis=1).astype(BF16)
    ka_ref[...] = jnp.concatenate(ka_g, axis=1).astype(BF16)
    va_ref[...] = va.astype(BF16)
    qb_ref[...] = jnp.concatenate(qb_g, axis=1).astype(BF16)
    kb_ref[...] = jnp.concatenate(kb_g, axis=1).astype(BF16)
    vb_ref[...] = vb.astype(BF16)


def _row_tile(t, pref=512):
    tm = pref
    while t % tm:
        tm //= 2
    return tm


def _mod_index(rows_per_batch, tm, fixed):
    if fixed is not None:
        return lambda i: (fixed, 0, 0)
    return lambda i: ((i * tm) // rows_per_batch, 0, 0)


def _ab_in(x, g, sc, sh, w, tables, rows_per_batch, fixed_mod):
    t, d = x.shape
    tm = _row_tile(min(t, rows_per_batch))
    rope = tables is not None
    midx = _mod_index(rows_per_batch, tm, fixed_mod)
    full = lambda a: pl.BlockSpec(a.shape, lambda i: (0,) * a.ndim)
    in_specs = [
        pl.BlockSpec((tm, d), lambda i: (i, 0)),
        full(g),
        pl.BlockSpec((1, 1, d), midx),
        pl.BlockSpec((1, 1, d), midx),
    ] + [full(a) for a in w]
    args = [x, g, sc, sh] + list(w)
    if rope:
        nt = rows_per_batch // tm
        for tab in tables:
            in_specs.append(pl.BlockSpec((tm, LANES), lambda i: (i % nt, 0)))
            args.append(tab)
    widths = [N_HEADS_A * LANES, N_KV_A * LANES, N_KV_A * LANES,
              MLA_HEADS * LANES, MLA_HEADS * LANES, MLA_HEADS * LANES]
    return pl.pallas_call(
        functools.partial(_ab_in_kernel, rope=rope),
        out_shape=[jax.ShapeDtypeStruct((t, n), BF16) for n in widths],
        grid=(t // tm,),
        in_specs=in_specs,
        out_specs=[pl.BlockSpec((tm, n), lambda i: (i, 0)) for n in widths],
        compiler_params=_params("arbitrary"),
        name="ab_in_rope" if rope else "ab_in",
    )(*args)


def _c_in_kernel(*refs, rope):
    if rope:
        x_ref, g_ref, sc_ref, sh_ref, w_ref, qn_ref, kn_ref, cc_ref, sc2_ref, q_ref, k_ref, v_ref = refs
    else:
        x_ref, g_ref, sc_ref, sh_ref, w_ref, qn_ref, kn_ref, q_ref, k_ref, v_ref = refs
    h = _modnorm(x_ref[...], g_ref[...], sc_ref[0], sh_ref[0]).astype(BF16)
    p = _dot(h, w_ref[...])
    nq = N_HEADS_C * HEAD_DIM_C
    nk = N_KV_C * HEAD_DIM_C
    q_g = [_rms(t, qn_ref[...]) for t in _groups(p[:, :nq])]
    k_g = [_rms(t, kn_ref[...]) for t in _groups(p[:, nq:nq + nk])]
    scale = HEAD_DIM_C ** -0.5
    if rope:
        cc, ss = cc_ref[...], sc2_ref[...]
        q_g = [_rope_group(t, cc, ss, HEAD_DIM_C // 2, 0) for t in q_g]
        k_g = [_rope_group(t, cc, ss, HEAD_DIM_C // 2, 0) for t in k_g]
    q_ref[...] = jnp.concatenate([t * scale for t in q_g], axis=1).astype(BF16)
    k_ref[...] = jnp.concatenate(k_g, axis=1).astype(BF16)
    v_ref[...] = p[:, nq + nk:].astype(BF16)


def _c_in(x, g, sc, sh, w, qn, kn, tables, rows_per_batch, fixed_mod):
    t, d = x.shape
    tm = _row_tile(min(t, rows_per_batch))
    rope = tables is not None
    midx = _mod_index(rows_per_batch, tm, fixed_mod)
    full = lambda a: pl.BlockSpec(a.shape, lambda i: (0,) * a.ndim)
    in_specs = [
        pl.BlockSpec((tm, d), lambda i: (i, 0)),
        full(g),
        pl.BlockSpec((1, 1, d), midx),
        pl.BlockSpec((1, 1, d), midx),
        full(w), full(qn), full(kn),
    ]
    args = [x, g, sc, sh, w, qn, kn]
    if rope:
        nt = rows_per_batch // tm
        for tab in tables:
            in_specs.append(pl.BlockSpec((tm, LANES), lambda i: (i % nt, 0)))
            args.append(tab)
    widths = [N_HEADS_C * HEAD_DIM_C, N_KV_C * HEAD_DIM_C, N_KV_C * HEAD_DIM_C]
    return pl.pallas_call(
        functools.partial(_c_in_kernel, rope=rope),
        out_shape=[jax.ShapeDtypeStruct((t, n), BF16) for n in widths],
        grid=(t // tm,),
        in_specs=in_specs,
        out_specs=[pl.BlockSpec((tm, n), lambda i: (i, 0)) for n in widths],
        compiler_params=_params("arbitrary"),
        name="c_in_rope" if rope else "c_in",
    )(*args)


def _attn_kernel(*refs, n_group, has_lat, has_sink, window, tq, s_lat, kw):
    refs = list(refs)
    sink_ref = refs.pop(0) if has_sink else None
    q_ref = refs.pop(0)
    if has_lat:
        k1_ref, v1_ref = refs.pop(0), refs.pop(0)
    k2_ref, v2_ref, o_ref = refs
    hk = pl.program_id(1)
    qi = pl.program_id(2)
    k2 = k2_ref[...]
    v2 = v2_ref[...]
    if has_lat:
        if window:
            ws = pl.multiple_of(jnp.clip(qi * tq - WINDOW, 0, s_lat - kw), LANES)
            k1 = k1_ref[pl.ds(ws, kw), :]
            v1 = v1_ref[pl.ds(ws, kw), :]
            qpos = qi * tq + lax.broadcasted_iota(jnp.int32, (tq, kw), 0)
            kpos = ws + lax.broadcasted_iota(jnp.int32, (tq, kw), 1)
            ok = jnp.abs(kpos - qpos) <= WINDOW
        else:
            k1 = k1_ref[...]
            v1 = v1_ref[...]
    for g in range(n_group):
        q = q_ref[:, g * LANES:(g + 1) * LANES]
        s2 = _dot_t(q, k2)
        m = jnp.max(s2, axis=-1, keepdims=True)
        if has_lat:
            s1 = _dot_t(q, k1)
            if window:
                s1 = jnp.where(ok, s1, NEG_INF)
            m = jnp.maximum(m, jnp.max(s1, axis=-1, keepdims=True))
        if has_sink:
            sk = sink_ref[hk * n_group + g]
            m = jnp.maximum(m, sk)
        e2 = jnp.exp(s2 - m)
        den = jnp.sum(e2, axis=-1, keepdims=True)
        acc = _dot(e2.astype(BF16), v2)
        if has_lat:
            e1 = jnp.exp(s1 - m)
            den = den + jnp.sum(e1, axis=-1, keepdims=True)
            acc = acc + _dot(e1.astype(BF16), v1)
        if has_sink:
            den = den + jnp.exp(sk - m)
        o_ref[:, g * LANES:(g + 1) * LANES] = (acc / den).astype(BF16)


def _attention(q, k_lat, v_lat, k_ctx, v_ctx, sink, *, batch, n_kv, n_group, window, tq):
    tq_total = q.shape[0] // batch
    tq = min(tq, tq_total)
    n_ctx = k_ctx.shape[0] // batch
    has_lat = k_lat is not None
    has_sink = sink is not None
    s_lat = k_lat.shape[0] // batch if has_lat else 0
    kw = min(tq + 2 * WINDOW, s_lat) if window else 0
    nq = tq_total // tq
    gw = n_group * LANES
    in_specs, args = [], []
    if has_sink:
        in_specs.append(pl.BlockSpec(memory_space=pltpu.SMEM))
        args.append(sink)
    in_specs.append(pl.BlockSpec((tq, gw), lambda b, h, i: (b * nq + i, h)))
    args.append(q)
    if has_lat:
        in_specs += [pl.BlockSpec((s_lat, LANES), lambda b, h, i: (b, h))] * 2
        args += [k_lat, v_lat]
    in_specs += [pl.BlockSpec((n_ctx, LANES), lambda b, h, i: (b, h))] * 2
    args += [k_ctx, v_ctx]
    return pl.pallas_call(
        functools.partial(_attn_kernel, n_group=n_group, has_lat=has_lat, has_sink=has_sink,
                          window=window, tq=tq, s_lat=s_lat, kw=kw),
        out_shape=jax.ShapeDtypeStruct(q.shape, BF16),
        grid=(batch, n_kv, nq),
        in_specs=in_specs,
        out_specs=pl.BlockSpec((tq, gw), lambda b, h, i: (b * nq + i, h)),
        compiler_params=_params("arbitrary", "arbitrary", "arbitrary"),
        name="attn_win" if window else ("attn_dense" if has_lat else "attn_ctx"),
    )(*args)


def _out_kernel(*refs, n_parts, tm):
    o_refs = refs[:n_parts]
    w_refs = refs[n_parts:2 * n_parts]
    (x_ref, g1_ref, gn_ref, sc_ref, sh_ref, rw_ref, rb_ref,
     xo_ref, tok_ref, idx_ref, gate_ref) = refs[2 * n_parts:]
    y = _dot(o_refs[0][...], w_refs[0][...])
    for o_r, w_r in zip(o_refs[1:], w_refs[1:]):
        y = y + _dot(o_r[...], w_r[...])
    x = x_ref[...] + g1_ref[0] * y
    xo_ref[...] = x
    tok = _modnorm(x, gn_ref[...], sc_ref[0], sh_ref[0])
    for j in range(tok.shape[1] // LANES):
        tok_ref[pl.ds(j, tm, stride=SUBLANES), :] = tok[:, j * LANES:(j + 1) * LANES]
    logits = jnp.dot(tok, rw_ref[...], preferred_element_type=F32, precision=HIGHEST) + rb_ref[...]
    lane = lax.broadcasted_iota(jnp.int32, logits.shape, 1).astype(F32)
    vals, idxs = [], []
    l = logits
    for _ in range(TOP_K):
        m = jnp.max(l, axis=-1, keepdims=True)
        idx = jnp.min(jnp.where(l == m, lane, float(LANES)), axis=-1, keepdims=True)
        vals.append(m)
        idxs.append(idx)
        l = jnp.where(lane == idx, -jnp.inf, l)
    es = [jnp.exp(v - vals[0]) for v in vals]
    den = es[0]
    for e in es[1:]:
        den = den + e
    lane8 = lax.broadcasted_iota(jnp.int32, (tm, SUBLANES), 1)
    io = jnp.zeros((tm, SUBLANES), jnp.int32)
    go = jnp.zeros((tm, SUBLANES), F32)
    for k in range(TOP_K):
        io = jnp.where(lane8 == k, idxs[k].astype(jnp.int32), io)
        go = jnp.where(lane8 == k, es[k] / den, go)
    idx_ref[...] = io
    gate_ref[...] = go


def _out_proj(o_parts, w_parts, x, g1, gn, sc, sh, rw, rb, rows_per_batch, fixed_mod):
    t, d = x.shape
    tm = _row_tile(min(t, rows_per_batch), 256)
    midx = _mod_index(rows_per_batch, tm, fixed_mod)
    full = lambda a: pl.BlockSpec(a.shape, lambda i: (0,) * a.ndim)
    n_parts = len(o_parts)
    in_specs = ([pl.BlockSpec((tm, o.shape[1]), lambda i: (i, 0)) for o in o_parts]
                + [full(w) for w in w_parts]
                + [pl.BlockSpec((tm, d), lambda i: (i, 0)),
                   pl.BlockSpec((1, 1, d), midx), full(gn),
                   pl.BlockSpec((1, 1, d), midx), pl.BlockSpec((1, 1, d), midx),
                   full(rw), full(rb)])
    return pl.pallas_call(
        functools.partial(_out_kernel, n_parts=n_parts, tm=tm),
        out_shape=[jax.ShapeDtypeStruct((t, d), F32),
                   jax.ShapeDtypeStruct((t * SUBLANES, d // SUBLANES), F32),
                   jax.ShapeDtypeStruct((t, SUBLANES), jnp.int32),
                   jax.ShapeDtypeStruct((t, SUBLANES), F32)],
        grid=(t // tm,),
        in_specs=in_specs,
        out_specs=[pl.BlockSpec((tm, d), lambda i: (i, 0)),
                   pl.BlockSpec((tm * SUBLANES, d // SUBLANES), lambda i: (i, 0)),
                   pl.BlockSpec((tm, SUBLANES), lambda i: (i, 0)),
                   pl.BlockSpec((tm, SUBLANES), lambda i: (i, 0))],
        compiler_params=_params("arbitrary"),
        name="out_proj",
    )(*o_parts, *w_parts, x, g1, gn, sc, sh, rw, rb)


ROWCOPY_CHUNK = 2048
ROWCOPY_UNROLL = 8


def _rowcopy_kernel(idx_ref, src_ref, dst_ref, sem, *, chunk):
    base = pl.program_id(0) * chunk

    def row_dma(src_row, dst_row):
        return pltpu.make_async_copy(
            src_ref.at[pl.ds(pl.multiple_of(src_row * SUBLANES, SUBLANES), SUBLANES)],
            dst_ref.at[pl.ds(pl.multiple_of(dst_row * SUBLANES, SUBLANES), SUBLANES)], sem)

    def body(t, carry):
        for u in range(ROWCOPY_UNROLL):
            r = t * ROWCOPY_UNROLL + u
            row_dma(idx_ref[0, 0, r], base + r).start()
        return carry

    lax.fori_loop(0, chunk // ROWCOPY_UNROLL, body, 0)
    done = dst_ref.at[pl.ds(pl.multiple_of(base * SUBLANES, SUBLANES), chunk * SUBLANES)]
    pltpu.make_async_copy(done, done, sem).wait()


def _rowcopy(src, idx):
    n_dst = idx.shape[0]
    chunk = ROWCOPY_CHUNK
    while n_dst % chunk:
        chunk //= 2
    assert n_dst % chunk == 0 and chunk % ROWCOPY_UNROLL == 0
    return pl.pallas_call(
        functools.partial(_rowcopy_kernel, chunk=chunk),
        out_shape=jax.ShapeDtypeStruct((n_dst * SUBLANES, LANES), src.dtype),
        grid=(n_dst // chunk,),
        in_specs=[pl.BlockSpec((1, 1, chunk), lambda i: (i, 0, 0), memory_space=pltpu.SMEM),
                  pl.BlockSpec(memory_space=pl.ANY)],
        out_specs=pl.BlockSpec(memory_space=pl.ANY),
        scratch_shapes=[pltpu.SemaphoreType.DMA],
        compiler_params=_params("arbitrary"),
        name="rowcopy",
    )(idx.reshape(n_dst // chunk, 1, chunk), src)


EXPERT_BLOCK = 512


def _expert_kernel(be_ref, nb_ref, x_ref, w1_ref, b1_ref, w2_ref, b2_ref, o_ref, *, bm):
    del be_ref
    b = pl.program_id(0)
    d_ff = w2_ref.shape[1]
    n_j = w1_ref.shape[1] // LANES

    @pl.when(b < nb_ref[0])
    def _():
        x = jnp.concatenate([x_ref[pl.ds(j, bm, stride=SUBLANES), :] for j in range(n_j)], axis=1)
        u = _dot(x.astype(BF16), w1_ref[0]) + b1_ref[0]
        glu = jnp.minimum(u[:, :d_ff], SWIGLU_LIMIT)
        lin = jnp.clip(u[:, d_ff:], -SWIGLU_LIMIT, SWIGLU_LIMIT)
        a = glu * jax.nn.sigmoid(SWIGLU_ALPHA * glu) * (lin + 1.0)
        y = _dot(a.astype(BF16), w2_ref[0]) + b2_ref[0]
        for j in range(y.shape[1] // LANES):
            o_ref[pl.ds(j, bm, stride=SUBLANES), :] = y[:, j * LANES:(j + 1) * LANES]

    @pl.when(b >= nb_ref[0])
    def _():
        o_ref[...] = jnp.zeros(o_ref.shape, o_ref.dtype)


def _expert_mlp(xs, blk_expert, n_used, w1, b1, w2, b2):
    bm = EXPERT_BLOCK
    nblk = blk_expert.shape[0]
    _, d, f2 = w1.shape
    grid_spec = pltpu.PrefetchScalarGridSpec(
        num_scalar_prefetch=2,
        grid=(nblk,),
        in_specs=[
            pl.BlockSpec((bm * SUBLANES, LANES), lambda b, be, nb: (b, 0)),
            pl.BlockSpec((1, d, f2), lambda b, be, nb: (be[b], 0, 0)),
            pl.BlockSpec((1, 1, f2), lambda b, be, nb: (be[b], 0, 0)),
            pl.BlockSpec((1, f2 // 2, d), lambda b, be, nb: (be[b], 0, 0)),
            pl.BlockSpec((1, 1, d), lambda b, be, nb: (be[b], 0, 0)),
        ],
        out_specs=pl.BlockSpec((bm * SUBLANES, LANES), lambda b, be, nb: (b, 0)),
    )
    return pl.pallas_call(
        functools.partial(_expert_kernel, bm=bm),
        out_shape=jax.ShapeDtypeStruct(xs.shape, F32),
        grid_spec=grid_spec,
        compiler_params=_params("arbitrary"),
        name="expert_mlp",
    )(blk_expert, n_used, xs, w1, b1, w2, b2)


def _combine_kernel(*refs, tm, final):
    if final:
        x_ref, y0, y1, y2, y3, gate_ref, g2_ref, fg_ref, o_ref = refs
    else:
        x_ref, y0, y1, y2, y3, gate_ref, g2_ref, o_ref = refs
    gate = gate_ref[...]
    ys = (y0, y1, y2, y3)
    parts = []
    for j in range(x_ref.shape[1] // LANES):
        f = gate[:, 0:1] * ys[0][pl.ds(j, tm, stride=SUBLANES), :]
        for k in range(1, TOP_K):
            f = f + gate[:, k:k + 1] * ys[k][pl.ds(j, tm, stride=SUBLANES), :]
        parts.append(f)
    x = x_ref[...] + g2_ref[0] * jnp.concatenate(parts, axis=1)
    if final:
        x = _rms(x, fg_ref[...])
    o_ref[...] = x


def _combine(x, y4, gates, g2, final_g, *, t_all, row_off, rows_per_batch, fixed_mod):
    t, d = x.shape
    tm = _row_tile(min(t, rows_per_batch), 256)
    midx = _mod_index(rows_per_batch, tm, fixed_mod)
    final = final_g is not None
    n_all, off = t_all // tm, row_off // tm
    in_specs = [pl.BlockSpec((tm, d), lambda i: (i, 0))]
    for k in range(TOP_K):
        in_specs.append(pl.BlockSpec((tm * SUBLANES, LANES), lambda i, k=k: (k * n_all + off + i, 0)))
    in_specs += [pl.BlockSpec((tm, SUBLANES), lambda i: (off + i, 0)), pl.BlockSpec((1, 1, d), midx)]
    args = [x, y4, y4, y4, y4, gates, g2]
    if final:
        in_specs.append(pl.BlockSpec(final_g.shape, lambda i: (0, 0)))
        args.append(final_g)
    return pl.pallas_call(
        functools.partial(_combine_kernel, tm=tm, final=final),
        out_shape=jax.ShapeDtypeStruct((t, d), F32),
        grid=(t // tm,),
        in_specs=in_specs,
        out_specs=pl.BlockSpec((tm, d), lambda i: (i, 0)),
        compiler_params=_params("arbitrary"),
        name="combine_final" if final else "combine",
    )(*args)


def _pad_heads(w, n_heads, axis):
    shape = list(w.shape)
    hd = shape[axis] // n_heads
    w = w.reshape(shape[:axis] + [n_heads, hd] + shape[axis + 1:])
    pad = [(0, 0)] * w.ndim
    pad[axis + 1] = (0, LANES - hd)
    w = jnp.pad(w, pad)
    shape[axis] = n_heads * LANES
    return w.reshape(shape)


def _rope_tables(s, rot_dim, lo):
    pos = jnp.arange(s, dtype=jnp.int32)
    rows, cols = (pos // GRID_W).astype(F32), (pos % GRID_W).astype(F32)
    quarter = rot_dim // 4
    inv = ROPE_THETA ** (-jnp.arange(quarter, dtype=F32) / quarter)
    ang = jnp.concatenate([rows[:, None] * inv, cols[:, None] * inv], axis=-1)
    cos, sin = jnp.cos(ang), jnp.sin(ang)
    hi = LANES - lo - rot_dim
    cos_t = jnp.concatenate([jnp.ones((s, lo), F32), cos, cos, jnp.ones((s, hi), F32)], axis=1)
    sin_t = jnp.concatenate([jnp.zeros((s, lo), F32), -sin, sin, jnp.zeros((s, hi), F32)], axis=1)
    return cos_t, sin_t


def _moe_plan(eidx, bm):
    t = eidx.shape[0]
    ids = jnp.arange(N_EXPERTS, dtype=jnp.int32)
    sel = eidx[:, :, None] == ids
    hit = sel.astype(jnp.int32).sum(1)
    incl = jnp.cumsum(hit, axis=0)
    counts = incl[-1]
    rank = jnp.where(sel, (incl - hit)[:, None, :], 0).sum(-1)
    padded = (counts + bm - 1) // bm * bm
    pends = jnp.cumsum(padded)
    pstarts = pends - padded
    dest = jnp.where(sel, pstarts, 0).sum(-1) + rank
    n_assign = t * TOP_K
    nblk = -(-(n_assign + N_EXPERTS * (bm - 1)) // bm)
    nblk = -(-nblk // 4) * 4
    blk_start = jnp.arange(nblk, dtype=jnp.int32) * bm
    blk_expert = jnp.minimum(jnp.searchsorted(pends, blk_start, side="right"), N_EXPERTS - 1)
    n_used = (pends[-1] // bm).astype(jnp.int32).reshape(1)
    tok = jnp.broadcast_to(jnp.arange(t, dtype=jnp.int32)[:, None], (t, TOP_K))
    src_of_row = jnp.zeros((nblk * bm,), jnp.int32).at[dest.reshape(-1)].set(
        tok.reshape(-1), unique_indices=True)
    return src_of_row, dest, blk_expert.astype(jnp.int32), n_used


def _moe(tokens_rows, eidx, w1, b1, w2, b2):
    src_of_row, dest, blk_expert, n_used = _moe_plan(eidx, EXPERT_BLOCK)
    xs = _rowcopy(tokens_rows, src_of_row)
    ys = _expert_mlp(xs, blk_expert, n_used, w1, b1, w2, b2)
    return _rowcopy(ys, dest.T.reshape(-1))


def kernel(x, c, ctx, c_ctx, ada_w, ada_b, norm_mix_g, norm_ffn_g, ab_w_in, mla_q_norm_g, mla_wq_b,
           mla_kv_norm_g, mla_wkv_b, swa_sink, ab_w_out, c_w_in, c_q_norm_g, c_k_norm_g, c_w_out,
           router_w, router_b, moe_w_in, moe_b_in, moe_w_out, moe_b_out, final_norm_g):
    bn, s, d = x.shape
    n_ctx = ctx.shape[1]
    depth = ada_w.shape[0]
    t_lat, t_ctx = bn * s, bn * n_ctx
    row = lambda v: v.reshape(1, -1)

    n_mod = -(-(bn + 1) // SUBLANES) * SUBLANES
    c_all = jnp.concatenate([c, c_ctx[None], jnp.zeros((n_mod - bn - 1, d), F32)], axis=0)
    mod = _modulation(c_all, ada_w, ada_b).reshape(depth, n_mod, 6, 1, d)
    ctx_row = bn

    tab_a = _rope_tables(s, HEAD_DIM_A, 0)
    tab_b = _rope_tables(s, MLA_ROPE, MLA_NOPE)
    tab_c = _rope_tables(s, HEAD_DIM_C, 0)

    xl = x.reshape(t_lat, d)
    xc = ctx.reshape(t_ctx, d)
    for i in range(depth):
        with_ctx = i < depth - 1
        j = i // 2
        sh1, sc1, g1, sh2, sc2, g2 = [mod[i, :, m] for m in range(6)]
        gmix, gffn = row(norm_mix_g[i]), row(norm_ffn_g[i])
        if i % 2 == 0:
            w_in = ab_w_in[j]
            sizes = [N_HEADS_A * HEAD_DIM_A, N_KV_A * HEAD_DIM_A, N_KV_A * HEAD_DIM_A,
                     MLA_Q_RANK, MLA_KV_RANK, MLA_ROPE]
            offs = [0]
            for n in sizes:
                offs.append(offs[-1] + n)
            cols = [w_in[:, offs[m]:offs[m + 1]] for m in range(6)]
            krg = jnp.pad(cols[5], ((0, 0), (MLA_NOPE, LANES - MLA_NOPE - MLA_ROPE)))
            w1 = jnp.concatenate([_pad_heads(cols[0], N_HEADS_A, 1), _pad_heads(cols[1], N_KV_A, 1),
                                  _pad_heads(cols[2], N_KV_A, 1), cols[3], cols[4], krg], axis=1).astype(BF16)
            wq = _pad_heads(mla_wq_b[j], MLA_HEADS, 1).astype(BF16)
            wkv = mla_wkv_b[j].reshape(MLA_KV_RANK, MLA_HEADS, MLA_NOPE + MLA_V)
            wk = _pad_heads(wkv[:, :, :MLA_NOPE].reshape(MLA_KV_RANK, -1), MLA_HEADS, 1).astype(BF16)
            wv = _pad_heads(wkv[:, :, MLA_NOPE:].reshape(MLA_KV_RANK, -1), MLA_HEADS, 1).astype(BF16)
            ws = [w1, row(mla_q_norm_g[j]), wq, row(mla_kv_norm_g[j]), wk, wv]
            na = N_HEADS_A * HEAD_DIM_A
            wo_a = _pad_heads(ab_w_out[j][:na], N_HEADS_A, 0).astype(BF16)
            wo_b = _pad_heads(ab_w_out[j][na:], MLA_HEADS, 0).astype(BF16)
            sink = swa_sink[j]

            qa, ka, va, qb, kb, vb = _ab_in(xl, gmix, sc1, sh1, ws, tab_a + tab_b, s, None)
            qa_c, ka_c, va_c, qb_c, kb_c, vb_c = _ab_in(xc, gmix, sc1, sh1, ws, None, n_ctx, ctx_row)
            ga = N_HEADS_A // N_KV_A
            oa = _attention(qa, ka, va, ka_c, va_c, sink, batch=bn, n_kv=N_KV_A, n_group=ga,
                            window=True, tq=256)
            ob = _attention(qb, kb, vb, kb_c, vb_c, None, batch=bn, n_kv=MLA_HEADS, n_group=1,
                            window=False, tq=512)
            o_lat, w_o = [oa, ob], [wo_a, wo_b]
            if with_ctx:
                oa_c = _attention(qa_c, None, None, ka_c, va_c, sink, batch=bn, n_kv=N_KV_A,
                                  n_group=ga, window=False, tq=256)
                ob_c = _attention(qb_c, None, None, kb_c, vb_c, None, batch=bn, n_kv=MLA_HEADS,
                                  n_group=1, window=False, tq=256)
                o_ctx = [oa_c, ob_c]
        else:
            w_in = c_w_in[j].astype(BF16)
            qn, kn = row(c_q_norm_g[j]), row(c_k_norm_g[j])
            q, k, v = _c_in(xl, gmix, sc1, sh1, w_in, qn, kn, tab_c, s, None)
            q_c, k_c, v_c = _c_in(xc, gmix, sc1, sh1, w_in, qn, kn, None, n_ctx, ctx_row)
            gc = N_HEADS_C // N_KV_C
            o = _attention(q, k, v, k_c, v_c, None, batch=bn, n_kv=N_KV_C, n_group=gc,
                           window=False, tq=256)
            o_lat, w_o = [o], [c_w_out[j].astype(BF16)]
            if with_ctx:
                o_ctx = [_attention(q_c, None, None, k_c, v_c, None, batch=bn, n_kv=N_KV_C,
                                    n_group=gc, window=False, tq=256)]

        rw = jnp.pad(router_w[i], ((0, 0), (0, LANES - N_EXPERTS)))
        rb = jnp.concatenate([router_b[i], jnp.full((LANES - N_EXPERTS,), NEG_INF, F32)]).reshape(1, LANES)
        xl, tok_l, idx_l, gate_l = _out_proj(o_lat, w_o, xl, g1, gffn, sc2, sh2, rw, rb, s, None)
        if with_ctx:
            xc, tok_c, idx_c, gate_c = _out_proj(o_ctx, w_o, xc, g1, gffn, sc2, sh2, rw, rb, n_ctx, ctx_row)
            tokens = jnp.concatenate([tok_l, tok_c], axis=0)
            eidx = jnp.concatenate([idx_l, idx_c], axis=0)
            gates = jnp.concatenate([gate_l, gate_c], axis=0)
        else:
            tokens, eidx, gates = tok_l, idx_l, gate_l
        t_all = eidx.shape[0]

        w1e = jnp.concatenate([moe_w_in[i][:, :, 0::2], moe_w_in[i][:, :, 1::2]], axis=-1).astype(BF16)
        b1e = jnp.concatenate([moe_b_in[i][:, 0::2], moe_b_in[i][:, 1::2]], axis=-1)[:, None, :]
        w2e = moe_w_out[i].astype(BF16)
        b2e = moe_b_out[i][:, None, :]
        y4 = _moe(tokens, eidx[:, :TOP_K], w1e, b1e, w2e, b2e)

        last = i == depth - 1
        xl = _combine(xl, y4, gates, g2, row(final_norm_g) if last else None,
                      t_all=t_all, row_off=0, rows_per_batch=s, fixed_mod=None)
        if with_ctx:
            xc = _combine(xc, y4, gates, g2, None,
                          t_all=t_all, row_off=t_lat, rows_per_batch=n_ctx, fixed_mod=ctx_row)
    return xl.reshape(bn, s, d)
```

```python
import functools

import jax
import jax.numpy as jnp
from jax import lax
from jax.experimental import pallas as pl
from jax.experimental.pallas import tpu as pltpu

GRID_W = 64
N_HEADS_A, N_KV_A, HEAD_DIM_A, WINDOW = 8, 2, 64, 128
MLA_HEADS, MLA_Q_RANK, MLA_KV_RANK, MLA_NOPE, MLA_ROPE, MLA_V = 8, 384, 256, 64, 32, 64
N_HEADS_C, N_KV_C, HEAD_DIM_C = 8, 2, 128
N_EXPERTS, TOP_K = 32, 4
SWIGLU_ALPHA, SWIGLU_LIMIT = 1.702, 7.0
ROPE_THETA, RMS_EPS, NEG_INF = 10000.0, 1e-6, -1e30

LANES = 128
SUBLANES = 8
VMEM_LIMIT_BYTES = 56 * 1024 * 1024

F32 = jnp.float32
BF16 = jnp.bfloat16
HIGHEST = lax.Precision.HIGHEST


def _params(*sem):
    return pltpu.CompilerParams(dimension_semantics=sem, vmem_limit_bytes=VMEM_LIMIT_BYTES)


def _dot(a, b):
    return jnp.dot(a, b, preferred_element_type=F32)


def _dot_t(a, b):
    return lax.dot_general(a, b, (((1,), (1,)), ((), ())), preferred_element_type=F32)


def _rms(x, g):
    return x * lax.rsqrt(jnp.mean(x * x, axis=-1, keepdims=True) + RMS_EPS) * g


def _modnorm(x, g, sc, sh):
    return _rms(x, g) * (1.0 + sc) + sh


def _rope_group(x, cos, sin, half, lo):
    lane = lax.broadcasted_iota(jnp.int32, x.shape, 1)
    first = (lane >= lo) & (lane < lo + half)
    rot = jnp.where(first, pltpu.roll(x, LANES - half, 1), pltpu.roll(x, half, 1))
    return x * cos + rot * sin


def _groups(x):
    return [x[:, i * LANES:(i + 1) * LANES] for i in range(x.shape[1] // LANES)]


def _mod_kernel(c_ref, w_ref, b_ref, o_ref):
    c = c_ref[...]
    a = c * jax.nn.sigmoid(c)
    o_ref[0] = jnp.dot(a, w_ref[0], preferred_element_type=F32, precision=HIGHEST) + b_ref[0]


def _modulation(c_all, ada_w, ada_b):
    depth, d, n = ada_w.shape
    r = c_all.shape[0]
    nt = n // 4
    return pl.pallas_call(
        _mod_kernel,
        out_shape=jax.ShapeDtypeStruct((depth, r, n), F32),
        grid=(depth, n // nt),
        in_specs=[
            pl.BlockSpec((r, d), lambda i, j: (0, 0)),
            pl.BlockSpec((1, d, nt), lambda i, j: (i, 0, j)),
            pl.BlockSpec((1, 1, nt), lambda i, j: (i, 0, j)),
        ],
        out_specs=pl.BlockSpec((1, r, nt), lambda i, j: (i, 0, j)),
        compiler_params=_params("arbitrary", "arbitrary"),
        name="modulation",
    )(c_all, ada_w, ada_b.reshape(depth, 1, n))


def _ab_in_kernel(*refs, rope):
    if rope:
        (x_ref, g_ref, sc_ref, sh_ref, w1_ref, qg_ref, wq_ref, kvg_ref, wk_ref, wv_ref,
         ca_ref, sa_ref, cb_ref, sb_ref, qa_ref, ka_ref, va_ref, qb_ref, kb_ref, vb_ref) = refs
    else:
        (x_ref, g_ref, sc_ref, sh_ref, w1_ref, qg_ref, wq_ref, kvg_ref, wk_ref, wv_ref,
         qa_ref, ka_ref, va_ref, qb_ref, kb_ref, vb_ref) = refs
    h = _modnorm(x_ref[...], g_ref[...], sc_ref[0], sh_ref[0]).astype(BF16)
    p = _dot(h, w1_ref[...])
    nqa = N_HEADS_A * LANES
    nka = N_KV_A * LANES
    o = 0
    qa = p[:, o:o + nqa]; o += nqa
    ka = p[:, o:o + nka]; o += nka
    va = p[:, o:o + nka]; o += nka
    cq = p[:, o:o + MLA_Q_RANK]; o += MLA_Q_RANK
    ckv = p[:, o:o + MLA_KV_RANK]; o += MLA_KV_RANK
    krg = p[:, o:o + LANES]

    qb = _dot(_rms(cq, qg_ref[...]).astype(BF16), wq_ref[...])
    ckv_n = _rms(ckv, kvg_ref[...]).astype(BF16)
    kb = _dot(ckv_n, wk_ref[...])
    vb = _dot(ckv_n, wv_ref[...])

    scale_a = HEAD_DIM_A ** -0.5
    scale_b = (MLA_NOPE + MLA_ROPE) ** -0.5
    if rope:
        ca, sa, cb, sb = ca_ref[...], sa_ref[...], cb_ref[...], sb_ref[...]
        half_a, half_b = HEAD_DIM_A // 2, MLA_ROPE // 2
        qa_g = [_rope_group(t, ca, sa, half_a, 0) * scale_a for t in _groups(qa)]
        ka_g = [_rope_group(t, ca, sa, half_a, 0) for t in _groups(ka)]
        qb_g = [_rope_group(t, cb, sb, half_b, MLA_NOPE) * scale_b for t in _groups(qb)]
        krg = _rope_group(krg, cb, sb, half_b, MLA_NOPE)
    else:
        qa_g = [t * scale_a for t in _groups(qa)]
        ka_g = _groups(ka)
        qb_g = [t * scale_b for t in _groups(qb)]
    kb_g = [t + krg for t in _groups(kb)]
    qa_ref[...] = jnp.concatenate(qa_g, axis=1).astype(BF16)
    ka_ref[...] = jnp.concatenate(ka_g, axis=1).astype(BF16)
    va_ref[...] = va.astype(BF16)
    qb_ref[...] = jnp.concatenate(qb_g, axis=1).astype(BF16)
    kb_ref[...] = jnp.concatenate(kb_g, axis=1).astype(BF16)
    vb_ref[...] = vb.astype(BF16)


def _row_tile(t, pref=512):
    tm = pref
    while t % tm:
        tm //= 2
    return tm


def _mod_index(rows_per_batch, tm, fixed):
    if fixed is not None:
        return lambda i: (fixed, 0, 0)
    return lambda i: ((i * tm) // rows_per_batch, 0, 0)


def _ab_in(x, g, sc, sh, w, tables, rows_per_batch, fixed_mod):
    t, d = x.shape
    tm = _row_tile(min(t, rows_per_batch))
    rope = tables is not None
    midx = _mod_index(rows_per_batch, tm, fixed_mod)
    full = lambda a: pl.BlockSpec(a.shape, lambda i: (0,) * a.ndim)
    in_specs = [
        pl.BlockSpec((tm, d), lambda i: (i, 0)),
        full(g),
        pl.BlockSpec((1, 1, d), midx),
        pl.BlockSpec((1, 1, d), midx),
    ] + [full(a) for a in w]
    args = [x, g, sc, sh] + list(w)
    if rope:
        nt = rows_per_batch // tm
        for tab in tables:
            in_specs.append(pl.BlockSpec((tm, LANES), lambda i: (i % nt, 0)))
            args.append(tab)
    widths = [N_HEADS_A * LANES, N_KV_A * LANES, N_KV_A * LANES,
              MLA_HEADS * LANES, MLA_HEADS * LANES, MLA_HEADS * LANES]
    return pl.pallas_call(
        functools.partial(_ab_in_kernel, rope=rope),
        out_shape=[jax.ShapeDtypeStruct((t, n), BF16) for n in widths],
        grid=(t // tm,),
        in_specs=in_specs,
        out_specs=[pl.BlockSpec((tm, n), lambda i: (i, 0)) for n in widths],
        compiler_params=_params("arbitrary"),
        name="ab_in_rope" if rope else "ab_in",
    )(*args)


def _c_in_kernel(*refs, rope):
    if rope:
        x_ref, g_ref, sc_ref, sh_ref, w_ref, qn_ref, kn_ref, cc_ref, sc2_ref, q_ref, k_ref, v_ref = refs
    else:
        x_ref, g_ref, sc_ref, sh_ref, w_ref, qn_ref, kn_ref, q_ref, k_ref, v_ref = refs
    h = _modnorm(x_ref[...], g_ref[...], sc_ref[0], sh_ref[0]).astype(BF16)
    p = _dot(h, w_ref[...])
    nq = N_HEADS_C * HEAD_DIM_C
    nk = N_KV_C * HEAD_DIM_C
    q_g = [_rms(t, qn_ref[...]) for t in _groups(p[:, :nq])]
    k_g = [_rms(t, kn_ref[...]) for t in _groups(p[:, nq:nq + nk])]
    scale = HEAD_DIM_C ** -0.5
    if rope:
        cc, ss = cc_ref[...], sc2_ref[...]
        q_g = [_rope_group(t, cc, ss, HEAD_DIM_C // 2, 0) for t in q_g]
        k_g = [_rope_group(t, cc, ss, HEAD_DIM_C // 2, 0) for t in k_g]
    q_ref[...] = jnp.concatenate([t * scale for t in q_g], axis=1).astype(BF16)
    k_ref[...] = jnp.concatenate(k_g, axis=1).astype(BF16)
    v_ref[...] = p[:, nq + nk:].astype(BF16)


def _c_in(x, g, sc, sh, w, qn, kn, tables, rows_per_batch, fixed_mod):
    t, d = x.shape
    tm = _row_tile(min(t, rows_per_batch))
    rope = tables is not None
    midx = _mod_index(rows_per_batch, tm, fixed_mod)
    full = lambda a: pl.BlockSpec(a.shape, lambda i: (0,) * a.ndim)
    in_specs = [
        pl.BlockSpec((tm, d), lambda i: (i, 0)),
        full(g),
        pl.BlockSpec((1, 1, d), midx),
        pl.BlockSpec((1, 1, d), midx),
        full(w), full(qn), full(kn),
    ]
    args = [x, g, sc, sh, w, qn, kn]
    if rope:
        nt = rows_per_batch // tm
        for tab in tables:
            in_specs.append(pl.BlockSpec((tm, LANES), lambda i: (i % nt, 0)))
            args.append(tab)
    widths = [N_HEADS_C * HEAD_DIM_C, N_KV_C * HEAD_DIM_C, N_KV_C * HEAD_DIM_C]
    return pl.pallas_call(
        functools.partial(_c_in_kernel, rope=rope),
        out_shape=[jax.ShapeDtypeStruct((t, n), BF16) for n in widths],
        grid=(t // tm,),
        in_specs=in_specs,
        out_specs=[pl.BlockSpec((tm, n), lambda i: (i, 0)) for n in widths],
        compiler_params=_params("arbitrary"),
        name="c_in_rope" if rope else "c_in",
    )(*args)


def _attn_kernel(*refs, n_group, has_lat, has_sink, window, tq, s_lat, kw):
    refs = list(refs)
    sink_ref = refs.pop(0) if has_sink else None
    q_ref = refs.pop(0)
    if has_lat:
        k1_ref, v1_ref = refs.pop(0), refs.pop(0)
    k2_ref, v2_ref, o_ref = refs
    hk = pl.program_id(1)
    qi = pl.program_id(2)
    k2 = k2_ref[...]
    v2 = v2_ref[...]
    if has_lat:
        if window:
            ws = pl.multiple_of(jnp.clip(qi * tq - WINDOW, 0, s_lat - kw), LANES)
            k1 = k1_ref[pl.ds(ws, kw), :]
            v1 = v1_ref[pl.ds(ws, kw), :]
            qpos = qi * tq + lax.broadcasted_iota(jnp.int32, (tq, kw), 0)
            kpos = ws + lax.broadcasted_iota(jnp.int32, (tq, kw), 1)
            ok = jnp.abs(kpos - qpos) <= WINDOW
        else:
            k1 = k1_ref[...]
            v1 = v1_ref[...]
    for g in range(n_group):
        q = q_ref[:, g * LANES:(g + 1) * LANES]
        s2 = _dot_t(q, k2)
        m = jnp.max(s2, axis=-1, keepdims=True)
        if has_lat:
            s1 = _dot_t(q, k1)
            if window:
                s1 = jnp.where(ok, s1, NEG_INF)
            m = jnp.maximum(m, jnp.max(s1, axis=-1, keepdims=True))
        if has_sink:
            sk = sink_ref[hk * n_group + g]
            m = jnp.maximum(m, sk)
        e2 = jnp.exp(s2 - m)
        den = jnp.sum(e2, axis=-1, keepdims=True)
        acc = _dot(e2.astype(BF16), v2)
        if has_lat:
            e1 = jnp.exp(s1 - m)
            den = den + jnp.sum(e1, axis=-1, keepdims=True)
            acc = acc + _dot(e1.astype(BF16), v1)
        if has_sink:
            den = den + jnp.exp(sk - m)
        o_ref[:, g * LANES:(g + 1) * LANES] = (acc / den).astype(BF16)


def _attention(q, k_lat, v_lat, k_ctx, v_ctx, sink, *, batch, n_kv, n_group, window, tq):
    tq_total = q.shape[0] // batch
    tq = min(tq, tq_total)
    n_ctx = k_ctx.shape[0] // batch
    has_lat = k_lat is not None
    has_sink = sink is not None
    s_lat = k_lat.shape[0] // batch if has_lat else 0
    kw = min(tq + 2 * WINDOW, s_lat) if window else 0
    nq = tq_total // tq
    gw = n_group * LANES
    in_specs, args = [], []
    if has_sink:
        in_specs.append(pl.BlockSpec(memory_space=pltpu.SMEM))
        args.append(sink)
    in_specs.append(pl.BlockSpec((tq, gw), lambda b, h, i: (b * nq + i, h)))
    args.append(q)
    if has_lat:
        in_specs += [pl.BlockSpec((s_lat, LANES), lambda b, h, i: (b, h))] * 2
        args += [k_lat, v_lat]
    in_specs += [pl.BlockSpec((n_ctx, LANES), lambda b, h, i: (b, h))] * 2
    args += [k_ctx, v_ctx]
    return pl.pallas_call(
        functools.partial(_attn_kernel, n_group=n_group, has_lat=has_lat, has_sink=has_sink,
                          window=window, tq=tq, s_lat=s_lat, kw=kw),
        out_shape=jax.ShapeDtypeStruct(q.shape, BF16),
        grid=(batch, n_kv, nq),
        in_specs=in_specs,
        out_specs=pl.BlockSpec((tq, gw), lambda b, h, i: (b * nq + i, h)),
        compiler_params=_params("arbitrary", "arbitrary", "arbitrary"),
        name="attn_win" if window else ("attn_dense" if has_lat else "attn_ctx"),
    )(*args)


def _out_kernel(*refs, n_parts, tm):
    o_refs = refs[:n_parts]
    w_refs = refs[n_parts:2 * n_parts]
    (x_ref, g1_ref, gn_ref, sc_ref, sh_ref, rw_ref, rb_ref,
     xo_ref, tok_ref, idx_ref, gate_ref) = refs[2 * n_parts:]
    y = _dot(o_refs[0][...], w_refs[0][...])
    for o_r, w_r in zip(o_refs[1:], w_refs[1:]):
        y = y + _dot(o_r[...], w_r[...])
    x = x_ref[...] + g1_ref[0] * y
    xo_ref[...] = x
    tok = _modnorm(x, gn_ref[...], sc_ref[0], sh_ref[0])
    for j in range(tok.shape[1] // LANES):
        tok_ref[pl.ds(j, tm, stride=SUBLANES), :] = tok[:, j * LANES:(j + 1) * LANES]
    logits = jnp.dot(tok, rw_ref[...], preferred_element_type=F32, precision=HIGHEST) + rb_ref[...]
    lane = lax.broadcasted_iota(jnp.int32, logits.shape, 1).astype(F32)
    vals, idxs = [], []
    l = logits
    for _ in range(TOP_K):
        m = jnp.max(l, axis=-1, keepdims=True)
        idx = jnp.min(jnp.where(l == m, lane, float(LANES)), axis=-1, keepdims=True)
        vals.append(m)
        idxs.append(idx)
        l = jnp.where(lane == idx, -jnp.inf, l)
    es = [jnp.exp(v - vals[0]) for v in vals]
    den = es[0]
    for e in es[1:]:
        den = den + e
    lane8 = lax.broadcasted_iota(jnp.int32, (tm, SUBLANES), 1)
    io = jnp.zeros((tm, SUBLANES), jnp.int32)
    go = jnp.zeros((tm, SUBLANES), F32)
    for k in range(TOP_K):
        io = jnp.where(lane8 == k, idxs[k].astype(jnp.int32), io)
        go = jnp.where(lane8 == k, es[k] / den, go)
    idx_ref[...] = io
    gate_ref[...] = go


def _out_proj(o_parts, w_parts, x, g1, gn, sc, sh, rw, rb, rows_per_batch, fixed_mod):
    t, d = x.shape
    tm = _row_tile(min(t, rows_per_batch), 256)
    midx = _mod_index(rows_per_batch, tm, fixed_mod)
    full = lambda a: pl.BlockSpec(a.shape, lambda i: (0,) * a.ndim)
    n_parts = len(o_parts)
    in_specs = ([pl.BlockSpec((tm, o.shape[1]), lambda i: (i, 0)) for o in o_parts]
                + [full(w) for w in w_parts]
                + [pl.BlockSpec((tm, d), lambda i: (i, 0)),
                   pl.BlockSpec((1, 1, d), midx), full(gn),
                   pl.BlockSpec((1, 1, d), midx), pl.BlockSpec((1, 1, d), midx),
                   full(rw), full(rb)])
    return pl.pallas_call(
        functools.partial(_out_kernel, n_parts=n_parts, tm=tm),
        out_shape=[jax.ShapeDtypeStruct((t, d), F32),
                   jax.ShapeDtypeStruct((t * SUBLANES, d // SUBLANES), F32),
                   jax.ShapeDtypeStruct((t, SUBLANES), jnp.int32),
                   jax.ShapeDtypeStruct((t, SUBLANES), F32)],
        grid=(t // tm,),
        in_specs=in_specs,
        out_specs=[pl.BlockSpec((tm, d), lambda i: (i, 0)),
                   pl.BlockSpec((tm * SUBLANES, d // SUBLANES), lambda i: (i, 0)),
                   pl.BlockSpec((tm, SUBLANES), lambda i: (i, 0)),
                   pl.BlockSpec((tm, SUBLANES), lambda i: (i, 0))],
        compiler_params=_params("arbitrary"),
        name="out_proj",
    )(*o_parts, *w_parts, x, g1, gn, sc, sh, rw, rb)


PAIR_TILE = 2 * LANES


def _wprep_kernel(p_ref, w1_ref, w2_ref, o1_ref, o2_ref):
    perm = p_ref[...]
    for j in range(w1_ref.shape[2] // PAIR_TILE):
        cols = slice(j * PAIR_TILE, (j + 1) * PAIR_TILE)
        o1_ref[0, :, cols] = _dot(w1_ref[0, :, cols].astype(BF16), perm).astype(BF16)
    o2_ref[0] = w2_ref[0].astype(BF16)


def _expert_weights(w_in, w_out):
    e, d, f2 = w_in.shape
    i = jnp.arange(LANES)
    perm = jnp.zeros((PAIR_TILE, PAIR_TILE), F32).at[2 * i, i].set(1.0).at[2 * i + 1, LANES + i].set(1.0)
    return pl.pallas_call(
        _wprep_kernel,
        out_shape=[jax.ShapeDtypeStruct(w_in.shape, BF16), jax.ShapeDtypeStruct(w_out.shape, BF16)],
        grid=(e,),
        in_specs=[pl.BlockSpec((PAIR_TILE, PAIR_TILE), lambda n: (0, 0)),
                  pl.BlockSpec((1, d, f2), lambda n: (n, 0, 0)),
                  pl.BlockSpec((1, f2 // 2, d), lambda n: (n, 0, 0))],
        out_specs=[pl.BlockSpec((1, d, f2), lambda n: (n, 0, 0)),
                   pl.BlockSpec((1, f2 // 2, d), lambda n: (n, 0, 0))],
        compiler_params=_params("arbitrary"),
        name="expert_weights",
    )(perm.astype(BF16), w_in, w_out)


DISPATCH_TILE = 512


def _row(ref, r):
    return ref.at[pl.ds(pl.multiple_of(r * SUBLANES, SUBLANES), SUBLANES)]


def _dispatch_kernel(plo_ref, phi_ref, dest_ref, tok_ref, xs_ref, zero_ref, sem, zsem, *, tm):
    @pl.when(pl.program_id(0) == 0)
    def _():
        zero_ref[...] = jnp.zeros(zero_ref.shape, zero_ref.dtype)

        def fill_range(e, carry):
            lo, hi = plo_ref[e], phi_ref[e]

            def fill_chunk(c, carry):
                c_lo = lo + c * tm
                c_hi = jnp.minimum(c_lo + tm, hi)

                def start_row(r, carry):
                    pltpu.make_async_copy(zero_ref, _row(xs_ref, r), zsem).start()
                    return carry

                def wait_row(r, carry):
                    pltpu.make_async_copy(zero_ref, _row(xs_ref, r), zsem).wait()
                    return carry

                lax.fori_loop(c_lo, c_hi, start_row, carry)
                return lax.fori_loop(c_lo, c_hi, wait_row, carry)

            return lax.fori_loop(0, (hi - lo + tm - 1) // tm, fill_chunk, carry)

        lax.fori_loop(0, plo_ref.shape[0], fill_range, 0)

    def body(t, carry):
        for u in range(2):
            r = t * 2 + u
            src = _row(tok_ref, r)
            for k in range(TOP_K):
                pltpu.make_async_copy(src, _row(xs_ref, dest_ref[0, 0, r * TOP_K + k]), sem).start()
        return carry

    lax.fori_loop(0, tm // 2, body, 0)
    done = xs_ref.at[pl.ds(0, tm * TOP_K * SUBLANES)]
    pltpu.make_async_copy(done, done, sem).wait()


def _dispatch(tokens_rows, dest, pad_lo, pad_hi, cap):
    t = dest.shape[0]
    tm = _row_tile(t, DISPATCH_TILE)
    grid_spec = pltpu.PrefetchScalarGridSpec(
        num_scalar_prefetch=2,
        grid=(t // tm,),
        in_specs=[pl.BlockSpec((1, 1, tm * TOP_K), lambda i, lo, hi: (i, 0, 0), memory_space=pltpu.SMEM),
                  pl.BlockSpec((tm * SUBLANES, LANES), lambda i, lo, hi: (i, 0))],
        out_specs=pl.BlockSpec(memory_space=pl.ANY),
        scratch_shapes=[pltpu.VMEM((SUBLANES, LANES), F32),
                        pltpu.SemaphoreType.DMA, pltpu.SemaphoreType.DMA],
    )
    return pl.pallas_call(
        functools.partial(_dispatch_kernel, tm=tm),
        out_shape=jax.ShapeDtypeStruct((cap * SUBLANES, LANES), F32),
        grid_spec=grid_spec,
        compiler_params=_params("arbitrary"),
        name="dispatch",
    )(pad_lo, pad_hi, dest.reshape(t // tm, 1, tm * TOP_K), tokens_rows)


EXPERT_BLOCK = 512


def _expert_kernel(be_ref, nb_ref, x_ref, w1_ref, b1_ref, w2_ref, b2_ref, o_ref, *, bm):
    del be_ref
    b = pl.program_id(0)
    n_j = w1_ref.shape[1] // LANES

    @pl.when(b < nb_ref[0])
    def _():
        x = jnp.concatenate([x_ref[pl.ds(j, bm, stride=SUBLANES), :] for j in range(n_j)], axis=1)
        u = _dot(x.astype(BF16), w1_ref[0]) + b1_ref[0]
        acts = []
        for j in range(u.shape[1] // PAIR_TILE):
            glu = jnp.minimum(u[:, j * PAIR_TILE:j * PAIR_TILE + LANES], SWIGLU_LIMIT)
            lin = jnp.clip(u[:, j * PAIR_TILE + LANES:(j + 1) * PAIR_TILE], -SWIGLU_LIMIT, SWIGLU_LIMIT)
            acts.append(glu * jax.nn.sigmoid(SWIGLU_ALPHA * glu) * (lin + 1.0))
        a = jnp.concatenate(acts, axis=1)
        y = _dot(a.astype(BF16), w2_ref[0]) + b2_ref[0]
        for j in range(y.shape[1] // LANES):
            o_ref[pl.ds(j, bm, stride=SUBLANES), :] = y[:, j * LANES:(j + 1) * LANES]

    @pl.when(b >= nb_ref[0])
    def _():
        o_ref[...] = jnp.zeros(o_ref.shape, o_ref.dtype)


def _expert_mlp(xs, blk_expert, n_used, w1, b1, w2, b2):
    bm = EXPERT_BLOCK
    nblk = blk_expert.shape[0]
    _, d, f2 = w1.shape
    grid_spec = pltpu.PrefetchScalarGridSpec(
        num_scalar_prefetch=2,
        grid=(nblk,),
        in_specs=[
            pl.BlockSpec((bm * SUBLANES, LANES), lambda b, be, nb: (b, 0)),
            pl.BlockSpec((1, d, f2), lambda b, be, nb: (be[b], 0, 0)),
            pl.BlockSpec((1, 1, f2), lambda b, be, nb: (be[b], 0, 0)),
            pl.BlockSpec((1, f2 // 2, d), lambda b, be, nb: (be[b], 0, 0)),
            pl.BlockSpec((1, 1, d), lambda b, be, nb: (be[b], 0, 0)),
        ],
        out_specs=pl.BlockSpec((bm * SUBLANES, LANES), lambda b, be, nb: (b, 0)),
    )
    return pl.pallas_call(
        functools.partial(_expert_kernel, bm=bm),
        out_shape=jax.ShapeDtypeStruct(xs.shape, F32),
        grid_spec=grid_spec,
        compiler_params=_params("arbitrary"),
        name="expert_mlp",
    )(blk_expert, n_used, xs, w1, b1, w2, b2)


def _combine_kernel(*refs, tm, final):
    if final:
        dest_ref, x_ref, ys_ref, gate_ref, g2_ref, fg_ref, o_ref, ybuf, sem = refs
    else:
        dest_ref, x_ref, ys_ref, gate_ref, g2_ref, o_ref, ybuf, sem = refs

    def body(t, carry):
        for u in range(2):
            r = t * 2 + u
            for k in range(TOP_K):
                pltpu.make_async_copy(_row(ys_ref, dest_ref[0, 0, r * TOP_K + k]),
                                      _row(ybuf, k * tm + r), sem).start()
        return carry

    lax.fori_loop(0, tm // 2, body, 0)
    pltpu.make_async_copy(ybuf, ybuf, sem).wait()

    gate = gate_ref[...]
    parts = []
    for j in range(x_ref.shape[1] // LANES):
        f = gate[:, 0:1] * ybuf[pl.ds(j, tm, stride=SUBLANES), :]
        for k in range(1, TOP_K):
            f = f + gate[:, k:k + 1] * ybuf[pl.ds(k * tm * SUBLANES + j, tm, stride=SUBLANES), :]
        parts.append(f)
    x = x_ref[...] + g2_ref[0] * jnp.concatenate(parts, axis=1)
    if final:
        x = _rms(x, fg_ref[...])
    o_ref[...] = x


def _combine(x, ys, dest, gates, g2, final_g, *, row_off, rows_per_batch, fixed_mod):
    t, d = x.shape
    tm = _row_tile(min(t, rows_per_batch), 256)
    midx = _mod_index(rows_per_batch, tm, fixed_mod)
    final = final_g is not None
    off = row_off // tm
    t_all = dest.shape[0]
    in_specs = [pl.BlockSpec((1, 1, tm * TOP_K), lambda i: (off + i, 0, 0), memory_space=pltpu.SMEM),
                pl.BlockSpec((tm, d), lambda i: (i, 0)),
                pl.BlockSpec(memory_space=pl.ANY),
                pl.BlockSpec((tm, SUBLANES), lambda i: (off + i, 0)),
                pl.BlockSpec((1, 1, d), midx)]
    args = [dest.reshape(t_all // tm, 1, tm * TOP_K), x, ys, gates, g2]
    if final:
        in_specs.append(pl.BlockSpec(final_g.shape, lambda i: (0, 0)))
        args.append(final_g)
    return pl.pallas_call(
        functools.partial(_combine_kernel, tm=tm, final=final),
        out_shape=jax.ShapeDtypeStruct((t, d), F32),
        grid=(t // tm,),
        in_specs=in_specs,
        out_specs=pl.BlockSpec((tm, d), lambda i: (i, 0)),
        scratch_shapes=[pltpu.VMEM((TOP_K * tm * SUBLANES, LANES), F32), pltpu.SemaphoreType.DMA],
        compiler_params=_params("arbitrary"),
        name="combine_final" if final else "combine",
    )(*args)


def _pad_heads(w, n_heads, axis):
    shape = list(w.shape)
    hd = shape[axis] // n_heads
    w = w.reshape(shape[:axis] + [n_heads, hd] + shape[axis + 1:])
    pad = [(0, 0)] * w.ndim
    pad[axis + 1] = (0, LANES - hd)
    w = jnp.pad(w, pad)
    shape[axis] = n_heads * LANES
    return w.reshape(shape)


def _rope_tables(s, rot_dim, lo):
    pos = jnp.arange(s, dtype=jnp.int32)
    rows, cols = (pos // GRID_W).astype(F32), (pos % GRID_W).astype(F32)
    quarter = rot_dim // 4
    inv = ROPE_THETA ** (-jnp.arange(quarter, dtype=F32) / quarter)
    ang = jnp.concatenate([rows[:, None] * inv, cols[:, None] * inv], axis=-1)
    cos, sin = jnp.cos(ang), jnp.sin(ang)
    hi = LANES - lo - rot_dim
    cos_t = jnp.concatenate([jnp.ones((s, lo), F32), cos, cos, jnp.ones((s, hi), F32)], axis=1)
    sin_t = jnp.concatenate([jnp.zeros((s, lo), F32), -sin, sin, jnp.zeros((s, hi), F32)], axis=1)
    return cos_t, sin_t


def _moe_plan(eidx, bm):
    t = eidx.shape[0]
    ids = jnp.arange(N_EXPERTS, dtype=jnp.int32)
    sel = eidx[:, :, None] == ids
    hit = sel.astype(jnp.int32).sum(1)
    incl = jnp.cumsum(hit, axis=0)
    counts = incl[-1]
    rank = jnp.where(sel, (incl - hit)[:, None, :], 0).sum(-1)
    padded = (counts + bm - 1) // bm * bm
    pends = jnp.cumsum(padded)
    pstarts = pends - padded
    dest = jnp.where(sel, pstarts, 0).sum(-1) + rank
    n_assign = t * TOP_K
    nblk = -(-(n_assign + N_EXPERTS * (bm - 1)) // bm)
    blk_start = jnp.arange(nblk, dtype=jnp.int32) * bm
    blk_expert = jnp.minimum(jnp.searchsorted(pends, blk_start, side="right"), N_EXPERTS - 1)
    n_used = (pends[-1] // bm).astype(jnp.int32).reshape(1)
    pad_lo = jnp.concatenate([pstarts + counts, pends[-1:]]).astype(jnp.int32)
    pad_hi = jnp.concatenate([pends, jnp.full((1,), nblk * bm, pends.dtype)]).astype(jnp.int32)
    return dest.astype(jnp.int32), blk_expert.astype(jnp.int32), n_used, pad_lo, pad_hi


def _moe(tokens_rows, eidx, w1, b1, w2, b2):
    dest, blk_expert, n_used, pad_lo, pad_hi = _moe_plan(eidx, EXPERT_BLOCK)
    xs = _dispatch(tokens_rows, dest, pad_lo, pad_hi, blk_expert.shape[0] * EXPERT_BLOCK)
    return _expert_mlp(xs, blk_expert, n_used, w1, b1, w2, b2), dest


def kernel(x, c, ctx, c_ctx, ada_w, ada_b, norm_mix_g, norm_ffn_g, ab_w_in, mla_q_norm_g, mla_wq_b,
           mla_kv_norm_g, mla_wkv_b, swa_sink, ab_w_out, c_w_in, c_q_norm_g, c_k_norm_g, c_w_out,
           router_w, router_b, moe_w_in, moe_b_in, moe_w_out, moe_b_out, final_norm_g):
    bn, s, d = x.shape
    n_ctx = ctx.shape[1]
    depth = ada_w.shape[0]
    t_lat, t_ctx = bn * s, bn * n_ctx
    row = lambda v: v.reshape(1, -1)

    n_mod = -(-(bn + 1) // SUBLANES) * SUBLANES
    c_all = jnp.concatenate([c, c_ctx[None], jnp.zeros((n_mod - bn - 1, d), F32)], axis=0)
    mod = _modulation(c_all, ada_w, ada_b).reshape(depth, n_mod, 6, 1, d)
    ctx_row = bn

    tab_a = _rope_tables(s, HEAD_DIM_A, 0)
    tab_b = _rope_tables(s, MLA_ROPE, MLA_NOPE)
    tab_c = _rope_tables(s, HEAD_DIM_C, 0)

    xl = x.reshape(t_lat, d)
    xc = ctx.reshape(t_ctx, d)
    for i in range(depth):
        with_ctx = i < depth - 1
        j = i // 2
        sh1, sc1, g1, sh2, sc2, g2 = [mod[i, :, m] for m in range(6)]
        gmix, gffn = row(norm_mix_g[i]), row(norm_ffn_g[i])
        if i % 2 == 0:
            w_in = ab_w_in[j]
            sizes = [N_HEADS_A * HEAD_DIM_A, N_KV_A * HEAD_DIM_A, N_KV_A * HEAD_DIM_A,
                     MLA_Q_RANK, MLA_KV_RANK, MLA_ROPE]
            offs = [0]
            for n in sizes:
                offs.append(offs[-1] + n)
            cols = [w_in[:, offs[m]:offs[m + 1]] for m in range(6)]
            krg = jnp.pad(cols[5], ((0, 0), (MLA_NOPE, LANES - MLA_NOPE - MLA_ROPE)))
            w1 = jnp.concatenate([_pad_heads(cols[0], N_HEADS_A, 1), _pad_heads(cols[1], N_KV_A, 1),
                                  _pad_heads(cols[2], N_KV_A, 1), cols[3], cols[4], krg], axis=1).astype(BF16)
            wq = _pad_heads(mla_wq_b[j], MLA_HEADS, 1).astype(BF16)
            wkv = mla_wkv_b[j].reshape(MLA_KV_RANK, MLA_HEADS, MLA_NOPE + MLA_V)
            wk = _pad_heads(wkv[:, :, :MLA_NOPE].reshape(MLA_KV_RANK, -1), MLA_HEADS, 1).astype(BF16)
            wv = _pad_heads(wkv[:, :, MLA_NOPE:].reshape(MLA_KV_RANK, -1), MLA_HEADS, 1).astype(BF16)
            ws = [w1, row(mla_q_norm_g[j]), wq, row(mla_kv_norm_g[j]), wk, wv]
            na = N_HEADS_A * HEAD_DIM_A
            wo_a = _pad_heads(ab_w_out[j][:na], N_HEADS_A, 0).astype(BF16)
            wo_b = _pad_heads(ab_w_out[j][na:], MLA_HEADS, 0).astype(BF16)
            sink = swa_sink[j]

            qa, ka, va, qb, kb, vb = _ab_in(xl, gmix, sc1, sh1, ws, tab_a + tab_b, s, None)
            qa_c, ka_c, va_c, qb_c, kb_c, vb_c = _ab_in(xc, gmix, sc1, sh1, ws, None, n_ctx, ctx_row)
            ga = N_HEADS_A // N_KV_A
            oa = _attention(qa, ka, va, ka_c, va_c, sink, batch=bn, n_kv=N_KV_A, n_group=ga,
                            window=True, tq=256)
            ob = _attention(qb, kb, vb, kb_c, vb_c, None, batch=bn, n_kv=MLA_HEADS, n_group=1,
                            window=False, tq=512)
            o_lat, w_o = [oa, ob], [wo_a, wo_b]
            if with_ctx:
                oa_c = _attention(qa_c, None, None, ka_c, va_c, sink, batch=bn, n_kv=N_KV_A,
                                  n_group=ga, window=False, tq=256)
                ob_c = _attention(qb_c, None, None, kb_c, vb_c, None, batch=bn, n_kv=MLA_HEADS,
                                  n_group=1, window=False, tq=256)
                o_ctx = [oa_c, ob_c]
        else:
            w_in = c_w_in[j].astype(BF16)
            qn, kn = row(c_q_norm_g[j]), row(c_k_norm_g[j])
            q, k, v = _c_in(xl, gmix, sc1, sh1, w_in, qn, kn, tab_c, s, None)
            q_c, k_c, v_c = _c_in(xc, gmix, sc1, sh1, w_in, qn, kn, None, n_ctx, ctx_row)
            gc = N_HEADS_C // N_KV_C
            o = _attention(q, k, v, k_c, v_c, None, batch=bn, n_kv=N_KV_C, n_group=gc,
                           window=False, tq=256)
            o_lat, w_o = [o], [c_w_out[j].astype(BF16)]
            if with_ctx:
                o_ctx = [_attention(q_c, None, None, k_c, v_c, None, batch=bn, n_kv=N_KV_C,
                                    n_group=gc, window=False, tq=256)]

        rw = jnp.pad(router_w[i], ((0, 0), (0, LANES - N_EXPERTS)))
        rb = jnp.concatenate([router_b[i], jnp.full((LANES - N_EXPERTS,), NEG_INF, F32)]).reshape(1, LANES)
        xl, tok_l, idx_l, gate_l = _out_proj(o_lat, w_o, xl, g1, gffn, sc2, sh2, rw, rb, s, None)
        if with_ctx:
            xc, tok_c, idx_c, gate_c = _out_proj(o_ctx, w_o, xc, g1, gffn, sc2, sh2, rw, rb, n_ctx, ctx_row)
            tokens = jnp.concatenate([tok_l, tok_c], axis=0)
            eidx = jnp.concatenate([idx_l, idx_c], axis=0)
            gates = jnp.concatenate([gate_l, gate_c], axis=0)
        else:
            tokens, eidx, gates = tok_l, idx_l, gate_l

        w1e, w2e = _expert_weights(moe_w_in[i], moe_w_out[i])
        n_tiles = moe_b_in.shape[-1] // PAIR_TILE
        b1e = moe_b_in[i].reshape(N_EXPERTS, n_tiles, LANES, 2).transpose(0, 1, 3, 2)
        b1e = b1e.reshape(N_EXPERTS, 1, -1)
        b2e = moe_b_out[i][:, None, :]
        ys, dest = _moe(tokens, eidx[:, :TOP_K], w1e, b1e, w2e, b2e)

        last = i == depth - 1
        xl = _combine(xl, ys, dest, gates, g2, row(final_norm_g) if last else None,
                      row_off=0, rows_per_batch=s, fixed_mod=None)
        if with_ctx:
            xc = _combine(xc, ys, dest, gates, g2, None,
                          row_off=t_lat, rows_per_batch=n_ctx, fixed_mod=ctx_row)
    return xl.reshape(bn, s, d)
```

```python
import functools

import jax
import jax.numpy as jnp
from jax import lax
from jax.experimental import pallas as pl
from jax.experimental.pallas import tpu as pltpu

GRID_W = 64
N_HEADS_A, N_KV_A, HEAD_DIM_A, WINDOW = 8, 2, 64, 128
MLA_HEADS, MLA_Q_RANK, MLA_KV_RANK, MLA_NOPE, MLA_ROPE, MLA_V = 8, 384, 256, 64, 32, 64
N_HEADS_C, N_KV_C, HEAD_DIM_C = 8, 2, 128
N_EXPERTS, TOP_K = 32, 4
MLA_STEP_HEADS = 4
SWIGLU_ALPHA, SWIGLU_LIMIT = 1.702, 7.0
ROPE_THETA, RMS_EPS, NEG_INF = 10000.0, 1e-6, -1e30

LANES = 128
SUBLANES = 8
VMEM_LIMIT_BYTES = 56 * 1024 * 1024

F32 = jnp.float32
BF16 = jnp.bfloat16
HIGHEST = lax.Precision.HIGHEST


def _params(*sem):
    return pltpu.CompilerParams(dimension_semantics=sem, vmem_limit_bytes=VMEM_LIMIT_BYTES)


def _dot(a, b):
    return jnp.dot(a, b, preferred_element_type=F32)


def _dot_t(a, b):
    return lax.dot_general(a, b, (((1,), (1,)), ((), ())), preferred_element_type=F32)


def _rms(x, g):
    return x * lax.rsqrt(jnp.mean(x * x, axis=-1, keepdims=True) + RMS_EPS) * g


def _modnorm(x, g, sc, sh):
    return _rms(x, g) * (1.0 + sc) + sh


def _rope_group(x, cos, sin, half, lo):
    lane = lax.broadcasted_iota(jnp.int32, x.shape, 1)
    first = (lane >= lo) & (lane < lo + half)
    rot = jnp.where(first, pltpu.roll(x, LANES - half, 1), pltpu.roll(x, half, 1))
    return x * cos + rot * sin


def _groups(x):
    return [x[:, i * LANES:(i + 1) * LANES] for i in range(x.shape[1] // LANES)]


def _mod_kernel(c_ref, w_ref, b_ref, o_ref):
    c = c_ref[...]
    a = c * jax.nn.sigmoid(c)
    o_ref[0] = jnp.dot(a, w_ref[0], preferred_element_type=F32, precision=HIGHEST) + b_ref[0]


def _modulation(c_all, ada_w, ada_b):
    depth, d, n = ada_w.shape
    r = c_all.shape[0]
    nt = n // 4
    return pl.pallas_call(
        _mod_kernel,
        out_shape=jax.ShapeDtypeStruct((depth, r, n), F32),
        grid=(depth, n // nt),
        in_specs=[
            pl.BlockSpec((r, d), lambda i, j: (0, 0)),
            pl.BlockSpec((1, d, nt), lambda i, j: (i, 0, j)),
            pl.BlockSpec((1, 1, nt), lambda i, j: (i, 0, j)),
        ],
        out_specs=pl.BlockSpec((1, r, nt), lambda i, j: (i, 0, j)),
        compiler_params=_params("arbitrary", "arbitrary"),
        name="modulation",
    )(c_all, ada_w, ada_b.reshape(depth, 1, n))


def _ab_in_kernel(*refs, rope):
    if rope:
        (x_ref, g_ref, sc_ref, sh_ref, w1_ref, qg_ref, wq_ref, kvg_ref, wk_ref, wv_ref,
         ca_ref, sa_ref, cb_ref, sb_ref, qa_ref, ka_ref, va_ref, qb_ref, kb_ref, vb_ref) = refs
    else:
        (x_ref, g_ref, sc_ref, sh_ref, w1_ref, qg_ref, wq_ref, kvg_ref, wk_ref, wv_ref,
         qa_ref, ka_ref, va_ref, qb_ref, kb_ref, vb_ref) = refs
    h = _modnorm(x_ref[...], g_ref[...], sc_ref[0], sh_ref[0]).astype(BF16)
    p = _dot(h, w1_ref[...])
    nqa = N_HEADS_A * LANES
    nka = N_KV_A * LANES
    o = 0
    qa = p[:, o:o + nqa]; o += nqa
    ka = p[:, o:o + nka]; o += nka
    va = p[:, o:o + nka]; o += nka
    cq = p[:, o:o + MLA_Q_RANK]; o += MLA_Q_RANK
    ckv = p[:, o:o + MLA_KV_RANK]; o += MLA_KV_RANK
    krg = p[:, o:o + LANES]

    qb = _dot(_rms(cq, qg_ref[...]).astype(BF16), wq_ref[...])
    ckv_n = _rms(ckv, kvg_ref[...]).astype(BF16)
    kb = _dot(ckv_n, wk_ref[...])
    vb = _dot(ckv_n, wv_ref[...])

    scale_a = HEAD_DIM_A ** -0.5
    scale_b = (MLA_NOPE + MLA_ROPE) ** -0.5
    if rope:
        ca, sa, cb, sb = ca_ref[...], sa_ref[...], cb_ref[...], sb_ref[...]
        half_a, half_b = HEAD_DIM_A // 2, MLA_ROPE // 2
        qa_g = [_rope_group(t, ca, sa, half_a, 0) * scale_a for t in _groups(qa)]
        ka_g = [_rope_group(t, ca, sa, half_a, 0) for t in _groups(ka)]
        qb_g = [_rope_group(t, cb, sb, half_b, MLA_NOPE) * scale_b for t in _groups(qb)]
        krg = _rope_group(krg, cb, sb, half_b, MLA_NOPE)
    else:
        qa_g = [t * scale_a for t in _groups(qa)]
        ka_g = _groups(ka)
        qb_g = [t * scale_b for t in _groups(qb)]
    kb_g = [t + krg for t in _groups(kb)]
    qa_ref[...] = jnp.concatenate(qa_g, axis=1).astype(BF16)
    ka_ref[...] = jnp.concatenate(ka_g, axis=1).astype(BF16)
    va_ref[...] = va.astype(BF16)
    qb_ref[...] = jnp.concatenate(qb_g, axis=1).astype(BF16)
    kb_ref[...] = jnp.concatenate(kb_g, axis=1).astype(BF16)
    vb_ref[...] = vb.astype(BF16)


def _row_tile(t, pref=512):
    tm = pref
    while t % tm:
        tm //= 2
    return tm


def _mod_index(rows_per_batch, tm, fixed):
    if fixed is not None:
        return lambda i: (fixed, 0, 0)
    return lambda i: ((i * tm) // rows_per_batch, 0, 0)


def _ab_in(x, g, sc, sh, w, tables, rows_per_batch, fixed_mod):
    t, d = x.shape
    tm = _row_tile(min(t, rows_per_batch))
    rope = tables is not None
    midx = _mod_index(rows_per_batch, tm, fixed_mod)
    full = lambda a: pl.BlockSpec(a.shape, lambda i: (0,) * a.ndim)
    in_specs = [
        pl.BlockSpec((tm, d), lambda i: (i, 0)),
        full(g),
        pl.BlockSpec((1, 1, d), midx),
        pl.BlockSpec((1, 1, d), midx),
    ] + [full(a) for a in w]
    args = [x, g, sc, sh] + list(w)
    if rope:
        nt = rows_per_batch // tm
        for tab in tables:
            in_specs.append(pl.BlockSpec((tm, LANES), lambda i: (i % nt, 0)))
            args.append(tab)
    widths = [N_HEADS_A * LANES, N_KV_A * LANES, N_KV_A * LANES,
              MLA_HEADS * LANES, MLA_HEADS * LANES, MLA_HEADS * LANES]
    return pl.pallas_call(
        functools.partial(_ab_in_kernel, rope=rope),
        out_shape=[jax.ShapeDtypeStruct((t, n), BF16) for n in widths],
        grid=(t // tm,),
        in_specs=in_specs,
        out_specs=[pl.BlockSpec((tm, n), lambda i: (i, 0)) for n in widths],
        compiler_params=_params("arbitrary"),
        name="ab_in_rope" if rope else "ab_in",
    )(*args)


def _c_in_kernel(*refs, rope):
    if rope:
        x_ref, g_ref, sc_ref, sh_ref, w_ref, qn_ref, kn_ref, cc_ref, sc2_ref, q_ref, k_ref, v_ref = refs
    else:
        x_ref, g_ref, sc_ref, sh_ref, w_ref, qn_ref, kn_ref, q_ref, k_ref, v_ref = refs
    h = _modnorm(x_ref[...], g_ref[...], sc_ref[0], sh_ref[0]).astype(BF16)
    p = _dot(h, w_ref[...])
    nq = N_HEADS_C * HEAD_DIM_C
    nk = N_KV_C * HEAD_DIM_C
    q_g = [_rms(t, qn_ref[...]) for t in _groups(p[:, :nq])]
    k_g = [_rms(t, kn_ref[...]) for t in _groups(p[:, nq:nq + nk])]
    scale = HEAD_DIM_C ** -0.5
    if rope:
        cc, ss = cc_ref[...], sc2_ref[...]
        q_g = [_rope_group(t, cc, ss, HEAD_DIM_C // 2, 0) for t in q_g]
        k_g = [_rope_group(t, cc, ss, HEAD_DIM_C // 2, 0) for t in k_g]
    q_ref[...] = jnp.concatenate([t * scale for t in q_g], axis=1).astype(BF16)
    k_ref[...] = jnp.concatenate(k_g, axis=1).astype(BF16)
    v_ref[...] = p[:, nq + nk:].astype(BF16)


def _c_in(x, g, sc, sh, w, qn, kn, tables, rows_per_batch, fixed_mod):
    t, d = x.shape
    tm = _row_tile(min(t, rows_per_batch))
    rope = tables is not None
    midx = _mod_index(rows_per_batch, tm, fixed_mod)
    full = lambda a: pl.BlockSpec(a.shape, lambda i: (0,) * a.ndim)
    in_specs = [
        pl.BlockSpec((tm, d), lambda i: (i, 0)),
        full(g),
        pl.BlockSpec((1, 1, d), midx),
        pl.BlockSpec((1, 1, d), midx),
        full(w), full(qn), full(kn),
    ]
    args = [x, g, sc, sh, w, qn, kn]
    if rope:
        nt = rows_per_batch // tm
        for tab in tables:
            in_specs.append(pl.BlockSpec((tm, LANES), lambda i: (i % nt, 0)))
            args.append(tab)
    widths = [N_HEADS_C * HEAD_DIM_C, N_KV_C * HEAD_DIM_C, N_KV_C * HEAD_DIM_C]
    return pl.pallas_call(
        functools.partial(_c_in_kernel, rope=rope),
        out_shape=[jax.ShapeDtypeStruct((t, n), BF16) for n in widths],
        grid=(t // tm,),
        in_specs=in_specs,
        out_specs=[pl.BlockSpec((tm, n), lambda i: (i, 0)) for n in widths],
        compiler_params=_params("arbitrary"),
        name="c_in_rope" if rope else "c_in",
    )(*args)


def _attn_kernel(*refs, n_kv, n_group, out_dim, has_lat, has_sink, window, tq, s_lat, kw):
    refs = list(refs)
    sink_ref = refs.pop(0) if has_sink else None
    q_ref = refs.pop(0)
    if has_lat:
        k1_ref, v1_ref = refs.pop(0), refs.pop(0)
    k2_ref, v2_ref, o_ref = refs
    hk0 = pl.program_id(1) * n_kv
    qi = pl.program_id(2)
    if window:
        ws = pl.multiple_of(jnp.clip(qi * tq - WINDOW, 0, s_lat - kw), LANES)
        rows = pl.ds(ws, kw)
        qpos = qi * tq + lax.broadcasted_iota(jnp.int32, (tq, kw), 0)
        kpos = ws + lax.broadcasted_iota(jnp.int32, (tq, kw), 1)
        ok = jnp.abs(kpos - qpos) <= WINDOW
    else:
        rows = slice(None)
    outs = []
    for j in range(n_kv):
        lanes = slice(j * LANES, (j + 1) * LANES)
        k2, v2 = k2_ref[:, lanes], v2_ref[:, lanes]
        if has_lat:
            k1, v1 = k1_ref[rows, lanes], v1_ref[rows, lanes]
        for g in range(n_group):
            h = j * n_group + g
            q = q_ref[:, h * LANES:(h + 1) * LANES]
            s2 = _dot_t(q, k2)
            m = jnp.max(s2, axis=-1, keepdims=True)
            if has_lat:
                s1 = _dot_t(q, k1)
                if window:
                    s1 = jnp.where(ok, s1, NEG_INF)
                m = jnp.maximum(m, jnp.max(s1, axis=-1, keepdims=True))
            if has_sink:
                sk = sink_ref[hk0 * n_group + h]
                m = jnp.maximum(m, sk)
            e2 = jnp.exp(s2 - m)
            den = jnp.sum(e2, axis=-1, keepdims=True)
            acc = _dot(e2.astype(BF16), v2)
            if has_lat:
                e1 = jnp.exp(s1 - m)
                den = den + jnp.sum(e1, axis=-1, keepdims=True)
                acc = acc + _dot(e1.astype(BF16), v1)
            if has_sink:
                den = den + jnp.exp(sk - m)
            outs.append(acc / den)
    if out_dim == LANES:
        for h, o in enumerate(outs):
            o_ref[:, h * LANES:(h + 1) * LANES] = o.astype(BF16)
    else:
        lane = lax.broadcasted_iota(jnp.int32, (tq, LANES), 1)
        for p in range(len(outs) // 2):
            pair = jnp.where(lane < out_dim, outs[2 * p], pltpu.roll(outs[2 * p + 1], out_dim, 1))
            o_ref[:, p * LANES:(p + 1) * LANES] = pair.astype(BF16)


def _attention(q, k_lat, v_lat, k_ctx, v_ctx, sink, *, batch, n_kv_total, n_kv, n_group, out_dim,
               window, tq):
    tq_total = q.shape[0] // batch
    tq = min(tq, tq_total)
    n_ctx = k_ctx.shape[0] // batch
    has_lat = k_lat is not None
    has_sink = sink is not None
    s_lat = k_lat.shape[0] // batch if has_lat else 0
    kw = min(tq + 2 * WINDOW, s_lat) if window else 0
    nq = tq_total // tq
    n_heads = n_kv * n_group
    assert out_dim == LANES or (2 * out_dim == LANES and n_heads % 2 == 0)
    in_specs, args = [], []
    if has_sink:
        in_specs.append(pl.BlockSpec(memory_space=pltpu.SMEM))
        args.append(sink)
    in_specs.append(pl.BlockSpec((tq, n_heads * LANES), lambda b, h, i: (b * nq + i, h)))
    args.append(q)
    if has_lat:
        in_specs += [pl.BlockSpec((s_lat, n_kv * LANES), lambda b, h, i: (b, h))] * 2
        args += [k_lat, v_lat]
    in_specs += [pl.BlockSpec((n_ctx, n_kv * LANES), lambda b, h, i: (b, h))] * 2
    args += [k_ctx, v_ctx]
    n_steps = n_kv_total // n_kv
    return pl.pallas_call(
        functools.partial(_attn_kernel, n_kv=n_kv, n_group=n_group, out_dim=out_dim, has_lat=has_lat,
                          has_sink=has_sink, window=window, tq=tq, s_lat=s_lat, kw=kw),
        out_shape=jax.ShapeDtypeStruct((q.shape[0], n_steps * n_heads * out_dim), BF16),
        grid=(batch, n_steps, nq),
        in_specs=in_specs,
        out_specs=pl.BlockSpec((tq, n_heads * out_dim), lambda b, h, i: (b * nq + i, h)),
        compiler_params=_params("arbitrary", "arbitrary", "arbitrary"),
        name="attn_win" if window else ("attn_dense" if has_lat else "attn_ctx"),
    )(*args)


def _out_kernel(*refs, n_parts, tm):
    o_refs = refs[:n_parts]
    w_refs = refs[n_parts:2 * n_parts]
    (x_ref, g1_ref, gn_ref, sc_ref, sh_ref, rw_ref, rb_ref,
     xo_ref, tok_ref, idx_ref, gate_ref) = refs[2 * n_parts:]
    y = _dot(o_refs[0][...], w_refs[0][...])
    for o_r, w_r in zip(o_refs[1:], w_refs[1:]):
        y = y + _dot(o_r[...], w_r[...])
    x = x_ref[...] + g1_ref[0] * y
    xo_ref[...] = x
    tok = _modnorm(x, gn_ref[...], sc_ref[0], sh_ref[0])
    for j in range(tok.shape[1] // LANES):
        tok_ref[pl.ds(j, tm, stride=SUBLANES), :] = tok[:, j * LANES:(j + 1) * LANES]
    tok_hi = tok.astype(BF16)
    tok_lo = (tok - tok_hi.astype(F32)).astype(BF16)
    hh_hl = _dot(tok_hi, rw_ref[...])
    logits = (hh_hl[:, :LANES] + hh_hl[:, LANES:] + _dot(tok_lo, rw_ref[:, :LANES])) + rb_ref[...]
    lane = lax.broadcasted_iota(jnp.int32, logits.shape, 1).astype(F32)
    vals, idxs = [], []
    l = logits
    for _ in range(TOP_K):
        m = jnp.max(l, axis=-1, keepdims=True)
        idx = jnp.min(jnp.where(l == m, lane, float(LANES)), axis=-1, keepdims=True)
        vals.append(m)
        idxs.append(idx)
        l = jnp.where(lane == idx, -jnp.inf, l)
    es = [jnp.exp(v - vals[0]) for v in vals]
    den = es[0]
    for e in es[1:]:
        den = den + e
    lane8 = lax.broadcasted_iota(jnp.int32, (tm, SUBLANES), 1)
    io = jnp.zeros((tm, SUBLANES), jnp.int32)
    go = jnp.zeros((tm, SUBLANES), F32)
    for k in range(TOP_K):
        io = jnp.where(lane8 == k, idxs[k].astype(jnp.int32), io)
        go = jnp.where(lane8 == k, es[k] / den, go)
    idx_ref[...] = io
    gate_ref[...] = go


def _out_proj(o_parts, w_parts, x, g1, gn, sc, sh, rw, rb, rows_per_batch, fixed_mod):
    t, d = x.shape
    tm = _row_tile(min(t, rows_per_batch), 256)
    midx = _mod_index(rows_per_batch, tm, fixed_mod)
    full = lambda a: pl.BlockSpec(a.shape, lambda i: (0,) * a.ndim)
    n_parts = len(o_parts)
    in_specs = ([pl.BlockSpec((tm, o.shape[1]), lambda i: (i, 0)) for o in o_parts]
                + [full(w) for w in w_parts]
                + [pl.BlockSpec((tm, d), lambda i: (i, 0)),
                   pl.BlockSpec((1, 1, d), midx), full(gn),
                   pl.BlockSpec((1, 1, d), midx), pl.BlockSpec((1, 1, d), midx),
                   full(rw), full(rb)])
    return pl.pallas_call(
        functools.partial(_out_kernel, n_parts=n_parts, tm=tm),
        out_shape=[jax.ShapeDtypeStruct((t, d), F32),
                   jax.ShapeDtypeStruct((t * SUBLANES, d // SUBLANES), F32),
                   jax.ShapeDtypeStruct((t, SUBLANES), jnp.int32),
                   jax.ShapeDtypeStruct((t, SUBLANES), F32)],
        grid=(t // tm,),
        in_specs=in_specs,
        out_specs=[pl.BlockSpec((tm, d), lambda i: (i, 0)),
                   pl.BlockSpec((tm * SUBLANES, d // SUBLANES), lambda i: (i, 0)),
                   pl.BlockSpec((tm, SUBLANES), lambda i: (i, 0)),
                   pl.BlockSpec((tm, SUBLANES), lambda i: (i, 0))],
        compiler_params=_params("arbitrary"),
        name="out_proj",
    )(*o_parts, *w_parts, x, g1, gn, sc, sh, rw, rb)


PAIR_TILE = 2 * LANES


def _wprep_kernel(p_ref, w1_ref, w2_ref, o1_ref, o2_ref):
    perm = p_ref[...]
    for j in range(w1_ref.shape[3] // PAIR_TILE):
        cols = slice(j * PAIR_TILE, (j + 1) * PAIR_TILE)
        o1_ref[0, :, cols] = _dot(w1_ref[0, 0, :, cols].astype(BF16), perm).astype(BF16)
    o2_ref[0] = w2_ref[0, 0].astype(BF16)


def _expert_weights(w_in, w_out, layer):
    _, e, d, f2 = w_in.shape
    i = jnp.arange(LANES)
    perm = jnp.zeros((PAIR_TILE, PAIR_TILE), F32).at[2 * i, i].set(1.0).at[2 * i + 1, LANES + i].set(1.0)
    return pl.pallas_call(
        _wprep_kernel,
        out_shape=[jax.ShapeDtypeStruct(w_in.shape[1:], BF16), jax.ShapeDtypeStruct(w_out.shape[1:], BF16)],
        grid=(e,),
        in_specs=[pl.BlockSpec((PAIR_TILE, PAIR_TILE), lambda n: (0, 0)),
                  pl.BlockSpec((1, 1, d, f2), lambda n: (layer, n, 0, 0)),
                  pl.BlockSpec((1, 1, f2 // 2, d), lambda n: (layer, n, 0, 0))],
        out_specs=[pl.BlockSpec((1, d, f2), lambda n: (n, 0, 0)),
                   pl.BlockSpec((1, f2 // 2, d), lambda n: (n, 0, 0))],
        compiler_params=_params("arbitrary"),
        name="expert_weights",
    )(perm.astype(BF16), w_in, w_out)


DISPATCH_TILE = 512


def _row(ref, r):
    return ref.at[pl.ds(pl.multiple_of(r * SUBLANES, SUBLANES), SUBLANES)]


def _dispatch_kernel(plo_ref, phi_ref, dest_ref, tok_ref, xs_ref, zero_ref, sem, zsem, *, tm):
    @pl.when(pl.program_id(0) == 0)
    def _():
        zero_ref[...] = jnp.zeros(zero_ref.shape, zero_ref.dtype)

        def fill_range(e, carry):
            lo, hi = plo_ref[e], phi_ref[e]

            def fill_chunk(c, carry):
                c_lo = lo + c * tm
                c_hi = jnp.minimum(c_lo + tm, hi)

                def start_row(r, carry):
                    pltpu.make_async_copy(zero_ref, _row(xs_ref, r), zsem).start()
                    return carry

                def wait_row(r, carry):
                    pltpu.make_async_copy(zero_ref, _row(xs_ref, r), zsem).wait()
                    return carry

                lax.fori_loop(c_lo, c_hi, start_row, carry)
                return lax.fori_loop(c_lo, c_hi, wait_row, carry)

            return lax.fori_loop(0, (hi - lo + tm - 1) // tm, fill_chunk, carry)

        lax.fori_loop(0, plo_ref.shape[0], fill_range, 0)

    def body(t, carry):
        for u in range(2):
            r = t * 2 + u
            src = _row(tok_ref, r)
            for k in range(TOP_K):
                pltpu.make_async_copy(src, _row(xs_ref, dest_ref[0, 0, r * TOP_K + k]), sem).start(
                    priority=k % 2)
        return carry

    lax.fori_loop(0, tm // 2, body, 0)
    done = xs_ref.at[pl.ds(0, tm * TOP_K * SUBLANES)]
    pltpu.make_async_copy(done, done, sem).wait()


def _dispatch(tokens_rows, dest, pad_lo, pad_hi, cap):
    t = dest.shape[0]
    tm = _row_tile(t, DISPATCH_TILE)
    grid_spec = pltpu.PrefetchScalarGridSpec(
        num_scalar_prefetch=2,
        grid=(t // tm,),
        in_specs=[pl.BlockSpec((1, 1, tm * TOP_K), lambda i, lo, hi: (i, 0, 0), memory_space=pltpu.SMEM),
                  pl.BlockSpec((tm * SUBLANES, LANES), lambda i, lo, hi: (i, 0))],
        out_specs=pl.BlockSpec(memory_space=pl.ANY),
        scratch_shapes=[pltpu.VMEM((SUBLANES, LANES), F32),
                        pltpu.SemaphoreType.DMA, pltpu.SemaphoreType.DMA],
    )
    return pl.pallas_call(
        functools.partial(_dispatch_kernel, tm=tm),
        out_shape=jax.ShapeDtypeStruct((cap * SUBLANES, LANES), F32),
        grid_spec=grid_spec,
        compiler_params=_params("arbitrary"),
        name="dispatch",
    )(pad_lo, pad_hi, dest.reshape(t // tm, 1, tm * TOP_K), tokens_rows)


EXPERT_BLOCK = 512


def _expert_kernel(be_ref, nb_ref, x_ref, w1_ref, b1_ref, w2_ref, b2_ref, o_ref, *, bm):
    del be_ref
    b = pl.program_id(0)
    n_j = w1_ref.shape[1] // LANES

    @pl.when(b < nb_ref[0])
    def _():
        x = jnp.concatenate([x_ref[pl.ds(j, bm, stride=SUBLANES), :] for j in range(n_j)], axis=1)
        u = _dot(x.astype(BF16), w1_ref[0]) + b1_ref[0]
        acts = []
        for j in range(u.shape[1] // PAIR_TILE):
            glu = jnp.minimum(u[:, j * PAIR_TILE:j * PAIR_TILE + LANES], SWIGLU_LIMIT)
            lin = jnp.clip(u[:, j * PAIR_TILE + LANES:(j + 1) * PAIR_TILE], -SWIGLU_LIMIT, SWIGLU_LIMIT)
            acts.append(glu * jax.nn.sigmoid(SWIGLU_ALPHA * glu) * (lin + 1.0))
        a = jnp.concatenate(acts, axis=1)
        y = _dot(a.astype(BF16), w2_ref[0]) + b2_ref[0]
        for j in range(y.shape[1] // LANES):
            o_ref[pl.ds(j, bm, stride=SUBLANES), :] = y[:, j * LANES:(j + 1) * LANES]

    @pl.when(b >= nb_ref[0])
    def _():
        o_ref[...] = jnp.zeros(o_ref.shape, o_ref.dtype)


def _expert_mlp(xs, blk_expert, n_used, w1, b1, w2, b2):
    bm = EXPERT_BLOCK
    nblk = blk_expert.shape[0]
    _, d, f2 = w1.shape
    grid_spec = pltpu.PrefetchScalarGridSpec(
        num_scalar_prefetch=2,
        grid=(nblk,),
        in_specs=[
            pl.BlockSpec((bm * SUBLANES, LANES), lambda b, be, nb: (b, 0)),
            pl.BlockSpec((1, d, f2), lambda b, be, nb: (be[b], 0, 0)),
            pl.BlockSpec((1, 1, f2), lambda b, be, nb: (be[b], 0, 0)),
            pl.BlockSpec((1, f2 // 2, d), lambda b, be, nb: (be[b], 0, 0)),
            pl.BlockSpec((1, 1, d), lambda b, be, nb: (be[b], 0, 0)),
        ],
        out_specs=pl.BlockSpec((bm * SUBLANES, LANES), lambda b, be, nb: (b, 0)),
    )
    return pl.pallas_call(
        functools.partial(_expert_kernel, bm=bm),
        out_shape=jax.ShapeDtypeStruct(xs.shape, F32),
        grid_spec=grid_spec,
        compiler_params=_params("arbitrary"),
        name="expert_mlp",
    )(blk_expert, n_used, xs, w1, b1, w2, b2)


def _combine_kernel(*refs, tm, final):
    if final:
        dest_ref, x_ref, ys_ref, gate_ref, g2_ref, fg_ref, o_ref, ybuf, sem = refs
    else:
        dest_ref, x_ref, ys_ref, gate_ref, g2_ref, o_ref, ybuf, sem = refs

    def body(t, carry):
        for u in range(2):
            r = t * 2 + u
            for k in range(TOP_K):
                pltpu.make_async_copy(_row(ys_ref, dest_ref[0, 0, r * TOP_K + k]),
                                      _row(ybuf, k * tm + r), sem).start(priority=k % 2)
        return carry

    lax.fori_loop(0, tm // 2, body, 0)
    pltpu.make_async_copy(ybuf, ybuf, sem).wait()

    gate = gate_ref[...]
    parts = []
    for j in range(x_ref.shape[1] // LANES):
        f = gate[:, 0:1] * ybuf[pl.ds(j, tm, stride=SUBLANES), :]
        for k in range(1, TOP_K):
            f = f + gate[:, k:k + 1] * ybuf[pl.ds(k * tm * SUBLANES + j, tm, stride=SUBLANES), :]
        parts.append(f)
    x = x_ref[...] + g2_ref[0] * jnp.concatenate(parts, axis=1)
    if final:
        x = _rms(x, fg_ref[...])
    o_ref[...] = x


def _combine(x, ys, dest, gates, g2, final_g, *, row_off, rows_per_batch, fixed_mod):
    t, d = x.shape
    tm = _row_tile(min(t, rows_per_batch), 256)
    midx = _mod_index(rows_per_batch, tm, fixed_mod)
    final = final_g is not None
    off = row_off // tm
    t_all = dest.shape[0]
    in_specs = [pl.BlockSpec((1, 1, tm * TOP_K), lambda i: (off + i, 0, 0), memory_space=pltpu.SMEM),
                pl.BlockSpec((tm, d), lambda i: (i, 0)),
                pl.BlockSpec(memory_space=pl.ANY),
                pl.BlockSpec((tm, SUBLANES), lambda i: (off + i, 0)),
                pl.BlockSpec((1, 1, d), midx)]
    args = [dest.reshape(t_all // tm, 1, tm * TOP_K), x, ys, gates, g2]
    if final:
        in_specs.append(pl.BlockSpec(final_g.shape, lambda i: (0, 0)))
        args.append(final_g)
    return pl.pallas_call(
        functools.partial(_combine_kernel, tm=tm, final=final),
        out_shape=jax.ShapeDtypeStruct((t, d), F32),
        grid=(t // tm,),
        in_specs=in_specs,
        out_specs=pl.BlockSpec((tm, d), lambda i: (i, 0)),
        scratch_shapes=[pltpu.VMEM((TOP_K * tm * SUBLANES, LANES), F32), pltpu.SemaphoreType.DMA],
        compiler_params=_params("arbitrary"),
        name="combine_final" if final else "combine",
    )(*args)


def _pad_heads(w, n_heads, axis):
    shape = list(w.shape)
    hd = shape[axis] // n_heads
    w = w.reshape(shape[:axis] + [n_heads, hd] + shape[axis + 1:])
    pad = [(0, 0)] * w.ndim
    pad[axis + 1] = (0, LANES - hd)
    w = jnp.pad(w, pad)
    shape[axis] = n_heads * LANES
    return w.reshape(shape)


def _rope_tables(s, rot_dim, lo):
    pos = jnp.arange(s, dtype=jnp.int32)
    rows, cols = (pos // GRID_W).astype(F32), (pos % GRID_W).astype(F32)
    quarter = rot_dim // 4
    inv = ROPE_THETA ** (-jnp.arange(quarter, dtype=F32) / quarter)
    ang = jnp.concatenate([rows[:, None] * inv, cols[:, None] * inv], axis=-1)
    cos, sin = jnp.cos(ang), jnp.sin(ang)
    hi = LANES - lo - rot_dim
    cos_t = jnp.concatenate([jnp.ones((s, lo), F32), cos, cos, jnp.ones((s, hi), F32)], axis=1)
    sin_t = jnp.concatenate([jnp.zeros((s, lo), F32), -sin, sin, jnp.zeros((s, hi), F32)], axis=1)
    return cos_t, sin_t


def _moe_plan(eidx, bm):
    t = eidx.shape[0]
    ids = jnp.arange(N_EXPERTS, dtype=jnp.int32)
    sel = eidx[:, :, None] == ids
    hit = sel.astype(jnp.int32).sum(1)
    chunk = _row_tile(t, 256)
    tri = jnp.tril(jnp.ones((chunk, chunk), F32))
    within = jnp.einsum("ij,cjk->cik", tri, hit.astype(F32).reshape(t // chunk, chunk, N_EXPERTS),
                        precision=HIGHEST)
    totals = within[:, -1, :]
    incl = (within + (jnp.cumsum(totals, axis=0) - totals)[:, None, :]).reshape(t, N_EXPERTS)
    incl = incl.astype(jnp.int32)
    counts = incl[-1]
    rank = jnp.where(sel, (incl - hit)[:, None, :], 0).sum(-1)
    padded = (counts + bm - 1) // bm * bm
    pends = jnp.cumsum(padded)
    pstarts = pends - padded
    dest = jnp.where(sel, pstarts, 0).sum(-1) + rank
    n_assign = t * TOP_K
    nblk = -(-(n_assign + N_EXPERTS * (bm - 1)) // bm)
    blk_start = jnp.arange(nblk, dtype=jnp.int32) * bm
    blk_expert = jnp.minimum(jnp.searchsorted(pends, blk_start, side="right"), N_EXPERTS - 1)
    n_used = (pends[-1] // bm).astype(jnp.int32).reshape(1)
    pad_lo = jnp.concatenate([pstarts + counts, pends[-1:]]).astype(jnp.int32)
    pad_hi = jnp.concatenate([pends, jnp.full((1,), nblk * bm, pends.dtype)]).astype(jnp.int32)
    return dest.astype(jnp.int32), blk_expert.astype(jnp.int32), n_used, pad_lo, pad_hi


def _moe(tokens_rows, eidx, w1, b1, w2, b2):
    dest, blk_expert, n_used, pad_lo, pad_hi = _moe_plan(eidx, EXPERT_BLOCK)
    xs = _dispatch(tokens_rows, dest, pad_lo, pad_hi, blk_expert.shape[0] * EXPERT_BLOCK)
    return _expert_mlp(xs, blk_expert, n_used, w1, b1, w2, b2), dest


def kernel(x, c, ctx, c_ctx, ada_w, ada_b, norm_mix_g, norm_ffn_g, ab_w_in, mla_q_norm_g, mla_wq_b,
           mla_kv_norm_g, mla_wkv_b, swa_sink, ab_w_out, c_w_in, c_q_norm_g, c_k_norm_g, c_w_out,
           router_w, router_b, moe_w_in, moe_b_in, moe_w_out, moe_b_out, final_norm_g):
    bn, s, d = x.shape
    n_ctx = ctx.shape[1]
    depth = ada_w.shape[0]
    t_lat, t_ctx = bn * s, bn * n_ctx
    row = lambda v: v.reshape(1, -1)

    n_mod = -(-(bn + 1) // SUBLANES) * SUBLANES
    c_all = jnp.concatenate([c, c_ctx[None], jnp.zeros((n_mod - bn - 1, d), F32)], axis=0)
    mod = _modulation(c_all, ada_w, ada_b).reshape(depth, n_mod, 6, 1, d)
    ctx_row = bn

    tab_a = _rope_tables(s, HEAD_DIM_A, 0)
    tab_b = _rope_tables(s, MLA_ROPE, MLA_NOPE)
    tab_c = _rope_tables(s, HEAD_DIM_C, 0)

    xl = x.reshape(t_lat, d)
    xc = ctx.reshape(t_ctx, d)
    for i in range(depth):
        with_ctx = i < depth - 1
        j = i // 2
        sh1, sc1, g1, sh2, sc2, g2 = [mod[i, :, m] for m in range(6)]
        gmix, gffn = row(norm_mix_g[i]), row(norm_ffn_g[i])
        if i % 2 == 0:
            w_in = ab_w_in[j]
            sizes = [N_HEADS_A * HEAD_DIM_A, N_KV_A * HEAD_DIM_A, N_KV_A * HEAD_DIM_A,
                     MLA_Q_RANK, MLA_KV_RANK, MLA_ROPE]
            offs = [0]
            for n in sizes:
                offs.append(offs[-1] + n)
            cols = [w_in[:, offs[m]:offs[m + 1]] for m in range(6)]
            krg = jnp.pad(cols[5], ((0, 0), (MLA_NOPE, LANES - MLA_NOPE - MLA_ROPE)))
            w1 = jnp.concatenate([_pad_heads(cols[0], N_HEADS_A, 1), _pad_heads(cols[1], N_KV_A, 1),
                                  _pad_heads(cols[2], N_KV_A, 1), cols[3], cols[4], krg], axis=1).astype(BF16)
            wq = _pad_heads(mla_wq_b[j], MLA_HEADS, 1).astype(BF16)
            wkv = mla_wkv_b[j].reshape(MLA_KV_RANK, MLA_HEADS, MLA_NOPE + MLA_V)
            wk = _pad_heads(wkv[:, :, :MLA_NOPE].reshape(MLA_KV_RANK, -1), MLA_HEADS, 1).astype(BF16)
            wv = _pad_heads(wkv[:, :, MLA_NOPE:].reshape(MLA_KV_RANK, -1), MLA_HEADS, 1).astype(BF16)
            ws = [w1, row(mla_q_norm_g[j]), wq, row(mla_kv_norm_g[j]), wk, wv]
            na = N_HEADS_A * HEAD_DIM_A
            wo_a = ab_w_out[j][:na].astype(BF16)
            wo_b = ab_w_out[j][na:].astype(BF16)
            sink = swa_sink[j]

            qa, ka, va, qb, kb, vb = _ab_in(xl, gmix, sc1, sh1, ws, tab_a + tab_b, s, None)
            qa_c, ka_c, va_c, qb_c, kb_c, vb_c = _ab_in(xc, gmix, sc1, sh1, ws, None, n_ctx, ctx_row)
            ga = N_HEADS_A // N_KV_A
            cfg_a = dict(batch=bn, n_kv_total=N_KV_A, n_kv=1, n_group=ga, out_dim=HEAD_DIM_A)
            cfg_b = dict(batch=bn, n_kv_total=MLA_HEADS, n_kv=MLA_STEP_HEADS, n_group=1, out_dim=MLA_V)
            oa = _attention(qa, ka, va, ka_c, va_c, sink, window=True, tq=256, **cfg_a)
            ob = _attention(qb, kb, vb, kb_c, vb_c, None, window=False, tq=256, **cfg_b)
            o_lat, w_o = [oa, ob], [wo_a, wo_b]
            if with_ctx:
                oa_c = _attention(qa_c, None, None, ka_c, va_c, sink, window=False, tq=256, **cfg_a)
                ob_c = _attention(qb_c, None, None, kb_c, vb_c, None, window=False, tq=256, **cfg_b)
                o_ctx = [oa_c, ob_c]
        else:
            w_in = c_w_in[j].astype(BF16)
            qn, kn = row(c_q_norm_g[j]), row(c_k_norm_g[j])
            q, k, v = _c_in(xl, gmix, sc1, sh1, w_in, qn, kn, tab_c, s, None)
            q_c, k_c, v_c = _c_in(xc, gmix, sc1, sh1, w_in, qn, kn, None, n_ctx, ctx_row)
            gc = N_HEADS_C // N_KV_C
            cfg_c = dict(batch=bn, n_kv_total=N_KV_C, n_kv=1, n_group=gc, out_dim=HEAD_DIM_C,
                         window=False, tq=256)
            o = _attention(q, k, v, k_c, v_c, None, **cfg_c)
            o_lat, w_o = [o], [c_w_out[j].astype(BF16)]
            if with_ctx:
                o_ctx = [_attention(q_c, None, None, k_c, v_c, None, **cfg_c)]

        rw = jnp.pad(router_w[i], ((0, 0), (0, LANES - N_EXPERTS)))
        rw_hi = rw.astype(BF16)
        rw = jnp.concatenate([rw_hi, (rw - rw_hi.astype(F32)).astype(BF16)], axis=1)
        rb = jnp.concatenate([router_b[i], jnp.full((LANES - N_EXPERTS,), NEG_INF, F32)]).reshape(1, LANES)
        xl, tok_l, idx_l, gate_l = _out_proj(o_lat, w_o, xl, g1, gffn, sc2, sh2, rw, rb, s, None)
        if with_ctx:
            xc, tok_c, idx_c, gate_c = _out_proj(o_ctx, w_o, xc, g1, gffn, sc2, sh2, rw, rb, n_ctx, ctx_row)
            tokens = jnp.concatenate([tok_l, tok_c], axis=0)
            eidx = jnp.concatenate([idx_l, idx_c], axis=0)
            gates = jnp.concatenate([gate_l, gate_c], axis=0)
        else:
            tokens, eidx, gates = tok_l, idx_l, gate_l

        w1e, w2e = _expert_weights(moe_w_in, moe_w_out, i)
        n_tiles = moe_b_in.shape[-1] // PAIR_TILE
        b1e = moe_b_in[i].reshape(N_EXPERTS, n_tiles, LANES, 2).transpose(0, 1, 3, 2)
        b1e = b1e.reshape(N_EXPERTS, 1, -1)
        b2e = moe_b_out[i][:, None, :]
        ys, dest = _moe(tokens, eidx[:, :TOP_K], w1e, b1e, w2e, b2e)

        last = i == depth - 1
        xl = _combine(xl, ys, dest, gates, g2, row(final_norm_g) if last else None,
                      row_off=0, rows_per_batch=s, fixed_mod=None)
        if with_ctx:
            xc = _combine(xc, ys, dest, gates, g2, None,
                          row_off=t_lat, rows_per_batch=n_ctx, fixed_mod=ctx_row)
    return xl.reshape(bn, s, d)
```

```python
import functools

import jax
import jax.numpy as jnp
from jax import lax
from jax.experimental import pallas as pl
from jax.experimental.pallas import tpu as pltpu

GRID_W = 64
N_HEADS_A, N_KV_A, HEAD_DIM_A, WINDOW = 8, 2, 64, 128
MLA_HEADS, MLA_Q_RANK, MLA_KV_RANK, MLA_NOPE, MLA_ROPE, MLA_V = 8, 384, 256, 64, 32, 64
N_HEADS_C, N_KV_C, HEAD_DIM_C = 8, 2, 128
N_EXPERTS, TOP_K = 32, 4
MLA_STEP_HEADS = 4
SWIGLU_ALPHA, SWIGLU_LIMIT = 1.702, 7.0
ROPE_THETA, RMS_EPS, NEG_INF = 10000.0, 1e-6, -1e30
LOG2_E = 1.4426950408889634

LANES = 128
SUBLANES = 8
VMEM_LIMIT_BYTES = 56 * 1024 * 1024

F32 = jnp.float32
BF16 = jnp.bfloat16
HIGHEST = lax.Precision.HIGHEST


def _params(*sem):
    return pltpu.CompilerParams(dimension_semantics=sem, vmem_limit_bytes=VMEM_LIMIT_BYTES)


def _dot(a, b):
    return jnp.dot(a, b, preferred_element_type=F32)


def _dot_t(a, b):
    return lax.dot_general(a, b, (((1,), (1,)), ((), ())), preferred_element_type=F32)


def _rms(x, g):
    return x * lax.rsqrt(jnp.mean(x * x, axis=-1, keepdims=True) + RMS_EPS) * g


def _modnorm(x, g, sc, sh):
    return _rms(x, g) * (1.0 + sc) + sh


def _rope_group(x, cos, sin, half, lo):
    lane = lax.broadcasted_iota(jnp.int32, x.shape, 1)
    first = (lane >= lo) & (lane < lo + half)
    rot = jnp.where(first, pltpu.roll(x, LANES - half, 1), pltpu.roll(x, half, 1))
    return x * cos + rot * sin


def _groups(x):
    return [x[:, i * LANES:(i + 1) * LANES] for i in range(x.shape[1] // LANES)]


def _mod_kernel(c_ref, w_ref, b_ref, o_ref):
    c = c_ref[...]
    a = c * jax.nn.sigmoid(c)
    o_ref[0] = jnp.dot(a, w_ref[0], preferred_element_type=F32, precision=HIGHEST) + b_ref[0]


def _modulation(c_all, ada_w, ada_b):
    depth, d, n = ada_w.shape
    r = c_all.shape[0]
    nt = n // 4
    return pl.pallas_call(
        _mod_kernel,
        out_shape=jax.ShapeDtypeStruct((depth, r, n), F32),
        grid=(depth, n // nt),
        in_specs=[
            pl.BlockSpec((r, d), lambda i, j: (0, 0)),
            pl.BlockSpec((1, d, nt), lambda i, j: (i, 0, j)),
            pl.BlockSpec((1, 1, nt), lambda i, j: (i, 0, j)),
        ],
        out_specs=pl.BlockSpec((1, r, nt), lambda i, j: (i, 0, j)),
        compiler_params=_params("arbitrary", "arbitrary"),
        name="modulation",
    )(c_all, ada_w, ada_b.reshape(depth, 1, n))


def _ab_in_kernel(*refs, rope):
    if rope:
        (x_ref, g_ref, sc_ref, sh_ref, w1_ref, qg_ref, wq_ref, kvg_ref, wk_ref, wv_ref,
         ca_ref, sa_ref, cb_ref, sb_ref, qa_ref, ka_ref, va_ref, qb_ref, kb_ref, vb_ref) = refs
    else:
        (x_ref, g_ref, sc_ref, sh_ref, w1_ref, qg_ref, wq_ref, kvg_ref, wk_ref, wv_ref,
         qa_ref, ka_ref, va_ref, qb_ref, kb_ref, vb_ref) = refs
    h = _modnorm(x_ref[...], g_ref[...], sc_ref[0], sh_ref[0]).astype(BF16)
    p = _dot(h, w1_ref[...])
    nqa = N_HEADS_A * LANES
    nka = N_KV_A * LANES
    o = 0
    qa = p[:, o:o + nqa]; o += nqa
    ka = p[:, o:o + nka]; o += nka
    va = p[:, o:o + nka]; o += nka
    cq = p[:, o:o + MLA_Q_RANK]; o += MLA_Q_RANK
    ckv = p[:, o:o + MLA_KV_RANK]; o += MLA_KV_RANK
    krg = p[:, o:o + LANES]

    qb = _dot(_rms(cq, qg_ref[...]).astype(BF16), wq_ref[...])
    ckv_n = _rms(ckv, kvg_ref[...]).astype(BF16)
    kb = _dot(ckv_n, wk_ref[...])
    vb = _dot(ckv_n, wv_ref[...])

    scale_a = HEAD_DIM_A ** -0.5 * LOG2_E
    scale_b = (MLA_NOPE + MLA_ROPE) ** -0.5 * LOG2_E
    if rope:
        ca, sa, cb, sb = ca_ref[...], sa_ref[...], cb_ref[...], sb_ref[...]
        half_a, half_b = HEAD_DIM_A // 2, MLA_ROPE // 2
        qa_g = [_rope_group(t, ca, sa, half_a, 0) * scale_a for t in _groups(qa)]
        ka_g = [_rope_group(t, ca, sa, half_a, 0) for t in _groups(ka)]
        qb_g = [_rope_group(t, cb, sb, half_b, MLA_NOPE) * scale_b for t in _groups(qb)]
        krg = _rope_group(krg, cb, sb, half_b, MLA_NOPE)
    else:
        qa_g = [t * scale_a for t in _groups(qa)]
        ka_g = _groups(ka)
        qb_g = [t * scale_b for t in _groups(qb)]
    kb_g = [t + krg for t in _groups(kb)]
    qa_ref[...] = jnp.concatenate(qa_g, axis=1).astype(BF16)
    ka_ref[...] = jnp.concatenate(ka_g, axis=1).astype(BF16)
    va_ref[...] = va.astype(BF16)
    qb_ref[...] = jnp.concatenate(qb_g, axis=1).astype(BF16)
    kb_ref[...] = jnp.concatenate(kb_g, axis=1).astype(BF16)
    vb_ref[...] = vb.astype(BF16)


def _row_tile(t, pref=512):
    tm = pref
    while t % tm:
        tm //= 2
    return tm


def _mod_index(rows_per_batch, tm, fixed):
    if fixed is not None:
        return lambda i: (fixed, 0, 0)
    return lambda i: ((i * tm) // rows_per_batch, 0, 0)


def _ab_in(x, g, sc, sh, w, tables, rows_per_batch, fixed_mod):
    t, d = x.shape
    tm = _row_tile(min(t, rows_per_batch))
    rope = tables is not None
    midx = _mod_index(rows_per_batch, tm, fixed_mod)
    full = lambda a: pl.BlockSpec(a.shape, lambda i: (0,) * a.ndim)
    in_specs = [
        pl.BlockSpec((tm, d), lambda i: (i, 0)),
        full(g),
        pl.BlockSpec((1, 1, d), midx),
        pl.BlockSpec((1, 1, d), midx),
    ] + [full(a) for a in w]
    args = [x, g, sc, sh] + list(w)
    if rope:
        nt = rows_per_batch // tm
        for tab in tables:
            in_specs.append(pl.BlockSpec((tm, LANES), lambda i: (i % nt, 0)))
            args.append(tab)
    widths = [N_HEADS_A * LANES, N_KV_A * LANES, N_KV_A * LANES,
              MLA_HEADS * LANES, MLA_HEADS * LANES, MLA_HEADS * LANES]
    return pl.pallas_call(
        functools.partial(_ab_in_kernel, rope=rope),
        out_shape=[jax.ShapeDtypeStruct((t, n), BF16) for n in widths],
        grid=(t // tm,),
        in_specs=in_specs,
        out_specs=[pl.BlockSpec((tm, n), lambda i: (i, 0)) for n in widths],
        compiler_params=_params("arbitrary"),
        name="ab_in_rope" if rope else "ab_in",
    )(*args)


def _c_in_kernel(*refs, rope):
    if rope:
        x_ref, g_ref, sc_ref, sh_ref, w_ref, qn_ref, kn_ref, cc_ref, sc2_ref, q_ref, k_ref, v_ref = refs
    else:
        x_ref, g_ref, sc_ref, sh_ref, w_ref, qn_ref, kn_ref, q_ref, k_ref, v_ref = refs
    h = _modnorm(x_ref[...], g_ref[...], sc_ref[0], sh_ref[0]).astype(BF16)
    p = _dot(h, w_ref[...])
    nq = N_HEADS_C * HEAD_DIM_C
    nk = N_KV_C * HEAD_DIM_C
    q_g = [_rms(t, qn_ref[...]) for t in _groups(p[:, :nq])]
    k_g = [_rms(t, kn_ref[...]) for t in _groups(p[:, nq:nq + nk])]
    scale = HEAD_DIM_C ** -0.5 * LOG2_E
    if rope:
        cc, ss = cc_ref[...], sc2_ref[...]
        q_g = [_rope_group(t, cc, ss, HEAD_DIM_C // 2, 0) for t in q_g]
        k_g = [_rope_group(t, cc, ss, HEAD_DIM_C // 2, 0) for t in k_g]
    q_ref[...] = jnp.concatenate([t * scale for t in q_g], axis=1).astype(BF16)
    k_ref[...] = jnp.concatenate(k_g, axis=1).astype(BF16)
    v_ref[...] = p[:, nq + nk:].astype(BF16)


def _c_in(x, g, sc, sh, w, qn, kn, tables, rows_per_batch, fixed_mod):
    t, d = x.shape
    tm = _row_tile(min(t, rows_per_batch))
    rope = tables is not None
    midx = _mod_index(rows_per_batch, tm, fixed_mod)
    full = lambda a: pl.BlockSpec(a.shape, lambda i: (0,) * a.ndim)
    in_specs = [
        pl.BlockSpec((tm, d), lambda i: (i, 0)),
        full(g),
        pl.BlockSpec((1, 1, d), midx),
        pl.BlockSpec((1, 1, d), midx),
        full(w), full(qn), full(kn),
    ]
    args = [x, g, sc, sh, w, qn, kn]
    if rope:
        nt = rows_per_batch // tm
        for tab in tables:
            in_specs.append(pl.BlockSpec((tm, LANES), lambda i: (i % nt, 0)))
            args.append(tab)
    widths = [N_HEADS_C * HEAD_DIM_C, N_KV_C * HEAD_DIM_C, N_KV_C * HEAD_DIM_C]
    return pl.pallas_call(
        functools.partial(_c_in_kernel, rope=rope),
        out_shape=[jax.ShapeDtypeStruct((t, n), BF16) for n in widths],
        grid=(t // tm,),
        in_specs=in_specs,
        out_specs=[pl.BlockSpec((tm, n), lambda i: (i, 0)) for n in widths],
        compiler_params=_params("arbitrary"),
        name="c_in_rope" if rope else "c_in",
    )(*args)


def _attn_kernel(*refs, n_kv, n_group, out_dim, has_lat, has_sink, window, tq, s_lat, kw):
    refs = list(refs)
    sink_ref = refs.pop(0) if has_sink else None
    q_ref = refs.pop(0)
    if has_lat:
        k1_ref, v1_ref = refs.pop(0), refs.pop(0)
    k2_ref, v2_ref, o_ref = refs
    hk0 = pl.program_id(1) * n_kv
    qi = pl.program_id(2)
    if window:
        ws = pl.multiple_of(jnp.clip(qi * tq - WINDOW, 0, s_lat - kw), LANES)
        rows = pl.ds(ws, kw)
        qpos = qi * tq + lax.broadcasted_iota(jnp.int32, (tq, kw), 0)
        kpos = ws + lax.broadcasted_iota(jnp.int32, (tq, kw), 1)
        ok = jnp.abs(kpos - qpos) <= WINDOW
    else:
        rows = slice(None)
    outs = []
    for j in range(n_kv):
        lanes = slice(j * LANES, (j + 1) * LANES)
        k2, v2 = k2_ref[:, lanes], v2_ref[:, lanes]
        if has_lat:
            k1, v1 = k1_ref[rows, lanes], v1_ref[rows, lanes]
        for g in range(n_group):
            h = j * n_group + g
            q = q_ref[:, h * LANES:(h + 1) * LANES]
            s2 = _dot_t(q, k2)
            m = jnp.max(s2, axis=-1, keepdims=True)
            if has_lat:
                s1 = _dot_t(q, k1)
                if window:
                    s1 = jnp.where(ok, s1, NEG_INF)
                m = jnp.maximum(m, jnp.max(s1, axis=-1, keepdims=True))
            if has_sink:
                sk = sink_ref[hk0 * n_group + h] * LOG2_E
                m = jnp.maximum(m, sk)
            e2 = jnp.exp2(s2 - m)
            den = jnp.sum(e2, axis=-1, keepdims=True)
            acc = _dot(e2.astype(BF16), v2)
            if has_lat:
                e1 = jnp.exp2(s1 - m)
                den = den + jnp.sum(e1, axis=-1, keepdims=True)
                acc = acc + _dot(e1.astype(BF16), v1)
            if has_sink:
                den = den + jnp.exp2(sk - m)
            outs.append(acc / den)
    if out_dim == LANES:
        for h, o in enumerate(outs):
            o_ref[:, h * LANES:(h + 1) * LANES] = o.astype(BF16)
    else:
        lane = lax.broadcasted_iota(jnp.int32, (tq, LANES), 1)
        for p in range(len(outs) // 2):
            pair = jnp.where(lane < out_dim, outs[2 * p], pltpu.roll(outs[2 * p + 1], out_dim, 1))
            o_ref[:, p * LANES:(p + 1) * LANES] = pair.astype(BF16)


def _attention(q, k_lat, v_lat, k_ctx, v_ctx, sink, *, batch, n_kv_total, n_kv, n_group, out_dim,
               window, tq):
    tq_total = q.shape[0] // batch
    tq = min(tq, tq_total)
    n_ctx = k_ctx.shape[0] // batch
    has_lat = k_lat is not None
    has_sink = sink is not None
    s_lat = k_lat.shape[0] // batch if has_lat else 0
    kw = min(tq + 2 * WINDOW, s_lat) if window else 0
    nq = tq_total // tq
    n_heads = n_kv * n_group
    assert out_dim == LANES or (2 * out_dim == LANES and n_heads % 2 == 0)
    in_specs, args = [], []
    if has_sink:
        in_specs.append(pl.BlockSpec(memory_space=pltpu.SMEM))
        args.append(sink)
    in_specs.append(pl.BlockSpec((tq, n_heads * LANES), lambda b, h, i: (b * nq + i, h)))
    args.append(q)
    if has_lat:
        in_specs += [pl.BlockSpec((s_lat, n_kv * LANES), lambda b, h, i: (b, h))] * 2
        args += [k_lat, v_lat]
    in_specs += [pl.BlockSpec((n_ctx, n_kv * LANES), lambda b, h, i: (b, h))] * 2
    args += [k_ctx, v_ctx]
    n_steps = n_kv_total // n_kv
    return pl.pallas_call(
        functools.partial(_attn_kernel, n_kv=n_kv, n_group=n_group, out_dim=out_dim, has_lat=has_lat,
                          has_sink=has_sink, window=window, tq=tq, s_lat=s_lat, kw=kw),
        out_shape=jax.ShapeDtypeStruct((q.shape[0], n_steps * n_heads * out_dim), BF16),
        grid=(batch, n_steps, nq),
        in_specs=in_specs,
        out_specs=pl.BlockSpec((tq, n_heads * out_dim), lambda b, h, i: (b * nq + i, h)),
        compiler_params=_params("arbitrary", "arbitrary", "arbitrary"),
        name="attn_win" if window else ("attn_dense" if has_lat else "attn_ctx"),
    )(*args)


def _out_kernel(*refs, n_parts, tm):
    o_refs = refs[:n_parts]
    w_refs = refs[n_parts:2 * n_parts]
    (x_ref, g1_ref, gn_ref, sc_ref, sh_ref, rw_ref, rb_ref,
     xo_ref, tok_ref, idx_ref, gate_ref) = refs[2 * n_parts:]
    y = _dot(o_refs[0][...], w_refs[0][...])
    for o_r, w_r in zip(o_refs[1:], w_refs[1:]):
        y = y + _dot(o_r[...], w_r[...])
    x = x_ref[...] + g1_ref[0] * y
    xo_ref[...] = x
    tok = _modnorm(x, gn_ref[...], sc_ref[0], sh_ref[0])
    for j in range(tok.shape[1] // LANES):
        tok_ref[pl.ds(j, tm, stride=SUBLANES), :] = tok[:, j * LANES:(j + 1) * LANES]
    tok_hi = tok.astype(BF16)
    tok_lo = (tok - tok_hi.astype(F32)).astype(BF16)
    hh_hl = _dot(tok_hi, rw_ref[...])
    logits = (hh_hl[:, :LANES] + hh_hl[:, LANES:] + _dot(tok_lo, rw_ref[:, :LANES])) + rb_ref[...]
    lane = lax.broadcasted_iota(jnp.int32, logits.shape, 1).astype(F32)
    vals, idxs = [], []
    l = logits
    for _ in range(TOP_K):
        m = jnp.max(l, axis=-1, keepdims=True)
        idx = jnp.min(jnp.where(l == m, lane, float(LANES)), axis=-1, keepdims=True)
        vals.append(m)
        idxs.append(idx)
        l = jnp.where(lane == idx, -jnp.inf, l)
    es = [jnp.exp(v - vals[0]) for v in vals]
    den = es[0]
    for e in es[1:]:
        den = den + e
    lane8 = lax.broadcasted_iota(jnp.int32, (tm, SUBLANES), 1)
    io = jnp.zeros((tm, SUBLANES), jnp.int32)
    go = jnp.zeros((tm, SUBLANES), F32)
    for k in range(TOP_K):
        io = jnp.where(lane8 == k, idxs[k].astype(jnp.int32), io)
        go = jnp.where(lane8 == k, es[k] / den, go)
    idx_ref[...] = io
    gate_ref[...] = go


def _out_proj(o_parts, w_parts, x, g1, gn, sc, sh, rw, rb, rows_per_batch, fixed_mod):
    t, d = x.shape
    tm = _row_tile(min(t, rows_per_batch), 256)
    midx = _mod_index(rows_per_batch, tm, fixed_mod)
    full = lambda a: pl.BlockSpec(a.shape, lambda i: (0,) * a.ndim)
    n_parts = len(o_parts)
    in_specs = ([pl.BlockSpec((tm, o.shape[1]), lambda i: (i, 0)) for o in o_parts]
                + [full(w) for w in w_parts]
                + [pl.BlockSpec((tm, d), lambda i: (i, 0)),
                   pl.BlockSpec((1, 1, d), midx), full(gn),
                   pl.BlockSpec((1, 1, d), midx), pl.BlockSpec((1, 1, d), midx),
                   full(rw), full(rb)])
    return pl.pallas_call(
        functools.partial(_out_kernel, n_parts=n_parts, tm=tm),
        out_shape=[jax.ShapeDtypeStruct((t, d), F32),
                   jax.ShapeDtypeStruct((t * SUBLANES, d // SUBLANES), F32),
                   jax.ShapeDtypeStruct((t, SUBLANES), jnp.int32),
                   jax.ShapeDtypeStruct((t, SUBLANES), F32)],
        grid=(t // tm,),
        in_specs=in_specs,
        out_specs=[pl.BlockSpec((tm, d), lambda i: (i, 0)),
                   pl.BlockSpec((tm * SUBLANES, d // SUBLANES), lambda i: (i, 0)),
                   pl.BlockSpec((tm, SUBLANES), lambda i: (i, 0)),
                   pl.BlockSpec((tm, SUBLANES), lambda i: (i, 0))],
        compiler_params=_params("arbitrary"),
        name="out_proj",
    )(*o_parts, *w_parts, x, g1, gn, sc, sh, rw, rb)


DISPATCH_TILE = 512
PAIR_TILE = 2 * LANES


def _row(ref, r):
    return ref.at[pl.ds(pl.multiple_of(r * SUBLANES, SUBLANES), SUBLANES)]


def _dispatch_kernel(*refs, tm, part_tiles, n_experts):
    n_parts = len(part_tiles)
    plo_ref, phi_ref, dest_ref = refs[:3]
    tok_refs = refs[3:3 + n_parts]
    p_ref, w1_ref, w2_ref, xs_ref, o1_ref, o2_ref, zero_ref, sem, zsem = refs[3 + n_parts:]
    i = pl.program_id(0)

    @pl.when(i == 0)
    def _():
        zero_ref[...] = jnp.zeros(zero_ref.shape, zero_ref.dtype)

        def fill_range(e, carry):
            lo, hi = plo_ref[e], phi_ref[e]

            def fill_chunk(c, carry):
                c_lo = lo + c * tm
                c_hi = jnp.minimum(c_lo + tm, hi)

                def start_row(r, carry):
                    pltpu.make_async_copy(zero_ref, _row(xs_ref, r), zsem).start()
                    return carry

                def wait_row(r, carry):
                    pltpu.make_async_copy(zero_ref, _row(xs_ref, r), zsem).wait()
                    return carry

                lax.fori_loop(c_lo, c_hi, start_row, carry)
                return lax.fori_loop(c_lo, c_hi, wait_row, carry)

            return lax.fori_loop(0, (hi - lo + tm - 1) // tm, fill_chunk, carry)

        lax.fori_loop(0, plo_ref.shape[0], fill_range, 0)

    def start_rows(tok_ref):
        def body(t, carry):
            for u in range(2):
                r = t * 2 + u
                src = _row(tok_ref, r)
                for k in range(TOP_K):
                    pltpu.make_async_copy(src, _row(xs_ref, dest_ref[0, 0, r * TOP_K + k]), sem).start(
                        priority=k % 2)
            return carry

        lax.fori_loop(0, tm // 2, body, 0)

    first = 0
    for tok_ref, n in zip(tok_refs, part_tiles):
        pl.when((i >= first) & (i < first + n))(functools.partial(start_rows, tok_ref))
        first += n

    @pl.when(i < n_experts)
    def _():
        perm = p_ref[...]
        for j in range(w1_ref.shape[3] // PAIR_TILE):
            cols = slice(j * PAIR_TILE, (j + 1) * PAIR_TILE)
            o1_ref[0, :, cols] = _dot(w1_ref[0, 0, :, cols].astype(BF16), perm).astype(BF16)
        o2_ref[0] = w2_ref[0, 0].astype(BF16)

    @pl.when(i < sum(part_tiles))
    def _():
        done = xs_ref.at[pl.ds(0, tm * TOP_K * SUBLANES)]
        pltpu.make_async_copy(done, done, sem).wait()


def _dispatch(token_parts, dest, pad_lo, pad_hi, cap, w_in, w_out, layer):
    t = dest.shape[0]
    tm = _row_tile(min(p.shape[0] // SUBLANES for p in token_parts), DISPATCH_TILE)
    part_tiles = tuple(p.shape[0] // SUBLANES // tm for p in token_parts)
    n_tiles = sum(part_tiles)
    assert n_tiles * tm == t
    _, e, d, f2 = w_in.shape
    lane = jnp.arange(LANES)
    perm = jnp.zeros((PAIR_TILE, PAIR_TILE), F32).at[2 * lane, lane].set(1.0)
    perm = perm.at[2 * lane + 1, LANES + lane].set(1.0).astype(BF16)
    in_specs = [pl.BlockSpec((1, 1, tm * TOP_K), lambda i, lo, hi: (jnp.minimum(i, n_tiles - 1), 0, 0),
                             memory_space=pltpu.SMEM)]
    first = 0
    for n in part_tiles:
        in_specs.append(pl.BlockSpec(
            (tm * SUBLANES, LANES), lambda i, lo, hi, first=first, n=n: (jnp.clip(i - first, 0, n - 1), 0)))
        first += n
    expert = lambda i, lo, hi: jnp.minimum(i, e - 1)
    in_specs += [pl.BlockSpec((PAIR_TILE, PAIR_TILE), lambda i, lo, hi: (0, 0)),
                 pl.BlockSpec((1, 1, d, f2), lambda i, lo, hi: (layer, expert(i, lo, hi), 0, 0)),
                 pl.BlockSpec((1, 1, f2 // 2, d), lambda i, lo, hi: (layer, expert(i, lo, hi), 0, 0))]
    grid_spec = pltpu.PrefetchScalarGridSpec(
        num_scalar_prefetch=2,
        grid=(max(n_tiles, e),),
        in_specs=in_specs,
        out_specs=[pl.BlockSpec(memory_space=pl.ANY),
                   pl.BlockSpec((1, d, f2), lambda i, lo, hi: (expert(i, lo, hi), 0, 0)),
                   pl.BlockSpec((1, f2 // 2, d), lambda i, lo, hi: (expert(i, lo, hi), 0, 0))],
        scratch_shapes=[pltpu.VMEM((SUBLANES, LANES), F32),
                        pltpu.SemaphoreType.DMA, pltpu.SemaphoreType.DMA],
    )
    return pl.pallas_call(
        functools.partial(_dispatch_kernel, tm=tm, part_tiles=part_tiles, n_experts=e),
        out_shape=[jax.ShapeDtypeStruct((cap * SUBLANES, LANES), F32),
                   jax.ShapeDtypeStruct(w_in.shape[1:], BF16), jax.ShapeDtypeStruct(w_out.shape[1:], BF16)],
        grid_spec=grid_spec,
        compiler_params=_params("arbitrary"),
        name="dispatch",
    )(pad_lo, pad_hi, dest.reshape(n_tiles, 1, tm * TOP_K), *token_parts, perm, w_in, w_out)


EXPERT_BLOCK = 512


def _expert_kernel(be_ref, nb_ref, x_ref, w1_ref, b1_ref, w2_ref, b2_ref, o_ref, *, bm):
    del be_ref
    b = pl.program_id(0)
    n_j = w1_ref.shape[1] // LANES

    @pl.when(b < nb_ref[0])
    def _():
        x = jnp.concatenate([x_ref[pl.ds(j, bm, stride=SUBLANES), :] for j in range(n_j)], axis=1)
        u = _dot(x.astype(BF16), w1_ref[0]) + b1_ref[0]
        acts = []
        for j in range(u.shape[1] // PAIR_TILE):
            glu = jnp.minimum(u[:, j * PAIR_TILE:j * PAIR_TILE + LANES], SWIGLU_LIMIT)
            lin = jnp.clip(u[:, j * PAIR_TILE + LANES:(j + 1) * PAIR_TILE], -SWIGLU_LIMIT, SWIGLU_LIMIT)
            acts.append(glu * jax.nn.sigmoid(SWIGLU_ALPHA * glu) * (lin + 1.0))
        a = jnp.concatenate(acts, axis=1)
        y = _dot(a.astype(BF16), w2_ref[0]) + b2_ref[0]
        for j in range(y.shape[1] // LANES):
            o_ref[pl.ds(j, bm, stride=SUBLANES), :] = y[:, j * LANES:(j + 1) * LANES]

    @pl.when(b >= nb_ref[0])
    def _():
        o_ref[...] = jnp.zeros(o_ref.shape, o_ref.dtype)


def _expert_mlp(xs, blk_expert, n_used, w1, b1, w2, b2):
    bm = EXPERT_BLOCK
    nblk = blk_expert.shape[0]
    _, d, f2 = w1.shape
    grid_spec = pltpu.PrefetchScalarGridSpec(
        num_scalar_prefetch=2,
        grid=(nblk,),
        in_specs=[
            pl.BlockSpec((bm * SUBLANES, LANES), lambda b, be, nb: (b, 0)),
            pl.BlockSpec((1, d, f2), lambda b, be, nb: (be[b], 0, 0)),
            pl.BlockSpec((1, 1, f2), lambda b, be, nb: (be[b], 0, 0)),
            pl.BlockSpec((1, f2 // 2, d), lambda b, be, nb: (be[b], 0, 0)),
            pl.BlockSpec((1, 1, d), lambda b, be, nb: (be[b], 0, 0)),
        ],
        out_specs=pl.BlockSpec((bm * SUBLANES, LANES), lambda b, be, nb: (b, 0)),
    )
    return pl.pallas_call(
        functools.partial(_expert_kernel, bm=bm),
        out_shape=jax.ShapeDtypeStruct(xs.shape, F32),
        grid_spec=grid_spec,
        compiler_params=_params("arbitrary"),
        name="expert_mlp",
    )(blk_expert, n_used, xs, w1, b1, w2, b2)


def _combine_kernel(*refs, tm, final):
    if final:
        dest_ref, x_ref, ys_ref, gate_ref, g2_ref, fg_ref, o_ref, ybuf, sem = refs
    else:
        dest_ref, x_ref, ys_ref, gate_ref, g2_ref, o_ref, ybuf, sem = refs

    def body(t, carry):
        for u in range(2):
            r = t * 2 + u
            for k in range(TOP_K):
                pltpu.make_async_copy(_row(ys_ref, dest_ref[0, 0, r * TOP_K + k]),
                                      _row(ybuf, k * tm + r), sem).start(priority=k % 2)
        return carry

    lax.fori_loop(0, tm // 2, body, 0)
    pltpu.make_async_copy(ybuf, ybuf, sem).wait()

    gate = gate_ref[...]
    parts = []
    for j in range(x_ref.shape[1] // LANES):
        f = gate[:, 0:1] * ybuf[pl.ds(j, tm, stride=SUBLANES), :]
        for k in range(1, TOP_K):
            f = f + gate[:, k:k + 1] * ybuf[pl.ds(k * tm * SUBLANES + j, tm, stride=SUBLANES), :]
        parts.append(f)
    x = x_ref[...] + g2_ref[0] * jnp.concatenate(parts, axis=1)
    if final:
        x = _rms(x, fg_ref[...])
    o_ref[...] = x


def _combine(x, ys, dest, gates, g2, final_g, *, row_off, rows_per_batch, fixed_mod):
    t, d = x.shape
    tm = _row_tile(min(t, rows_per_batch), 256)
    midx = _mod_index(rows_per_batch, tm, fixed_mod)
    final = final_g is not None
    off = row_off // tm
    t_all = dest.shape[0]
    in_specs = [pl.BlockSpec((1, 1, tm * TOP_K), lambda i: (off + i, 0, 0), memory_space=pltpu.SMEM),
                pl.BlockSpec((tm, d), lambda i: (i, 0)),
                pl.BlockSpec(memory_space=pl.ANY),
                pl.BlockSpec((tm, SUBLANES), lambda i: (off + i, 0)),
                pl.BlockSpec((1, 1, d), midx)]
    args = [dest.reshape(t_all // tm, 1, tm * TOP_K), x, ys, gates, g2]
    if final:
        in_specs.append(pl.BlockSpec(final_g.shape, lambda i: (0, 0)))
        args.append(final_g)
    return pl.pallas_call(
        functools.partial(_combine_kernel, tm=tm, final=final),
        out_shape=jax.ShapeDtypeStruct((t, d), F32),
        grid=(t // tm,),
        in_specs=in_specs,
        out_specs=pl.BlockSpec((tm, d), lambda i: (i, 0)),
        scratch_shapes=[pltpu.VMEM((TOP_K * tm * SUBLANES, LANES), F32), pltpu.SemaphoreType.DMA],
        compiler_params=_params("arbitrary"),
        name="combine_final" if final else "combine",
    )(*args)


def _pad_heads(w, n_heads, axis):
    shape = list(w.shape)
    hd = shape[axis] // n_heads
    w = w.reshape(shape[:axis] + [n_heads, hd] + shape[axis + 1:])
    pad = [(0, 0)] * w.ndim
    pad[axis + 1] = (0, LANES - hd)
    w = jnp.pad(w, pad)
    shape[axis] = n_heads * LANES
    return w.reshape(shape)


def _rope_tables(s, rot_dim, lo):
    pos = jnp.arange(s, dtype=jnp.int32)
    rows, cols = (pos // GRID_W).astype(F32), (pos % GRID_W).astype(F32)
    quarter = rot_dim // 4
    inv = ROPE_THETA ** (-jnp.arange(quarter, dtype=F32) / quarter)
    ang = jnp.concatenate([rows[:, None] * inv, cols[:, None] * inv], axis=-1)
    cos, sin = jnp.cos(ang), jnp.sin(ang)
    hi = LANES - lo - rot_dim
    cos_t = jnp.concatenate([jnp.ones((s, lo), F32), cos, cos, jnp.ones((s, hi), F32)], axis=1)
    sin_t = jnp.concatenate([jnp.zeros((s, lo), F32), -sin, sin, jnp.zeros((s, hi), F32)], axis=1)
    return cos_t, sin_t


def _moe_plan(eidx, bm):
    t = eidx.shape[0]
    ids = jnp.arange(N_EXPERTS, dtype=jnp.int32)
    sel = eidx[:, :, None] == ids
    hit = sel.astype(jnp.int32).sum(1)
    chunk = _row_tile(t, 256)
    tri = jnp.tril(jnp.ones((chunk, chunk), F32))
    within = jnp.einsum("ij,cjk->cik", tri, hit.astype(F32).reshape(t // chunk, chunk, N_EXPERTS),
                        precision=HIGHEST)
    totals = within[:, -1, :]
    incl = (within + (jnp.cumsum(totals, axis=0) - totals)[:, None, :]).reshape(t, N_EXPERTS)
    incl = incl.astype(jnp.int32)
    counts = incl[-1]
    rank = jnp.where(sel, (incl - hit)[:, None, :], 0).sum(-1)
    padded = (counts + bm - 1) // bm * bm
    pends = jnp.cumsum(padded)
    pstarts = pends - padded
    dest = jnp.where(sel, pstarts, 0).sum(-1) + rank
    n_assign = t * TOP_K
    nblk = -(-(n_assign + N_EXPERTS * (bm - 1)) // bm)
    blk_start = jnp.arange(nblk, dtype=jnp.int32) * bm
    blk_expert = jnp.minimum((blk_start[:, None] >= pends[None, :]).sum(1), N_EXPERTS - 1)
    n_used = (pends[-1] // bm).astype(jnp.int32).reshape(1)
    pad_lo = jnp.concatenate([pstarts + counts, pends[-1:]]).astype(jnp.int32)
    pad_hi = jnp.concatenate([pends, jnp.full((1,), nblk * bm, pends.dtype)]).astype(jnp.int32)
    return dest.astype(jnp.int32), blk_expert.astype(jnp.int32), n_used, pad_lo, pad_hi


def _moe(token_parts, eidx, w_in, b1, w_out, b2, layer):
    dest, blk_expert, n_used, pad_lo, pad_hi = _moe_plan(eidx, EXPERT_BLOCK)
    xs, w1, w2 = _dispatch(token_parts, dest, pad_lo, pad_hi, blk_expert.shape[0] * EXPERT_BLOCK,
                           w_in, w_out, layer)
    return _expert_mlp(xs, blk_expert, n_used, w1, b1, w2, b2), dest


def kernel(x, c, ctx, c_ctx, ada_w, ada_b, norm_mix_g, norm_ffn_g, ab_w_in, mla_q_norm_g, mla_wq_b,
           mla_kv_norm_g, mla_wkv_b, swa_sink, ab_w_out, c_w_in, c_q_norm_g, c_k_norm_g, c_w_out,
           router_w, router_b, moe_w_in, moe_b_in, moe_w_out, moe_b_out, final_norm_g):
    bn, s, d = x.shape
    n_ctx = ctx.shape[1]
    depth = ada_w.shape[0]
    t_lat, t_ctx = bn * s, bn * n_ctx
    row = lambda v: v.reshape(1, -1)

    n_mod = -(-(bn + 1) // SUBLANES) * SUBLANES
    c_all = jnp.concatenate([c, c_ctx[None], jnp.zeros((n_mod - bn - 1, d), F32)], axis=0)
    mod = _modulation(c_all, ada_w, ada_b).reshape(depth, n_mod, 6, 1, d)
    ctx_row = bn

    tab_a = _rope_tables(s, HEAD_DIM_A, 0)
    tab_b = _rope_tables(s, MLA_ROPE, MLA_NOPE)
    tab_c = _rope_tables(s, HEAD_DIM_C, 0)

    xl = x.reshape(t_lat, d)
    xc = ctx.reshape(t_ctx, d)
    for i in range(depth):
        with_ctx = i < depth - 1
        j = i // 2
        sh1, sc1, g1, sh2, sc2, g2 = [mod[i, :, m] for m in range(6)]
        gmix, gffn = row(norm_mix_g[i]), row(norm_ffn_g[i])
        if i % 2 == 0:
            w_in = ab_w_in[j]
            sizes = [N_HEADS_A * HEAD_DIM_A, N_KV_A * HEAD_DIM_A, N_KV_A * HEAD_DIM_A,
                     MLA_Q_RANK, MLA_KV_RANK, MLA_ROPE]
            offs = [0]
            for n in sizes:
                offs.append(offs[-1] + n)
            cols = [w_in[:, offs[m]:offs[m + 1]] for m in range(6)]
            krg = jnp.pad(cols[5], ((0, 0), (MLA_NOPE, LANES - MLA_NOPE - MLA_ROPE)))
            w1 = jnp.concatenate([_pad_heads(cols[0], N_HEADS_A, 1), _pad_heads(cols[1], N_KV_A, 1),
                                  _pad_heads(cols[2], N_KV_A, 1), cols[3], cols[4], krg], axis=1).astype(BF16)
            wq = _pad_heads(mla_wq_b[j], MLA_HEADS, 1).astype(BF16)
            wkv = mla_wkv_b[j].reshape(MLA_KV_RANK, MLA_HEADS, MLA_NOPE + MLA_V)
            wk = _pad_heads(wkv[:, :, :MLA_NOPE].reshape(MLA_KV_RANK, -1), MLA_HEADS, 1).astype(BF16)
            wv = _pad_heads(wkv[:, :, MLA_NOPE:].reshape(MLA_KV_RANK, -1), MLA_HEADS, 1).astype(BF16)
            ws = [w1, row(mla_q_norm_g[j]), wq, row(mla_kv_norm_g[j]), wk, wv]
            na = N_HEADS_A * HEAD_DIM_A
            wo_a = ab_w_out[j][:na].astype(BF16)
            wo_b = ab_w_out[j][na:].astype(BF16)
            sink = swa_sink[j]

            qa, ka, va, qb, kb, vb = _ab_in(xl, gmix, sc1, sh1, ws, tab_a + tab_b, s, None)
            qa_c, ka_c, va_c, qb_c, kb_c, vb_c = _ab_in(xc, gmix, sc1, sh1, ws, None, n_ctx, ctx_row)
            ga = N_HEADS_A // N_KV_A
            cfg_a = dict(batch=bn, n_kv_total=N_KV_A, n_kv=1, n_group=ga, out_dim=HEAD_DIM_A)
            cfg_b = dict(batch=bn, n_kv_total=MLA_HEADS, n_kv=MLA_STEP_HEADS, n_group=1, out_dim=MLA_V)
            oa = _attention(qa, ka, va, ka_c, va_c, sink, window=True, tq=256, **cfg_a)
            ob = _attention(qb, kb, vb, kb_c, vb_c, None, window=False, tq=256, **cfg_b)
            o_lat, w_o = [oa, ob], [wo_a, wo_b]
            if with_ctx:
                oa_c = _attention(qa_c, None, None, ka_c, va_c, sink, window=False, tq=256, **cfg_a)
                ob_c = _attention(qb_c, None, None, kb_c, vb_c, None, window=False, tq=256, **cfg_b)
                o_ctx = [oa_c, ob_c]
        else:
            w_in = c_w_in[j].astype(BF16)
            qn, kn = row(c_q_norm_g[j]), row(c_k_norm_g[j])
            q, k, v = _c_in(xl, gmix, sc1, sh1, w_in, qn, kn, tab_c, s, None)
            q_c, k_c, v_c = _c_in(xc, gmix, sc1, sh1, w_in, qn, kn, None, n_ctx, ctx_row)
            gc = N_HEADS_C // N_KV_C
            cfg_c = dict(batch=bn, n_kv_total=N_KV_C, n_kv=1, n_group=gc, out_dim=HEAD_DIM_C,
                         window=False, tq=256)
            o = _attention(q, k, v, k_c, v_c, None, **cfg_c)
            o_lat, w_o = [o], [c_w_out[j].astype(BF16)]
            if with_ctx:
                o_ctx = [_attention(q_c, None, None, k_c, v_c, None, **cfg_c)]

        rw = jnp.pad(router_w[i], ((0, 0), (0, LANES - N_EXPERTS)))
        rw_hi = rw.astype(BF16)
        rw = jnp.concatenate([rw_hi, (rw - rw_hi.astype(F32)).astype(BF16)], axis=1)
        rb = jnp.concatenate([router_b[i], jnp.full((LANES - N_EXPERTS,), NEG_INF, F32)]).reshape(1, LANES)
        xl, tok_l, idx_l, gate_l = _out_proj(o_lat, w_o, xl, g1, gffn, sc2, sh2, rw, rb, s, None)
        if with_ctx:
            xc, tok_c, idx_c, gate_c = _out_proj(o_ctx, w_o, xc, g1, gffn, sc2, sh2, rw, rb, n_ctx, ctx_row)
            tokens = [tok_l, tok_c]
            eidx = jnp.concatenate([idx_l, idx_c], axis=0)
            gates = jnp.concatenate([gate_l, gate_c], axis=0)
        else:
            tokens, eidx, gates = [tok_l], idx_l, gate_l

        n_tiles = moe_b_in.shape[-1] // PAIR_TILE
        b1e = moe_b_in[i].reshape(N_EXPERTS, n_tiles, LANES, 2).transpose(0, 1, 3, 2)
        b1e = b1e.reshape(N_EXPERTS, 1, -1)
        b2e = moe_b_out[i][:, None, :]
        ys, dest = _moe(tokens, eidx[:, :TOP_K], moe_w_in, b1e, moe_w_out, b2e, i)

        last = i == depth - 1
        xl = _combine(xl, ys, dest, gates, g2, row(final_norm_g) if last else None,
                      row_off=0, rows_per_batch=s, fixed_mod=None)
        if with_ctx:
            xc = _combine(xc, ys, dest, gates, g2, None,
                          row_off=t_lat, rows_per_batch=n_ctx, fixed_mod=ctx_row)
    return xl.reshape(bn, s, d)
```

```python
import functools

import jax
import jax.numpy as jnp
from jax import lax
from jax.experimental import pallas as pl
from jax.experimental.pallas import tpu as pltpu

GRID_W = 64
N_HEADS_A, N_KV_A, HEAD_DIM_A, WINDOW = 8, 2, 64, 128
MLA_HEADS, MLA_Q_RANK, MLA_KV_RANK, MLA_NOPE, MLA_ROPE, MLA_V = 8, 384, 256, 64, 32, 64
N_HEADS_C, N_KV_C, HEAD_DIM_C = 8, 2, 128
N_EXPERTS, TOP_K = 32, 4
MLA_STEP_HEADS = 4
SWIGLU_ALPHA, SWIGLU_LIMIT = 1.702, 7.0
ROPE_THETA, RMS_EPS, NEG_INF = 10000.0, 1e-6, -1e30
LOG2_E = 1.4426950408889634

LANES = 128
SUBLANES = 8
VMEM_LIMIT_BYTES = 56 * 1024 * 1024

F32 = jnp.float32
BF16 = jnp.bfloat16
HIGHEST = lax.Precision.HIGHEST


def _params(*sem):
    return pltpu.CompilerParams(dimension_semantics=sem, vmem_limit_bytes=VMEM_LIMIT_BYTES)


def _dot(a, b):
    return jnp.dot(a, b, preferred_element_type=F32)


def _dot_t(a, b):
    return lax.dot_general(a, b, (((1,), (1,)), ((), ())), preferred_element_type=F32)


def _rms(x, g):
    return x * lax.rsqrt(jnp.mean(x * x, axis=-1, keepdims=True) + RMS_EPS) * g


def _modnorm(x, g, sc, sh):
    return _rms(x, g) * (1.0 + sc) + sh


def _rope_group(x, cos, sin, half, lo):
    lane = lax.broadcasted_iota(jnp.int32, x.shape, 1)
    first = (lane >= lo) & (lane < lo + half)
    rot = jnp.where(first, pltpu.roll(x, LANES - half, 1), pltpu.roll(x, half, 1))
    return x * cos + rot * sin


def _groups(x):
    return [x[:, i * LANES:(i + 1) * LANES] for i in range(x.shape[1] // LANES)]


def _mod_kernel(c_ref, w_ref, b_ref, o_ref):
    c = c_ref[...]
    a = c * jax.nn.sigmoid(c)
    o_ref[0] = jnp.dot(a, w_ref[0], preferred_element_type=F32, precision=HIGHEST) + b_ref[0]


def _modulation(c_all, ada_w, ada_b):
    depth, d, n = ada_w.shape
    r = c_all.shape[0]
    nt = n // 4
    return pl.pallas_call(
        _mod_kernel,
        out_shape=jax.ShapeDtypeStruct((depth, r, n), F32),
        grid=(depth, n // nt),
        in_specs=[
            pl.BlockSpec((r, d), lambda i, j: (0, 0)),
            pl.BlockSpec((1, d, nt), lambda i, j: (i, 0, j)),
            pl.BlockSpec((1, 1, nt), lambda i, j: (i, 0, j)),
        ],
        out_specs=pl.BlockSpec((1, r, nt), lambda i, j: (i, 0, j)),
        compiler_params=_params("arbitrary", "arbitrary"),
        name="modulation",
    )(c_all, ada_w, ada_b.reshape(depth, 1, n))


def _ab_in_kernel(*refs, rope):
    if rope:
        (x_ref, g_ref, sc_ref, sh_ref, w1_ref, qg_ref, wq_ref, kvg_ref, wk_ref, wv_ref,
         ca_ref, sa_ref, cb_ref, sb_ref, qa_ref, ka_ref, va_ref, qb_ref, kb_ref, vb_ref) = refs
    else:
        (x_ref, g_ref, sc_ref, sh_ref, w1_ref, qg_ref, wq_ref, kvg_ref, wk_ref, wv_ref,
         qa_ref, ka_ref, va_ref, qb_ref, kb_ref, vb_ref) = refs
    h = _modnorm(x_ref[...], g_ref[...], sc_ref[0], sh_ref[0]).astype(BF16)
    p = _dot(h, w1_ref[...])
    nqa = N_HEADS_A * LANES
    nka = N_KV_A * LANES
    o = 0
    qa = p[:, o:o + nqa]; o += nqa
    ka = p[:, o:o + nka]; o += nka
    va = p[:, o:o + nka]; o += nka
    cq = p[:, o:o + MLA_Q_RANK]; o += MLA_Q_RANK
    ckv = p[:, o:o + MLA_KV_RANK]; o += MLA_KV_RANK
    krg = p[:, o:o + LANES]

    qb = _dot(_rms(cq, qg_ref[...]).astype(BF16), wq_ref[...])
    ckv_n = _rms(ckv, kvg_ref[...]).astype(BF16)
    kb = _dot(ckv_n, wk_ref[...])
    vb = _dot(ckv_n, wv_ref[...])

    scale_a = HEAD_DIM_A ** -0.5 * LOG2_E
    scale_b = (MLA_NOPE + MLA_ROPE) ** -0.5 * LOG2_E
    if rope:
        ca, sa, cb, sb = ca_ref[...], sa_ref[...], cb_ref[...], sb_ref[...]
        half_a, half_b = HEAD_DIM_A // 2, MLA_ROPE // 2
        qa_g = [_rope_group(t, ca, sa, half_a, 0) * scale_a for t in _groups(qa)]
        ka_g = [_rope_group(t, ca, sa, half_a, 0) for t in _groups(ka)]
        qb_g = [_rope_group(t, cb, sb, half_b, MLA_NOPE) * scale_b for t in _groups(qb)]
        krg = _rope_group(krg, cb, sb, half_b, MLA_NOPE)
    else:
        qa_g = [t * scale_a for t in _groups(qa)]
        ka_g = _groups(ka)
        qb_g = [t * scale_b for t in _groups(qb)]
    kb_g = [t + krg for t in _groups(kb)]
    qa_ref[...] = jnp.concatenate(qa_g, axis=1).astype(BF16)
    ka_ref[...] = jnp.concatenate(ka_g, axis=1).astype(BF16)
    va_ref[...] = va.astype(BF16)
    qb_ref[...] = jnp.concatenate(qb_g, axis=1).astype(BF16)
    kb_ref[...] = jnp.concatenate(kb_g, axis=1).astype(BF16)
    vb_ref[...] = vb.astype(BF16)


def _row_tile(t, pref=512):
    tm = pref
    while t % tm:
        tm //= 2
    return tm


def _mod_index(rows_per_batch, tm, fixed):
    if fixed is not None:
        return lambda i: (fixed, 0, 0)
    return lambda i: ((i * tm) // rows_per_batch, 0, 0)


def _ab_in(x, g, sc, sh, w, tables, rows_per_batch, fixed_mod):
    t, d = x.shape
    tm = _row_tile(min(t, rows_per_batch))
    rope = tables is not None
    midx = _mod_index(rows_per_batch, tm, fixed_mod)
    full = lambda a: pl.BlockSpec(a.shape, lambda i: (0,) * a.ndim)
    in_specs = [
        pl.BlockSpec((tm, d), lambda i: (i, 0)),
        full(g),
        pl.BlockSpec((1, 1, d), midx),
        pl.BlockSpec((1, 1, d), midx),
    ] + [full(a) for a in w]
    args = [x, g, sc, sh] + list(w)
    if rope:
        nt = rows_per_batch // tm
        for tab in tables:
            in_specs.append(pl.BlockSpec((tm, LANES), lambda i: (i % nt, 0)))
            args.append(tab)
    widths = [N_HEADS_A * LANES, N_KV_A * LANES, N_KV_A * LANES,
              MLA_HEADS * LANES, MLA_HEADS * LANES, MLA_HEADS * LANES]
    return pl.pallas_call(
        functools.partial(_ab_in_kernel, rope=rope),
        out_shape=[jax.ShapeDtypeStruct((t, n), BF16) for n in widths],
        grid=(t // tm,),
        in_specs=in_specs,
        out_specs=[pl.BlockSpec((tm, n), lambda i: (i, 0)) for n in widths],
        compiler_params=_params("arbitrary"),
        name="ab_in_rope" if rope else "ab_in",
    )(*args)


def _c_in_kernel(*refs, rope):
    if rope:
        x_ref, g_ref, sc_ref, sh_ref, w_ref, qn_ref, kn_ref, cc_ref, sc2_ref, q_ref, k_ref, v_ref = refs
    else:
        x_ref, g_ref, sc_ref, sh_ref, w_ref, qn_ref, kn_ref, q_ref, k_ref, v_ref = refs
    h = _modnorm(x_ref[...], g_ref[...], sc_ref[0], sh_ref[0]).astype(BF16)
    p = _dot(h, w_ref[...])
    nq = N_HEADS_C * HEAD_DIM_C
    nk = N_KV_C * HEAD_DIM_C
    q_g = [_rms(t, qn_ref[...]) for t in _groups(p[:, :nq])]
    k_g = [_rms(t, kn_ref[...]) for t in _groups(p[:, nq:nq + nk])]
    scale = HEAD_DIM_C ** -0.5 * LOG2_E
    if rope:
        cc, ss = cc_ref[...], sc2_ref[...]
        q_g = [_rope_group(t, cc, ss, HEAD_DIM_C // 2, 0) for t in q_g]
        k_g = [_rope_group(t, cc, ss, HEAD_DIM_C // 2, 0) for t in k_g]
    q_ref[...] = jnp.concatenate([t * scale for t in q_g], axis=1).astype(BF16)
    k_ref[...] = jnp.concatenate(k_g, axis=1).astype(BF16)
    v_ref[...] = p[:, nq + nk:].astype(BF16)


def _c_in(x, g, sc, sh, w, qn, kn, tables, rows_per_batch, fixed_mod):
    t, d = x.shape
    tm = _row_tile(min(t, rows_per_batch))
    rope = tables is not None
    midx = _mod_index(rows_per_batch, tm, fixed_mod)
    full = lambda a: pl.BlockSpec(a.shape, lambda i: (0,) * a.ndim)
    in_specs = [
        pl.BlockSpec((tm, d), lambda i: (i, 0)),
        full(g),
        pl.BlockSpec((1, 1, d), midx),
        pl.BlockSpec((1, 1, d), midx),
        full(w), full(qn), full(kn),
    ]
    args = [x, g, sc, sh, w, qn, kn]
    if rope:
        nt = rows_per_batch // tm
        for tab in tables:
            in_specs.append(pl.BlockSpec((tm, LANES), lambda i: (i % nt, 0)))
            args.append(tab)
    widths = [N_HEADS_C * HEAD_DIM_C, N_KV_C * HEAD_DIM_C, N_KV_C * HEAD_DIM_C]
    return pl.pallas_call(
        functools.partial(_c_in_kernel, rope=rope),
        out_shape=[jax.ShapeDtypeStruct((t, n), BF16) for n in widths],
        grid=(t // tm,),
        in_specs=in_specs,
        out_specs=[pl.BlockSpec((tm, n), lambda i: (i, 0)) for n in widths],
        compiler_params=_params("arbitrary"),
        name="c_in_rope" if rope else "c_in",
    )(*args)


def _attn_kernel(*refs, n_kv, n_group, out_dim, has_lat, has_sink, window, tq, s_lat, kw):
    refs = list(refs)
    sink_ref = refs.pop(0) if has_sink else None
    q_ref = refs.pop(0)
    if has_lat:
        k1_ref, v1_ref = refs.pop(0), refs.pop(0)
    k2_ref, v2_ref, o_ref = refs
    hk0 = pl.program_id(1) * n_kv
    qi = pl.program_id(2)
    if window:
        ws = pl.multiple_of(jnp.clip(qi * tq - WINDOW, 0, s_lat - kw), LANES)
        rows = pl.ds(ws, kw)
        qpos = qi * tq + lax.broadcasted_iota(jnp.int32, (tq, kw), 0)
        kpos = ws + lax.broadcasted_iota(jnp.int32, (tq, kw), 1)
        ok = jnp.abs(kpos - qpos) <= WINDOW
    else:
        rows = slice(None)
    outs = []
    for j in range(n_kv):
        lanes = slice(j * LANES, (j + 1) * LANES)
        k2, v2 = k2_ref[:, lanes], v2_ref[:, lanes]
        if has_lat:
            k1, v1 = k1_ref[rows, lanes], v1_ref[rows, lanes]
        for g in range(n_group):
            h = j * n_group + g
            q = q_ref[:, h * LANES:(h + 1) * LANES]
            s2 = _dot_t(q, k2)
            m = jnp.max(s2, axis=-1, keepdims=True)
            if has_lat:
                s1 = _dot_t(q, k1)
                if window:
                    s1 = jnp.where(ok, s1, NEG_INF)
                m = jnp.maximum(m, jnp.max(s1, axis=-1, keepdims=True))
            if has_sink:
                sk = sink_ref[hk0 * n_group + h] * LOG2_E
                m = jnp.maximum(m, sk)
            e2 = jnp.exp2(s2 - m)
            den = jnp.sum(e2, axis=-1, keepdims=True)
            acc = _dot(e2.astype(BF16), v2)
            if has_lat:
                e1 = jnp.exp2(s1 - m)
                den = den + jnp.sum(e1, axis=-1, keepdims=True)
                acc = acc + _dot(e1.astype(BF16), v1)
            if has_sink:
                den = den + jnp.exp2(sk - m)
            outs.append(acc / den)
    if out_dim == LANES:
        for h, o in enumerate(outs):
            o_ref[:, h * LANES:(h + 1) * LANES] = o.astype(BF16)
    else:
        lane = lax.broadcasted_iota(jnp.int32, (tq, LANES), 1)
        for p in range(len(outs) // 2):
            pair = jnp.where(lane < out_dim, outs[2 * p], pltpu.roll(outs[2 * p + 1], out_dim, 1))
            o_ref[:, p * LANES:(p + 1) * LANES] = pair.astype(BF16)


def _attention(q, k_lat, v_lat, k_ctx, v_ctx, sink, *, batch, n_kv_total, n_kv, n_group, out_dim,
               window, tq):
    tq_total = q.shape[0] // batch
    tq = min(tq, tq_total)
    n_ctx = k_ctx.shape[0] // batch
    has_lat = k_lat is not None
    has_sink = sink is not None
    s_lat = k_lat.shape[0] // batch if has_lat else 0
    kw = min(tq + 2 * WINDOW, s_lat) if window else 0
    nq = tq_total // tq
    n_heads = n_kv * n_group
    assert out_dim == LANES or (2 * out_dim == LANES and n_heads % 2 == 0)
    in_specs, args = [], []
    if has_sink:
        in_specs.append(pl.BlockSpec(memory_space=pltpu.SMEM))
        args.append(sink)
    in_specs.append(pl.BlockSpec((tq, n_heads * LANES), lambda b, h, i: (b * nq + i, h)))
    args.append(q)
    if has_lat:
        in_specs += [pl.BlockSpec((s_lat, n_kv * LANES), lambda b, h, i: (b, h))] * 2
        args += [k_lat, v_lat]
    in_specs += [pl.BlockSpec((n_ctx, n_kv * LANES), lambda b, h, i: (b, h))] * 2
    args += [k_ctx, v_ctx]
    n_steps = n_kv_total // n_kv
    return pl.pallas_call(
        functools.partial(_attn_kernel, n_kv=n_kv, n_group=n_group, out_dim=out_dim, has_lat=has_lat,
                          has_sink=has_sink, window=window, tq=tq, s_lat=s_lat, kw=kw),
        out_shape=jax.ShapeDtypeStruct((q.shape[0], n_steps * n_heads * out_dim), BF16),
        grid=(batch, n_steps, nq),
        in_specs=in_specs,
        out_specs=pl.BlockSpec((tq, n_heads * out_dim), lambda b, h, i: (b * nq + i, h)),
        compiler_params=_params("arbitrary", "arbitrary", "arbitrary"),
        name="attn_win" if window else ("attn_dense" if has_lat else "attn_ctx"),
    )(*args)


def _out_kernel(*refs, n_parts, tm):
    o_refs = refs[:n_parts]
    w_refs = refs[n_parts:2 * n_parts]
    (x_ref, g1_ref, gn_ref, sc_ref, sh_ref, rw_ref, rb_ref,
     xo_ref, tok_ref, idx_ref, gate_ref) = refs[2 * n_parts:]
    y = _dot(o_refs[0][...], w_refs[0][...])
    for o_r, w_r in zip(o_refs[1:], w_refs[1:]):
        y = y + _dot(o_r[...], w_r[...])
    x = x_ref[...] + g1_ref[0] * y
    xo_ref[...] = x
    tok = _modnorm(x, gn_ref[...], sc_ref[0], sh_ref[0])
    for j in range(tok.shape[1] // LANES):
        tok_ref[pl.ds(j, tm, stride=SUBLANES), :] = tok[:, j * LANES:(j + 1) * LANES]
    tok_hi = tok.astype(BF16)
    tok_lo = (tok - tok_hi.astype(F32)).astype(BF16)
    hh_hl = _dot(tok_hi, rw_ref[...])
    logits = (hh_hl[:, :LANES] + hh_hl[:, LANES:] + _dot(tok_lo, rw_ref[:, :LANES])) + rb_ref[...]
    lane = lax.broadcasted_iota(jnp.int32, logits.shape, 1).astype(F32)
    vals, idxs = [], []
    l = logits
    for _ in range(TOP_K):
        m = jnp.max(l, axis=-1, keepdims=True)
        idx = jnp.min(jnp.where(l == m, lane, float(LANES)), axis=-1, keepdims=True)
        vals.append(m)
        idxs.append(idx)
        l = jnp.where(lane == idx, -jnp.inf, l)
    es = [jnp.exp(v - vals[0]) for v in vals]
    den = es[0]
    for e in es[1:]:
        den = den + e
    lane8 = lax.broadcasted_iota(jnp.int32, (tm, SUBLANES), 1)
    io = jnp.zeros((tm, SUBLANES), jnp.int32)
    go = jnp.zeros((tm, SUBLANES), F32)
    for k in range(TOP_K):
        io = jnp.where(lane8 == k, idxs[k].astype(jnp.int32), io)
        go = jnp.where(lane8 == k, es[k] / den, go)
    idx_ref[...] = io
    gate_ref[...] = go


def _out_proj(o_parts, w_parts, x, g1, gn, sc, sh, rw, rb, rows_per_batch, fixed_mod):
    t, d = x.shape
    tm = _row_tile(min(t, rows_per_batch), 512)
    midx = _mod_index(rows_per_batch, tm, fixed_mod)
    full = lambda a: pl.BlockSpec(a.shape, lambda i: (0,) * a.ndim)
    n_parts = len(o_parts)
    in_specs = ([pl.BlockSpec((tm, o.shape[1]), lambda i: (i, 0)) for o in o_parts]
                + [full(w) for w in w_parts]
                + [pl.BlockSpec((tm, d), lambda i: (i, 0)),
                   pl.BlockSpec((1, 1, d), midx), full(gn),
                   pl.BlockSpec((1, 1, d), midx), pl.BlockSpec((1, 1, d), midx),
                   full(rw), full(rb)])
    return pl.pallas_call(
        functools.partial(_out_kernel, n_parts=n_parts, tm=tm),
        out_shape=[jax.ShapeDtypeStruct((t, d), F32),
                   jax.ShapeDtypeStruct((t * SUBLANES, d // SUBLANES), F32),
                   jax.ShapeDtypeStruct((t, SUBLANES), jnp.int32),
                   jax.ShapeDtypeStruct((t, SUBLANES), F32)],
        grid=(t // tm,),
        in_specs=in_specs,
        out_specs=[pl.BlockSpec((tm, d), lambda i: (i, 0)),
                   pl.BlockSpec((tm * SUBLANES, d // SUBLANES), lambda i: (i, 0)),
                   pl.BlockSpec((tm, SUBLANES), lambda i: (i, 0)),
                   pl.BlockSpec((tm, SUBLANES), lambda i: (i, 0))],
        compiler_params=_params("arbitrary"),
        name="out_proj",
    )(*o_parts, *w_parts, x, g1, gn, sc, sh, rw, rb)


DISPATCH_TILE = 512
PAIR_TILE = 2 * LANES
WEIGHT_SPLIT = 2


def _row(ref, r):
    return ref.at[pl.ds(pl.multiple_of(r * SUBLANES, SUBLANES), SUBLANES)]


def _dispatch_kernel(*refs, tm, part_tiles, n_experts):
    n_parts = len(part_tiles)
    plo_ref, phi_ref, dest_ref = refs[:3]
    tok_refs = refs[3:3 + n_parts]
    p_ref, w1_ref, w2_ref, xs_ref, o1_ref, o2_ref, zero_ref, sem, zsem = refs[3 + n_parts:]
    i = pl.program_id(0)

    @pl.when(i == 0)
    def _():
        zero_ref[...] = jnp.zeros(zero_ref.shape, zero_ref.dtype)

        def fill_range(e, carry):
            lo, hi = plo_ref[e], phi_ref[e]

            def fill_chunk(c, carry):
                c_lo = lo + c * tm
                c_hi = jnp.minimum(c_lo + tm, hi)

                def start_row(r, carry):
                    pltpu.make_async_copy(zero_ref, _row(xs_ref, r), zsem).start()
                    return carry

                def wait_row(r, carry):
                    pltpu.make_async_copy(zero_ref, _row(xs_ref, r), zsem).wait()
                    return carry

                lax.fori_loop(c_lo, c_hi, start_row, carry)
                return lax.fori_loop(c_lo, c_hi, wait_row, carry)

            return lax.fori_loop(0, (hi - lo + tm - 1) // tm, fill_chunk, carry)

        lax.fori_loop(0, plo_ref.shape[0], fill_range, 0)

    def start_rows(tok_ref):
        def body(t, carry):
            for u in range(2):
                r = t * 2 + u
                src = _row(tok_ref, r)
                for k in range(TOP_K):
                    pltpu.make_async_copy(src, _row(xs_ref, dest_ref[0, 0, r * TOP_K + k]), sem).start(
                        priority=k % 2)
            return carry

        lax.fori_loop(0, tm // 2, body, 0)

    first = 0
    for tok_ref, n in zip(tok_refs, part_tiles):
        pl.when((i >= first) & (i < first + n))(functools.partial(start_rows, tok_ref))
        first += n

    @pl.when(i < n_experts * WEIGHT_SPLIT)
    def _():
        perm = p_ref[...]
        for j in range(w1_ref.shape[3] // PAIR_TILE):
            cols = slice(j * PAIR_TILE, (j + 1) * PAIR_TILE)
            o1_ref[0, :, cols] = _dot(w1_ref[0, 0, :, cols].astype(BF16), perm).astype(BF16)
        o2_ref[0] = w2_ref[0, 0].astype(BF16)

    @pl.when(i < sum(part_tiles))
    def _():
        done = xs_ref.at[pl.ds(0, tm * TOP_K * SUBLANES)]
        pltpu.make_async_copy(done, done, sem).wait()


def _dispatch(token_parts, dest, pad_lo, pad_hi, cap, w_in, w_out, layer):
    t = dest.shape[0]
    tm = _row_tile(min(p.shape[0] // SUBLANES for p in token_parts), DISPATCH_TILE)
    part_tiles = tuple(p.shape[0] // SUBLANES // tm for p in token_parts)
    n_tiles = sum(part_tiles)
    assert n_tiles * tm == t
    _, e, d, f2 = w_in.shape
    lane = jnp.arange(LANES)
    perm = jnp.zeros((PAIR_TILE, PAIR_TILE), F32).at[2 * lane, lane].set(1.0)
    perm = perm.at[2 * lane + 1, LANES + lane].set(1.0).astype(BF16)
    in_specs = [pl.BlockSpec((1, 1, tm * TOP_K), lambda i, lo, hi: (jnp.minimum(i, n_tiles - 1), 0, 0),
                             memory_space=pltpu.SMEM)]
    first = 0
    for n in part_tiles:
        in_specs.append(pl.BlockSpec(
            (tm * SUBLANES, LANES), lambda i, lo, hi, first=first, n=n: (jnp.clip(i - first, 0, n - 1), 0)))
        first += n
    ws = WEIGHT_SPLIT
    assert f2 % (ws * PAIR_TILE) == 0 and (f2 // 2) % (ws * SUBLANES * 2) == 0
    share = lambda i: jnp.minimum(i, e * ws - 1)
    in_specs += [pl.BlockSpec((PAIR_TILE, PAIR_TILE), lambda i, lo, hi: (0, 0)),
                 pl.BlockSpec((1, 1, d, f2 // ws), lambda i, lo, hi: (layer, share(i) // ws, 0, share(i) % ws)),
                 pl.BlockSpec((1, 1, f2 // 2 // ws, d),
                              lambda i, lo, hi: (layer, share(i) // ws, share(i) % ws, 0))]
    grid_spec = pltpu.PrefetchScalarGridSpec(
        num_scalar_prefetch=2,
        grid=(max(n_tiles, e * ws),),
        in_specs=in_specs,
        out_specs=[pl.BlockSpec(memory_space=pl.ANY),
                   pl.BlockSpec((1, d, f2 // ws), lambda i, lo, hi: (share(i) // ws, 0, share(i) % ws)),
                   pl.BlockSpec((1, f2 // 2 // ws, d), lambda i, lo, hi: (share(i) // ws, share(i) % ws, 0))],
        scratch_shapes=[pltpu.VMEM((SUBLANES, LANES), F32),
                        pltpu.SemaphoreType.DMA, pltpu.SemaphoreType.DMA],
    )
    return pl.pallas_call(
        functools.partial(_dispatch_kernel, tm=tm, part_tiles=part_tiles, n_experts=e),
        out_shape=[jax.ShapeDtypeStruct((cap * SUBLANES, LANES), F32),
                   jax.ShapeDtypeStruct(w_in.shape[1:], BF16), jax.ShapeDtypeStruct(w_out.shape[1:], BF16)],
        grid_spec=grid_spec,
        compiler_params=_params("arbitrary"),
        name="dispatch",
    )(pad_lo, pad_hi, dest.reshape(n_tiles, 1, tm * TOP_K), *token_parts, perm, w_in, w_out)


EXPERT_BLOCK = 512


def _expert_kernel(be_ref, nb_ref, x_ref, w1_ref, b1_ref, w2_ref, b2_ref, o_ref, *, bm):
    del be_ref
    b = pl.program_id(0)
    n_j = w1_ref.shape[1] // LANES

    @pl.when(b < nb_ref[0])
    def _():
        x = jnp.concatenate([x_ref[pl.ds(j, bm, stride=SUBLANES), :] for j in range(n_j)], axis=1)
        u = _dot(x.astype(BF16), w1_ref[0]) + b1_ref[0]
        acts = []
        for j in range(u.shape[1] // PAIR_TILE):
            glu = jnp.minimum(u[:, j * PAIR_TILE:j * PAIR_TILE + LANES], SWIGLU_LIMIT)
            lin = jnp.clip(u[:, j * PAIR_TILE + LANES:(j + 1) * PAIR_TILE], -SWIGLU_LIMIT, SWIGLU_LIMIT)
            acts.append(glu * jax.nn.sigmoid(SWIGLU_ALPHA * glu) * (lin + 1.0))
        a = jnp.concatenate(acts, axis=1)
        y = _dot(a.astype(BF16), w2_ref[0]) + b2_ref[0]
        for j in range(y.shape[1] // LANES):
            o_ref[pl.ds(j, bm, stride=SUBLANES), :] = y[:, j * LANES:(j + 1) * LANES]

    @pl.when(b >= nb_ref[0])
    def _():
        o_ref[...] = jnp.zeros(o_ref.shape, o_ref.dtype)


def _expert_mlp(xs, blk_expert, n_used, w1, b1, w2, b2):
    bm = EXPERT_BLOCK
    nblk = blk_expert.shape[0]
    _, d, f2 = w1.shape
    grid_spec = pltpu.PrefetchScalarGridSpec(
        num_scalar_prefetch=2,
        grid=(nblk,),
        in_specs=[
            pl.BlockSpec((bm * SUBLANES, LANES), lambda b, be, nb: (b, 0)),
            pl.BlockSpec((1, d, f2), lambda b, be, nb: (be[b], 0, 0)),
            pl.BlockSpec((1, 1, f2), lambda b, be, nb: (be[b], 0, 0)),
            pl.BlockSpec((1, f2 // 2, d), lambda b, be, nb: (be[b], 0, 0)),
            pl.BlockSpec((1, 1, d), lambda b, be, nb: (be[b], 0, 0)),
        ],
        out_specs=pl.BlockSpec((bm * SUBLANES, LANES), lambda b, be, nb: (b, 0)),
    )
    return pl.pallas_call(
        functools.partial(_expert_kernel, bm=bm),
        out_shape=jax.ShapeDtypeStruct(xs.shape, F32),
        grid_spec=grid_spec,
        compiler_params=_params("arbitrary"),
        name="expert_mlp",
    )(blk_expert, n_used, xs, w1, b1, w2, b2)


def _combine_kernel(*refs, tm, final):
    if final:
        dest_ref, x_ref, ys_ref, gate_ref, g2_ref, fg_ref, o_ref, ybuf, sem = refs
    else:
        dest_ref, x_ref, ys_ref, gate_ref, g2_ref, o_ref, ybuf, sem = refs

    def body(t, carry):
        for u in range(2):
            r = t * 2 + u
            for k in range(TOP_K):
                pltpu.make_async_copy(_row(ys_ref, dest_ref[0, 0, r * TOP_K + k]),
                                      _row(ybuf, k * tm + r), sem).start(priority=k % 2)
        return carry

    lax.fori_loop(0, tm // 2, body, 0)
    pltpu.make_async_copy(ybuf, ybuf, sem).wait()

    gate = gate_ref[...]
    parts = []
    for j in range(x_ref.shape[1] // LANES):
        f = gate[:, 0:1] * ybuf[pl.ds(j, tm, stride=SUBLANES), :]
        for k in range(1, TOP_K):
            f = f + gate[:, k:k + 1] * ybuf[pl.ds(k * tm * SUBLANES + j, tm, stride=SUBLANES), :]
        parts.append(f)
    x = x_ref[...] + g2_ref[0] * jnp.concatenate(parts, axis=1)
    if final:
        x = _rms(x, fg_ref[...])
    o_ref[...] = x


def _combine(x, ys, dest, gates, g2, final_g, *, row_off, rows_per_batch, fixed_mod):
    t, d = x.shape
    tm = _row_tile(min(t, rows_per_batch), 256)
    midx = _mod_index(rows_per_batch, tm, fixed_mod)
    final = final_g is not None
    off = row_off // tm
    t_all = dest.shape[0]
    in_specs = [pl.BlockSpec((1, 1, tm * TOP_K), lambda i: (off + i, 0, 0), memory_space=pltpu.SMEM),
                pl.BlockSpec((tm, d), lambda i: (i, 0)),
                pl.BlockSpec(memory_space=pl.ANY),
                pl.BlockSpec((tm, SUBLANES), lambda i: (off + i, 0)),
                pl.BlockSpec((1, 1, d), midx)]
    args = [dest.reshape(t_all // tm, 1, tm * TOP_K), x, ys, gates, g2]
    if final:
        in_specs.append(pl.BlockSpec(final_g.shape, lambda i: (0, 0)))
        args.append(final_g)
    return pl.pallas_call(
        functools.partial(_combine_kernel, tm=tm, final=final),
        out_shape=jax.ShapeDtypeStruct((t, d), F32),
        grid=(t // tm,),
        in_specs=in_specs,
        out_specs=pl.BlockSpec((tm, d), lambda i: (i, 0)),
        scratch_shapes=[pltpu.VMEM((TOP_K * tm * SUBLANES, LANES), F32), pltpu.SemaphoreType.DMA],
        compiler_params=_params("arbitrary"),
        name="combine_final" if final else "combine",
    )(*args)


def _pad_heads(w, n_heads, axis):
    shape = list(w.shape)
    hd = shape[axis] // n_heads
    w = w.reshape(shape[:axis] + [n_heads, hd] + shape[axis + 1:])
    pad = [(0, 0)] * w.ndim
    pad[axis + 1] = (0, LANES - hd)
    w = jnp.pad(w, pad)
    shape[axis] = n_heads * LANES
    return w.reshape(shape)


def _rope_tables(s, rot_dim, lo):
    pos = jnp.arange(s, dtype=jnp.int32)
    rows, cols = (pos // GRID_W).astype(F32), (pos % GRID_W).astype(F32)
    quarter = rot_dim // 4
    inv = ROPE_THETA ** (-jnp.arange(quarter, dtype=F32) / quarter)
    ang = jnp.concatenate([rows[:, None] * inv, cols[:, None] * inv], axis=-1)
    cos, sin = jnp.cos(ang), jnp.sin(ang)
    hi = LANES - lo - rot_dim
    cos_t = jnp.concatenate([jnp.ones((s, lo), F32), cos, cos, jnp.ones((s, hi), F32)], axis=1)
    sin_t = jnp.concatenate([jnp.zeros((s, lo), F32), -sin, sin, jnp.zeros((s, hi), F32)], axis=1)
    return cos_t, sin_t


def _moe_plan(eidx, bm):
    t = eidx.shape[0]
    ids = jnp.arange(N_EXPERTS, dtype=jnp.int32)
    sel = eidx[:, :, None] == ids
    hit = sel.astype(jnp.int32).sum(1)
    chunk = _row_tile(t, 256)
    tri = jnp.tril(jnp.ones((chunk, chunk), F32))
    within = jnp.einsum("ij,cjk->cik", tri, hit.astype(F32).reshape(t // chunk, chunk, N_EXPERTS),
                        precision=HIGHEST)
    totals = within[:, -1, :]
    incl = (within + (jnp.cumsum(totals, axis=0) - totals)[:, None, :]).reshape(t, N_EXPERTS)
    incl = incl.astype(jnp.int32)
    counts = incl[-1]
    rank = jnp.where(sel, (incl - hit)[:, None, :], 0).sum(-1)
    padded = (counts + bm - 1) // bm * bm
    pends = jnp.cumsum(padded)
    pstarts = pends - padded
    dest = jnp.where(sel, pstarts, 0).sum(-1) + rank
    n_assign = t * TOP_K
    nblk = -(-(n_assign + N_EXPERTS * (bm - 1)) // bm)
    blk_start = jnp.arange(nblk, dtype=jnp.int32) * bm
    blk_expert = jnp.minimum((blk_start[:, None] >= pends[None, :]).sum(1), N_EXPERTS - 1)
    n_used = (pends[-1] // bm).astype(jnp.int32).reshape(1)
    pad_lo = jnp.concatenate([pstarts + counts, pends[-1:]]).astype(jnp.int32)
    pad_hi = jnp.concatenate([pends, jnp.full((1,), nblk * bm, pends.dtype)]).astype(jnp.int32)
    return dest.astype(jnp.int32), blk_expert.astype(jnp.int32), n_used, pad_lo, pad_hi


def _moe(token_parts, eidx, w_in, b1, w_out, b2, layer):
    dest, blk_expert, n_used, pad_lo, pad_hi = _moe_plan(eidx, EXPERT_BLOCK)
    xs, w1, w2 = _dispatch(token_parts, dest, pad_lo, pad_hi, blk_expert.shape[0] * EXPERT_BLOCK,
                           w_in, w_out, layer)
    return _expert_mlp(xs, blk_expert, n_used, w1, b1, w2, b2), dest


def kernel(x, c, ctx, c_ctx, ada_w, ada_b, norm_mix_g, norm_ffn_g, ab_w_in, mla_q_norm_g, mla_wq_b,
           mla_kv_norm_g, mla_wkv_b, swa_sink, ab_w_out, c_w_in, c_q_norm_g, c_k_norm_g, c_w_out,
           router_w, router_b, moe_w_in, moe_b_in, moe_w_out, moe_b_out, final_norm_g):
    bn, s, d = x.shape
    n_ctx = ctx.shape[1]
    depth = ada_w.shape[0]
    t_lat, t_ctx = bn * s, bn * n_ctx
    row = lambda v: v.reshape(1, -1)

    n_mod = -(-(bn + 1) // SUBLANES) * SUBLANES
    c_all = jnp.concatenate([c, c_ctx[None], jnp.zeros((n_mod - bn - 1, d), F32)], axis=0)
    mod = _modulation(c_all, ada_w, ada_b).reshape(depth, n_mod, 6, 1, d)
    ctx_row = bn

    tab_a = _rope_tables(s, HEAD_DIM_A, 0)
    tab_b = _rope_tables(s, MLA_ROPE, MLA_NOPE)
    tab_c = _rope_tables(s, HEAD_DIM_C, 0)

    xl = x.reshape(t_lat, d)
    xc = ctx.reshape(t_ctx, d)
    for i in range(depth):
        with_ctx = i < depth - 1
        j = i // 2
        sh1, sc1, g1, sh2, sc2, g2 = [mod[i, :, m] for m in range(6)]
        gmix, gffn = row(norm_mix_g[i]), row(norm_ffn_g[i])
        if i % 2 == 0:
            w_in = ab_w_in[j]
            sizes = [N_HEADS_A * HEAD_DIM_A, N_KV_A * HEAD_DIM_A, N_KV_A * HEAD_DIM_A,
                     MLA_Q_RANK, MLA_KV_RANK, MLA_ROPE]
            offs = [0]
            for n in sizes:
                offs.append(offs[-1] + n)
            cols = [w_in[:, offs[m]:offs[m + 1]] for m in range(6)]
            krg = jnp.pad(cols[5], ((0, 0), (MLA_NOPE, LANES - MLA_NOPE - MLA_ROPE)))
            w1 = jnp.concatenate([_pad_heads(cols[0], N_HEADS_A, 1), _pad_heads(cols[1], N_KV_A, 1),
                                  _pad_heads(cols[2], N_KV_A, 1), cols[3], cols[4], krg], axis=1).astype(BF16)
            wq = _pad_heads(mla_wq_b[j], MLA_HEADS, 1).astype(BF16)
            wkv = mla_wkv_b[j].reshape(MLA_KV_RANK, MLA_HEADS, MLA_NOPE + MLA_V)
            wk = _pad_heads(wkv[:, :, :MLA_NOPE].reshape(MLA_KV_RANK, -1), MLA_HEADS, 1).astype(BF16)
            wv = _pad_heads(wkv[:, :, MLA_NOPE:].reshape(MLA_KV_RANK, -1), MLA_HEADS, 1).astype(BF16)
            ws = [w1, row(mla_q_norm_g[j]), wq, row(mla_kv_norm_g[j]), wk, wv]
            na = N_HEADS_A * HEAD_DIM_A
            wo_a = ab_w_out[j][:na].astype(BF16)
            wo_b = ab_w_out[j][na:].astype(BF16)
            sink = swa_sink[j]

            qa, ka, va, qb, kb, vb = _ab_in(xl, gmix, sc1, sh1, ws, tab_a + tab_b, s, None)
            qa_c, ka_c, va_c, qb_c, kb_c, vb_c = _ab_in(xc, gmix, sc1, sh1, ws, None, n_ctx, ctx_row)
            ga = N_HEADS_A // N_KV_A
            cfg_a = dict(batch=bn, n_kv_total=N_KV_A, n_kv=1, n_group=ga, out_dim=HEAD_DIM_A)
            cfg_b = dict(batch=bn, n_kv_total=MLA_HEADS, n_kv=MLA_STEP_HEADS, n_group=1, out_dim=MLA_V)
            oa = _attention(qa, ka, va, ka_c, va_c, sink, window=True, tq=256, **cfg_a)
            ob = _attention(qb, kb, vb, kb_c, vb_c, None, window=False, tq=512, **cfg_b)
            o_lat, w_o = [oa, ob], [wo_a, wo_b]
            if with_ctx:
                oa_c = _attention(qa_c, None, None, ka_c, va_c, sink, window=False, tq=256, **cfg_a)
                ob_c = _attention(qb_c, None, None, kb_c, vb_c, None, window=False, tq=256, **cfg_b)
                o_ctx = [oa_c, ob_c]
        else:
            w_in = c_w_in[j].astype(BF16)
            qn, kn = row(c_q_norm_g[j]), row(c_k_norm_g[j])
            q, k, v = _c_in(xl, gmix, sc1, sh1, w_in, qn, kn, tab_c, s, None)
            q_c, k_c, v_c = _c_in(xc, gmix, sc1, sh1, w_in, qn, kn, None, n_ctx, ctx_row)
            gc = N_HEADS_C // N_KV_C
            cfg_c = dict(batch=bn, n_kv_total=N_KV_C, n_kv=1, n_group=gc, out_dim=HEAD_DIM_C,
                         window=False, tq=512)
            o = _attention(q, k, v, k_c, v_c, None, **cfg_c)
            o_lat, w_o = [o], [c_w_out[j].astype(BF16)]
            if with_ctx:
                o_ctx = [_attention(q_c, None, None, k_c, v_c, None, **cfg_c)]

        rw = jnp.pad(router_w[i], ((0, 0), (0, LANES - N_EXPERTS)))
        rw_hi = rw.astype(BF16)
        rw = jnp.concatenate([rw_hi, (rw - rw_hi.astype(F32)).astype(BF16)], axis=1)
        rb = jnp.concatenate([router_b[i], jnp.full((LANES - N_EXPERTS,), NEG_INF, F32)]).reshape(1, LANES)
        xl, tok_l, idx_l, gate_l = _out_proj(o_lat, w_o, xl, g1, gffn, sc2, sh2, rw, rb, s, None)
        if with_ctx:
            xc, tok_c, idx_c, gate_c = _out_proj(o_ctx, w_o, xc, g1, gffn, sc2, sh2, rw, rb, n_ctx, ctx_row)
            tokens = [tok_l, tok_c]
            eidx = jnp.concatenate([idx_l, idx_c], axis=0)
            gates = jnp.concatenate([gate_l, gate_c], axis=0)
        else:
            tokens, eidx, gates = [tok_l], idx_l, gate_l

        n_tiles = moe_b_in.shape[-1] // PAIR_TILE
        b1e = moe_b_in[i].reshape(N_EXPERTS, n_tiles, LANES, 2).transpose(0, 1, 3, 2)
        b1e = b1e.reshape(N_EXPERTS, 1, -1)
        b2e = moe_b_out[i][:, None, :]
        ys, dest = _moe(tokens, eidx[:, :TOP_K], moe_w_in, b1e, moe_w_out, b2e, i)

        last = i == depth - 1
        xl = _combine(xl, ys, dest, gates, g2, row(final_norm_g) if last else None,
                      row_off=0, rows_per_batch=s, fixed_mod=None)
        if with_ctx:
            xc = _combine(xc, ys, dest, gates, g2, None,
                          row_off=t_lat, rows_per_batch=n_ctx, fixed_mod=ctx_row)
    return xl.reshape(bn, s, d)
```

```python
import functools

import jax
import jax.numpy as jnp
from jax import lax
from jax.experimental import pallas as pl
from jax.experimental.pallas import tpu as pltpu

GRID_W = 64
N_HEADS_A, N_KV_A, HEAD_DIM_A, WINDOW = 8, 2, 64, 128
MLA_HEADS, MLA_Q_RANK, MLA_KV_RANK, MLA_NOPE, MLA_ROPE, MLA_V = 8, 384, 256, 64, 32, 64
N_HEADS_C, N_KV_C, HEAD_DIM_C = 8, 2, 128
N_EXPERTS, TOP_K = 32, 4
MLA_STEP_HEADS = 4
SWIGLU_ALPHA, SWIGLU_LIMIT = 1.702, 7.0
ROPE_THETA, RMS_EPS, NEG_INF = 10000.0, 1e-6, -1e30
LOG2_E = 1.4426950408889634

LANES = 128
SUBLANES = 8
VMEM_LIMIT_BYTES = 56 * 1024 * 1024

F32 = jnp.float32
BF16 = jnp.bfloat16
HIGHEST = lax.Precision.HIGHEST


def _params(*sem):
    return pltpu.CompilerParams(dimension_semantics=sem, vmem_limit_bytes=VMEM_LIMIT_BYTES)


def _dot(a, b):
    return jnp.dot(a, b, preferred_element_type=F32)


def _dot_t(a, b):
    return lax.dot_general(a, b, (((1,), (1,)), ((), ())), preferred_element_type=F32)


def _rms(x, g):
    return x * lax.rsqrt(jnp.mean(x * x, axis=-1, keepdims=True) + RMS_EPS) * g


def _modnorm(x, g, sc, sh):
    return _rms(x, g) * (1.0 + sc) + sh


def _rope_group(x, cos, sin, half, lo):
    lane = lax.broadcasted_iota(jnp.int32, x.shape, 1)
    first = (lane >= lo) & (lane < lo + half)
    rot = jnp.where(first, pltpu.roll(x, LANES - half, 1), pltpu.roll(x, half, 1))
    return x * cos + rot * sin


def _groups(x):
    return [x[:, i * LANES:(i + 1) * LANES] for i in range(x.shape[1] // LANES)]


def _mod_kernel(c_ref, w_ref, b_ref, o_ref):
    c = c_ref[...]
    a = c * jax.nn.sigmoid(c)
    o_ref[0] = jnp.dot(a, w_ref[0], preferred_element_type=F32, precision=HIGHEST) + b_ref[0]


def _modulation(c_all, ada_w, ada_b):
    depth, d, n = ada_w.shape
    r = c_all.shape[0]
    nt = n // 4
    return pl.pallas_call(
        _mod_kernel,
        out_shape=jax.ShapeDtypeStruct((depth, r, n), F32),
        grid=(depth, n // nt),
        in_specs=[
            pl.BlockSpec((r, d), lambda i, j: (0, 0)),
            pl.BlockSpec((1, d, nt), lambda i, j: (i, 0, j)),
            pl.BlockSpec((1, 1, nt), lambda i, j: (i, 0, j)),
        ],
        out_specs=pl.BlockSpec((1, r, nt), lambda i, j: (i, 0, j)),
        compiler_params=_params("arbitrary", "arbitrary"),
        name="modulation",
    )(c_all, ada_w, ada_b.reshape(depth, 1, n))


def _ab_in_kernel(*refs, rope):
    if rope:
        (x_ref, g_ref, sc_ref, sh_ref, w1_ref, qg_ref, wq_ref, kvg_ref, wk_ref, wv_ref,
         ca_ref, sa_ref, cb_ref, sb_ref, qa_ref, ka_ref, va_ref, qb_ref, kb_ref, vb_ref) = refs
    else:
        (x_ref, g_ref, sc_ref, sh_ref, w1_ref, qg_ref, wq_ref, kvg_ref, wk_ref, wv_ref,
         qa_ref, ka_ref, va_ref, qb_ref, kb_ref, vb_ref) = refs
    h = _modnorm(x_ref[...], g_ref[...], sc_ref[0], sh_ref[0]).astype(BF16)
    p = _dot(h, w1_ref[...])
    nqa = N_HEADS_A * LANES
    nka = N_KV_A * LANES
    o = 0
    qa = p[:, o:o + nqa]; o += nqa
    ka = p[:, o:o + nka]; o += nka
    va = p[:, o:o + nka]; o += nka
    cq = p[:, o:o + MLA_Q_RANK]; o += MLA_Q_RANK
    ckv = p[:, o:o + MLA_KV_RANK]; o += MLA_KV_RANK
    krg = p[:, o:o + LANES]

    qb = _dot(_rms(cq, qg_ref[...]).astype(BF16), wq_ref[...])
    ckv_n = _rms(ckv, kvg_ref[...]).astype(BF16)
    kb = _dot(ckv_n, wk_ref[...])
    vb = _dot(ckv_n, wv_ref[...])

    scale_a = HEAD_DIM_A ** -0.5 * LOG2_E
    scale_b = (MLA_NOPE + MLA_ROPE) ** -0.5 * LOG2_E
    if rope:
        ca, sa, cb, sb = ca_ref[...], sa_ref[...], cb_ref[...], sb_ref[...]
        half_a, half_b = HEAD_DIM_A // 2, MLA_ROPE // 2
        qa_g = [_rope_group(t, ca, sa, half_a, 0) * scale_a for t in _groups(qa)]
        ka_g = [_rope_group(t, ca, sa, half_a, 0) for t in _groups(ka)]
        qb_g = [_rope_group(t, cb, sb, half_b, MLA_NOPE) * scale_b for t in _groups(qb)]
        krg = _rope_group(krg, cb, sb, half_b, MLA_NOPE)
    else:
        qa_g = [t * scale_a for t in _groups(qa)]
        ka_g = _groups(ka)
        qb_g = [t * scale_b for t in _groups(qb)]
    kb_g = [t + krg for t in _groups(kb)]
    qa_ref[...] = jnp.concatenate(qa_g, axis=1).astype(BF16)
    ka_ref[...] = jnp.concatenate(ka_g, axis=1).astype(BF16)
    va_ref[...] = va.astype(BF16)
    qb_ref[...] = jnp.concatenate(qb_g, axis=1).astype(BF16)
    kb_ref[...] = jnp.concatenate(kb_g, axis=1).astype(BF16)
    vb_ref[...] = vb.astype(BF16)


def _row_tile(t, pref=512):
    tm = pref
    while t % tm:
        tm //= 2
    return tm


def _mod_index(rows_per_batch, tm, fixed):
    if fixed is not None:
        return lambda i: (fixed, 0, 0)
    return lambda i: ((i * tm) // rows_per_batch, 0, 0)


def _ab_in(x, g, sc, sh, w, tables, rows_per_batch, fixed_mod):
    t, d = x.shape
    tm = _row_tile(min(t, rows_per_batch))
    rope = tables is not None
    midx = _mod_index(rows_per_batch, tm, fixed_mod)
    full = lambda a: pl.BlockSpec(a.shape, lambda i: (0,) * a.ndim)
    in_specs = [
        pl.BlockSpec((tm, d), lambda i: (i, 0)),
        full(g),
        pl.BlockSpec((1, 1, d), midx),
        pl.BlockSpec((1, 1, d), midx),
    ] + [full(a) for a in w]
    args = [x, g, sc, sh] + list(w)
    if rope:
        nt = rows_per_batch // tm
        for tab in tables:
            in_specs.append(pl.BlockSpec((tm, LANES), lambda i: (i % nt, 0)))
            args.append(tab)
    widths = [N_HEADS_A * LANES, N_KV_A * LANES, N_KV_A * LANES,
              MLA_HEADS * LANES, MLA_HEADS * LANES, MLA_HEADS * LANES]
    return pl.pallas_call(
        functools.partial(_ab_in_kernel, rope=rope),
        out_shape=[jax.ShapeDtypeStruct((t, n), BF16) for n in widths],
        grid=(t // tm,),
        in_specs=in_specs,
        out_specs=[pl.BlockSpec((tm, n), lambda i: (i, 0)) for n in widths],
        compiler_params=_params("arbitrary"),
        name="ab_in_rope" if rope else "ab_in",
    )(*args)


def _c_in_kernel(*refs, rope):
    if rope:
        x_ref, g_ref, sc_ref, sh_ref, w_ref, qn_ref, kn_ref, cc_ref, sc2_ref, q_ref, k_ref, v_ref = refs
    else:
        x_ref, g_ref, sc_ref, sh_ref, w_ref, qn_ref, kn_ref, q_ref, k_ref, v_ref = refs
    h = _modnorm(x_ref[...], g_ref[...], sc_ref[0], sh_ref[0]).astype(BF16)
    p = _dot(h, w_ref[...])
    nq = N_HEADS_C * HEAD_DIM_C
    nk = N_KV_C * HEAD_DIM_C
    q_g = [_rms(t, qn_ref[...]) for t in _groups(p[:, :nq])]
    k_g = [_rms(t, kn_ref[...]) for t in _groups(p[:, nq:nq + nk])]
    scale = HEAD_DIM_C ** -0.5 * LOG2_E
    if rope:
        cc, ss = cc_ref[...], sc2_ref[...]
        q_g = [_rope_group(t, cc, ss, HEAD_DIM_C // 2, 0) for t in q_g]
        k_g = [_rope_group(t, cc, ss, HEAD_DIM_C // 2, 0) for t in k_g]
    q_ref[...] = jnp.concatenate([t * scale for t in q_g], axis=1).astype(BF16)
    k_ref[...] = jnp.concatenate(k_g, axis=1).astype(BF16)
    v_ref[...] = p[:, nq + nk:].astype(BF16)


def _c_in(x, g, sc, sh, w, qn, kn, tables, rows_per_batch, fixed_mod):
    t, d = x.shape
    tm = _row_tile(min(t, rows_per_batch))
    rope = tables is not None
    midx = _mod_index(rows_per_batch, tm, fixed_mod)
    full = lambda a: pl.BlockSpec(a.shape, lambda i: (0,) * a.ndim)
    in_specs = [
        pl.BlockSpec((tm, d), lambda i: (i, 0)),
        full(g),
        pl.BlockSpec((1, 1, d), midx),
        pl.BlockSpec((1, 1, d), midx),
        full(w), full(qn), full(kn),
    ]
    args = [x, g, sc, sh, w, qn, kn]
    if rope:
        nt = rows_per_batch // tm
        for tab in tables:
            in_specs.append(pl.BlockSpec((tm, LANES), lambda i: (i % nt, 0)))
            args.append(tab)
    widths = [N_HEADS_C * HEAD_DIM_C, N_KV_C * HEAD_DIM_C, N_KV_C * HEAD_DIM_C]
    return pl.pallas_call(
        functools.partial(_c_in_kernel, rope=rope),
        out_shape=[jax.ShapeDtypeStruct((t, n), BF16) for n in widths],
        grid=(t // tm,),
        in_specs=in_specs,
        out_specs=[pl.BlockSpec((tm, n), lambda i: (i, 0)) for n in widths],
        compiler_params=_params("arbitrary"),
        name="c_in_rope" if rope else "c_in",
    )(*args)


def _attn_kernel(*refs, n_kv, n_group, out_dim, has_lat, has_sink, window, tq, s_lat, kw):
    refs = list(refs)
    sink_ref = refs.pop(0) if has_sink else None
    q_ref = refs.pop(0)
    if has_lat:
        k1_ref, v1_ref = refs.pop(0), refs.pop(0)
    k2_ref, v2_ref, o_ref = refs
    hk0 = pl.program_id(1) * n_kv
    qi = pl.program_id(2)
    if window:
        ws = pl.multiple_of(jnp.clip(qi * tq - WINDOW, 0, s_lat - kw), LANES)
        rows = pl.ds(ws, kw)
        qpos = qi * tq + lax.broadcasted_iota(jnp.int32, (tq, kw), 0)
        kpos = ws + lax.broadcasted_iota(jnp.int32, (tq, kw), 1)
        ok = jnp.abs(kpos - qpos) <= WINDOW
    else:
        rows = slice(None)
    outs = []
    for j in range(n_kv):
        lanes = slice(j * LANES, (j + 1) * LANES)
        k2, v2 = k2_ref[:, lanes], v2_ref[:, lanes]
        if has_lat:
            k1, v1 = k1_ref[rows, lanes], v1_ref[rows, lanes]
        for g in range(n_group):
            h = j * n_group + g
            q = q_ref[:, h * LANES:(h + 1) * LANES]
            s2 = _dot_t(q, k2)
            m = jnp.max(s2, axis=-1, keepdims=True)
            if has_lat:
                s1 = _dot_t(q, k1)
                if window:
                    s1 = jnp.where(ok, s1, NEG_INF)
                m = jnp.maximum(m, jnp.max(s1, axis=-1, keepdims=True))
            if has_sink:
                sk = sink_ref[hk0 * n_group + h] * LOG2_E
                m = jnp.maximum(m, sk)
            e2 = jnp.exp2(s2 - m)
            den = jnp.sum(e2, axis=-1, keepdims=True)
            acc = _dot(e2.astype(BF16), v2)
            if has_lat:
                e1 = jnp.exp2(s1 - m)
                den = den + jnp.sum(e1, axis=-1, keepdims=True)
                acc = acc + _dot(e1.astype(BF16), v1)
            if has_sink:
                den = den + jnp.exp2(sk - m)
            outs.append(acc / den)
    if out_dim == LANES:
        for h, o in enumerate(outs):
            o_ref[:, h * LANES:(h + 1) * LANES] = o.astype(BF16)
    else:
        lane = lax.broadcasted_iota(jnp.int32, (tq, LANES), 1)
        for p in range(len(outs) // 2):
            pair = jnp.where(lane < out_dim, outs[2 * p], pltpu.roll(outs[2 * p + 1], out_dim, 1))
            o_ref[:, p * LANES:(p + 1) * LANES] = pair.astype(BF16)


def _attention(q, k_lat, v_lat, k_ctx, v_ctx, sink, *, batch, n_kv_total, n_kv, n_group, out_dim,
               window, tq):
    tq_total = q.shape[0] // batch
    tq = min(tq, tq_total)
    n_ctx = k_ctx.shape[0] // batch
    has_lat = k_lat is not None
    has_sink = sink is not None
    s_lat = k_lat.shape[0] // batch if has_lat else 0
    kw = min(tq + 2 * WINDOW, s_lat) if window else 0
    nq = tq_total // tq
    n_heads = n_kv * n_group
    assert out_dim == LANES or (2 * out_dim == LANES and n_heads % 2 == 0)
    in_specs, args = [], []
    if has_sink:
        in_specs.append(pl.BlockSpec(memory_space=pltpu.SMEM))
        args.append(sink)
    in_specs.append(pl.BlockSpec((tq, n_heads * LANES), lambda b, h, i: (b * nq + i, h)))
    args.append(q)
    if has_lat:
        in_specs += [pl.BlockSpec((s_lat, n_kv * LANES), lambda b, h, i: (b, h))] * 2
        args += [k_lat, v_lat]
    in_specs += [pl.BlockSpec((n_ctx, n_kv * LANES), lambda b, h, i: (b, h))] * 2
    args += [k_ctx, v_ctx]
    n_steps = n_kv_total // n_kv
    return pl.pallas_call(
        functools.partial(_attn_kernel, n_kv=n_kv, n_group=n_group, out_dim=out_dim, has_lat=has_lat,
                          has_sink=has_sink, window=window, tq=tq, s_lat=s_lat, kw=kw),
        out_shape=jax.ShapeDtypeStruct((q.shape[0], n_steps * n_heads * out_dim), BF16),
        grid=(batch, n_steps, nq),
        in_specs=in_specs,
        out_specs=pl.BlockSpec((tq, n_heads * out_dim), lambda b, h, i: (b * nq + i, h)),
        compiler_params=_params("arbitrary", "arbitrary", "arbitrary"),
        name="attn_win" if window else ("attn_dense" if has_lat else "attn_ctx"),
    )(*args)


def _out_kernel(*refs, n_parts, tm):
    o_refs = refs[:n_parts]
    w_refs = refs[n_parts:2 * n_parts]
    (x_ref, g1_ref, gn_ref, sc_ref, sh_ref, rw_ref, rb_ref,
     xo_ref, tok_ref, idx_ref, gate_ref) = refs[2 * n_parts:]
    y = _dot(o_refs[0][...], w_refs[0][...])
    for o_r, w_r in zip(o_refs[1:], w_refs[1:]):
        y = y + _dot(o_r[...], w_r[...])
    x = x_ref[...] + g1_ref[0] * y
    xo_ref[...] = x
    tok = _modnorm(x, gn_ref[...], sc_ref[0], sh_ref[0])
    for j in range(tok.shape[1] // LANES):
        tok_ref[pl.ds(j, tm, stride=SUBLANES), :] = tok[:, j * LANES:(j + 1) * LANES]
    tok_hi = tok.astype(BF16)
    tok_lo = (tok - tok_hi.astype(F32)).astype(BF16)
    hh_hl = _dot(tok_hi, rw_ref[...])
    logits = (hh_hl[:, :LANES] + hh_hl[:, LANES:] + _dot(tok_lo, rw_ref[:, :LANES])) + rb_ref[...]
    lane = lax.broadcasted_iota(jnp.int32, logits.shape, 1).astype(F32)
    vals, idxs = [], []
    l = logits
    for _ in range(TOP_K):
        m = jnp.max(l, axis=-1, keepdims=True)
        idx = jnp.min(jnp.where(l == m, lane, float(LANES)), axis=-1, keepdims=True)
        vals.append(m)
        idxs.append(idx)
        l = jnp.where(lane == idx, -jnp.inf, l)
    es = [jnp.exp(v - vals[0]) for v in vals]
    den = es[0]
    for e in es[1:]:
        den = den + e
    lane8 = lax.broadcasted_iota(jnp.int32, (tm, SUBLANES), 1)
    io = jnp.zeros((tm, SUBLANES), jnp.int32)
    go = jnp.zeros((tm, SUBLANES), F32)
    for k in range(TOP_K):
        io = jnp.where(lane8 == k, idxs[k].astype(jnp.int32), io)
        go = jnp.where(lane8 == k, es[k] / den, go)
    idx_ref[...] = io
    gate_ref[...] = go


def _out_proj(o_parts, w_parts, x, g1, gn, sc, sh, rw, rb, rows_per_batch, fixed_mod):
    t, d = x.shape
    tm = _row_tile(min(t, rows_per_batch), 512)
    midx = _mod_index(rows_per_batch, tm, fixed_mod)
    full = lambda a: pl.BlockSpec(a.shape, lambda i: (0,) * a.ndim)
    n_parts = len(o_parts)
    in_specs = ([pl.BlockSpec((tm, o.shape[1]), lambda i: (i, 0)) for o in o_parts]
                + [full(w) for w in w_parts]
                + [pl.BlockSpec((tm, d), lambda i: (i, 0)),
                   pl.BlockSpec((1, 1, d), midx), full(gn),
                   pl.BlockSpec((1, 1, d), midx), pl.BlockSpec((1, 1, d), midx),
                   full(rw), full(rb)])
    return pl.pallas_call(
        functools.partial(_out_kernel, n_parts=n_parts, tm=tm),
        out_shape=[jax.ShapeDtypeStruct((t, d), F32),
                   jax.ShapeDtypeStruct((t * SUBLANES, d // SUBLANES), F32),
                   jax.ShapeDtypeStruct((t, SUBLANES), jnp.int32),
                   jax.ShapeDtypeStruct((t, SUBLANES), F32)],
        grid=(t // tm,),
        in_specs=in_specs,
        out_specs=[pl.BlockSpec((tm, d), lambda i: (i, 0)),
                   pl.BlockSpec((tm * SUBLANES, d // SUBLANES), lambda i: (i, 0)),
                   pl.BlockSpec((tm, SUBLANES), lambda i: (i, 0)),
                   pl.BlockSpec((tm, SUBLANES), lambda i: (i, 0))],
        compiler_params=_params("arbitrary"),
        name="out_proj",
    )(*o_parts, *w_parts, x, g1, gn, sc, sh, rw, rb)


DISPATCH_TILE = 512
PAIR_TILE = 2 * LANES
WEIGHT_SPLIT = 2


def _row(ref, r):
    return ref.at[pl.ds(pl.multiple_of(r * SUBLANES, SUBLANES), SUBLANES)]


def _dispatch_kernel(*refs, tm, part_tiles, n_experts):
    n_parts = len(part_tiles)
    plo_ref, phi_ref, dest_ref = refs[:3]
    tok_refs = refs[3:3 + n_parts]
    p_ref, w1_ref, w2_ref, xs_ref, o1_ref, o2_ref, zero_ref, sem, zsem = refs[3 + n_parts:]
    i = pl.program_id(0)

    @pl.when(i == 0)
    def _():
        zero_ref[...] = jnp.zeros(zero_ref.shape, zero_ref.dtype)

        def fill_range(e, carry):
            lo, hi = plo_ref[e], phi_ref[e]

            def fill_chunk(c, carry):
                c_lo = lo + c * tm
                c_hi = jnp.minimum(c_lo + tm, hi)

                def start_row(r, carry):
                    pltpu.make_async_copy(zero_ref, _row(xs_ref, r), zsem).start()
                    return carry

                def wait_row(r, carry):
                    pltpu.make_async_copy(zero_ref, _row(xs_ref, r), zsem).wait()
                    return carry

                lax.fori_loop(c_lo, c_hi, start_row, carry)
                return lax.fori_loop(c_lo, c_hi, wait_row, carry)

            return lax.fori_loop(0, (hi - lo + tm - 1) // tm, fill_chunk, carry)

        lax.fori_loop(0, plo_ref.shape[0], fill_range, 0)

    def start_rows(tok_ref):
        def body(t, carry):
            for u in range(2):
                r = t * 2 + u
                src = _row(tok_ref, r)
                for k in range(TOP_K):
                    pltpu.make_async_copy(src, _row(xs_ref, dest_ref[0, 0, r * TOP_K + k]), sem).start(
                        priority=k % 2)
            return carry

        lax.fori_loop(0, tm // 2, body, 0)

    first = 0
    for tok_ref, n in zip(tok_refs, part_tiles):
        pl.when((i >= first) & (i < first + n))(functools.partial(start_rows, tok_ref))
        first += n

    @pl.when(i < n_experts * WEIGHT_SPLIT)
    def _():
        perm = p_ref[...]
        for j in range(w1_ref.shape[3] // PAIR_TILE):
            cols = slice(j * PAIR_TILE, (j + 1) * PAIR_TILE)
            o1_ref[0, :, cols] = _dot(w1_ref[0, 0, :, cols].astype(BF16), perm).astype(BF16)
        o2_ref[0] = w2_ref[0, 0].astype(BF16)

    @pl.when(i < sum(part_tiles))
    def _():
        done = xs_ref.at[pl.ds(0, tm * TOP_K * SUBLANES)]
        pltpu.make_async_copy(done, done, sem).wait()


def _dispatch(token_parts, dest, pad_lo, pad_hi, cap, w_in, w_out, layer):
    t = dest.shape[0]
    tm = _row_tile(min(p.shape[0] // SUBLANES for p in token_parts), DISPATCH_TILE)
    part_tiles = tuple(p.shape[0] // SUBLANES // tm for p in token_parts)
    n_tiles = sum(part_tiles)
    assert n_tiles * tm == t
    _, e, d, f2 = w_in.shape
    lane = jnp.arange(LANES)
    perm = jnp.zeros((PAIR_TILE, PAIR_TILE), F32).at[2 * lane, lane].set(1.0)
    perm = perm.at[2 * lane + 1, LANES + lane].set(1.0).astype(BF16)
    in_specs = [pl.BlockSpec((1, 1, tm * TOP_K), lambda i, lo, hi: (jnp.minimum(i, n_tiles - 1), 0, 0),
                             memory_space=pltpu.SMEM)]
    first = 0
    for n in part_tiles:
        in_specs.append(pl.BlockSpec(
            (tm * SUBLANES, LANES), lambda i, lo, hi, first=first, n=n: (jnp.clip(i - first, 0, n - 1), 0)))
        first += n
    ws = WEIGHT_SPLIT
    assert f2 % (ws * PAIR_TILE) == 0 and (f2 // 2) % (ws * SUBLANES * 2) == 0
    share = lambda i: jnp.minimum(i, e * ws - 1)
    in_specs += [pl.BlockSpec((PAIR_TILE, PAIR_TILE), lambda i, lo, hi: (0, 0)),
                 pl.BlockSpec((1, 1, d, f2 // ws), lambda i, lo, hi: (layer, share(i) // ws, 0, share(i) % ws)),
                 pl.BlockSpec((1, 1, f2 // 2 // ws, d),
                              lambda i, lo, hi: (layer, share(i) // ws, share(i) % ws, 0))]
    grid_spec = pltpu.PrefetchScalarGridSpec(
        num_scalar_prefetch=2,
        grid=(max(n_tiles, e * ws),),
        in_specs=in_specs,
        out_specs=[pl.BlockSpec(memory_space=pl.ANY),
                   pl.BlockSpec((1, d, f2 // ws), lambda i, lo, hi: (share(i) // ws, 0, share(i) % ws)),
                   pl.BlockSpec((1, f2 // 2 // ws, d), lambda i, lo, hi: (share(i) // ws, share(i) % ws, 0))],
        scratch_shapes=[pltpu.VMEM((SUBLANES, LANES), F32),
                        pltpu.SemaphoreType.DMA, pltpu.SemaphoreType.DMA],
    )
    return pl.pallas_call(
        functools.partial(_dispatch_kernel, tm=tm, part_tiles=part_tiles, n_experts=e),
        out_shape=[jax.ShapeDtypeStruct((cap * SUBLANES, LANES), F32),
                   jax.ShapeDtypeStruct(w_in.shape[1:], BF16), jax.ShapeDtypeStruct(w_out.shape[1:], BF16)],
        grid_spec=grid_spec,
        compiler_params=_params("arbitrary"),
        name="dispatch",
    )(pad_lo, pad_hi, dest.reshape(n_tiles, 1, tm * TOP_K), *token_parts, perm, w_in, w_out)


EXPERT_BLOCK = 512


def _expert_kernel(be_ref, nb_ref, x_ref, w1_ref, b1_ref, w2_ref, b2_ref, o_ref, *, bm):
    del be_ref
    b = pl.program_id(0)
    n_j = w1_ref.shape[1] // LANES

    @pl.when(b < nb_ref[0])
    def _():
        x = jnp.concatenate([x_ref[pl.ds(j, bm, stride=SUBLANES), :] for j in range(n_j)], axis=1)
        u = _dot(x.astype(BF16), w1_ref[0]) + b1_ref[0]
        acts = []
        for j in range(u.shape[1] // PAIR_TILE):
            glu = jnp.minimum(u[:, j * PAIR_TILE:j * PAIR_TILE + LANES], SWIGLU_LIMIT)
            lin = jnp.clip(u[:, j * PAIR_TILE + LANES:(j + 1) * PAIR_TILE], -SWIGLU_LIMIT, SWIGLU_LIMIT)
            acts.append(glu * jax.nn.sigmoid(SWIGLU_ALPHA * glu) * (lin + 1.0))
        a = jnp.concatenate(acts, axis=1)
        y = _dot(a.astype(BF16), w2_ref[0]) + b2_ref[0]
        for j in range(y.shape[1] // LANES):
            o_ref[pl.ds(j, bm, stride=SUBLANES), :] = y[:, j * LANES:(j + 1) * LANES]

    @pl.when(b >= nb_ref[0])
    def _():
        o_ref[...] = jnp.zeros(o_ref.shape, o_ref.dtype)


def _expert_mlp(xs, blk_expert, n_used, w1, b1, w2, b2):
    bm = EXPERT_BLOCK
    nblk = blk_expert.shape[0]
    _, d, f2 = w1.shape
    grid_spec = pltpu.PrefetchScalarGridSpec(
        num_scalar_prefetch=2,
        grid=(nblk,),
        in_specs=[
            pl.BlockSpec((bm * SUBLANES, LANES), lambda b, be, nb: (b, 0)),
            pl.BlockSpec((1, d, f2), lambda b, be, nb: (be[b], 0, 0)),
            pl.BlockSpec((1, 1, f2), lambda b, be, nb: (be[b], 0, 0)),
            pl.BlockSpec((1, f2 // 2, d), lambda b, be, nb: (be[b], 0, 0)),
            pl.BlockSpec((1, 1, d), lambda b, be, nb: (be[b], 0, 0)),
        ],
        out_specs=pl.BlockSpec((bm * SUBLANES, LANES), lambda b, be, nb: (b, 0)),
    )
    return pl.pallas_call(
        functools.partial(_expert_kernel, bm=bm),
        out_shape=jax.ShapeDtypeStruct(xs.shape, F32),
        grid_spec=grid_spec,
        compiler_params=_params("arbitrary"),
        name="expert_mlp",
    )(blk_expert, n_used, xs, w1, b1, w2, b2)


def _combine_kernel(*refs, tm, final):
    if final:
        dest_ref, next_ref, x_ref, ys_ref, gate_ref, g2_ref, fg_ref, o_ref, yb0, yb1, sems = refs
    else:
        dest_ref, next_ref, x_ref, ys_ref, gate_ref, g2_ref, o_ref, yb0, yb1, sems = refs
    i = pl.program_id(0)
    ybufs = (yb0, yb1)

    def start_gather(d_ref, slot):
        def body(t, carry):
            for u in range(2):
                r = t * 2 + u
                for k in range(TOP_K):
                    pltpu.make_async_copy(_row(ys_ref, d_ref[0, 0, r * TOP_K + k]),
                                          _row(ybufs[slot], k * tm + r), sems.at[slot]).start(priority=k % 2)
            return carry

        lax.fori_loop(0, tm // 2, body, 0)

    def finish(slot):
        ybuf = ybufs[slot]
        pltpu.make_async_copy(ybuf, ybuf, sems.at[slot]).wait()
        gate = gate_ref[...]
        parts = []
        for j in range(x_ref.shape[1] // LANES):
            f = gate[:, 0:1] * ybuf[pl.ds(j, tm, stride=SUBLANES), :]
            for k in range(1, TOP_K):
                f = f + gate[:, k:k + 1] * ybuf[pl.ds(k * tm * SUBLANES + j, tm, stride=SUBLANES), :]
            parts.append(f)
        x = x_ref[...] + g2_ref[0] * jnp.concatenate(parts, axis=1)
        if final:
            x = _rms(x, fg_ref[...])
        o_ref[...] = x

    @pl.when(i == 0)
    def _():
        start_gather(dest_ref, 0)

    for slot in range(2):
        @pl.when(i % 2 == slot)
        def _(slot=slot):
            @pl.when(i + 1 < pl.num_programs(0))
            def _():
                start_gather(next_ref, 1 - slot)

            finish(slot)


def _combine(x, ys, dest, gates, g2, final_g, *, row_off, rows_per_batch, fixed_mod):
    t, d = x.shape
    tm = _row_tile(min(t, rows_per_batch), 256)
    midx = _mod_index(rows_per_batch, tm, fixed_mod)
    final = final_g is not None
    off = row_off // tm
    t_all = dest.shape[0]
    n_steps = t // tm
    dest_tiles = dest.reshape(t_all // tm, 1, tm * TOP_K)
    in_specs = [pl.BlockSpec((1, 1, tm * TOP_K), lambda i: (off + i, 0, 0), memory_space=pltpu.SMEM),
                pl.BlockSpec((1, 1, tm * TOP_K), lambda i: (off + jnp.minimum(i + 1, n_steps - 1), 0, 0),
                             memory_space=pltpu.SMEM),
                pl.BlockSpec((tm, d), lambda i: (i, 0)),
                pl.BlockSpec(memory_space=pl.ANY),
                pl.BlockSpec((tm, SUBLANES), lambda i: (off + i, 0)),
                pl.BlockSpec((1, 1, d), midx)]
    args = [dest_tiles, dest_tiles, x, ys, gates, g2]
    if final:
        in_specs.append(pl.BlockSpec(final_g.shape, lambda i: (0, 0)))
        args.append(final_g)
    return pl.pallas_call(
        functools.partial(_combine_kernel, tm=tm, final=final),
        out_shape=jax.ShapeDtypeStruct((t, d), F32),
        grid=(n_steps,),
        in_specs=in_specs,
        out_specs=pl.BlockSpec((tm, d), lambda i: (i, 0)),
        scratch_shapes=[pltpu.VMEM((TOP_K * tm * SUBLANES, LANES), F32),
                        pltpu.VMEM((TOP_K * tm * SUBLANES, LANES), F32),
                        pltpu.SemaphoreType.DMA((2,))],
        compiler_params=_params("arbitrary"),
        name="combine_final" if final else "combine",
    )(*args)


def _pad_heads(w, n_heads, axis):
    shape = list(w.shape)
    hd = shape[axis] // n_heads
    w = w.reshape(shape[:axis] + [n_heads, hd] + shape[axis + 1:])
    pad = [(0, 0)] * w.ndim
    pad[axis + 1] = (0, LANES - hd)
    w = jnp.pad(w, pad)
    shape[axis] = n_heads * LANES
    return w.reshape(shape)


def _rope_tables(s, rot_dim, lo):
    pos = jnp.arange(s, dtype=jnp.int32)
    rows, cols = (pos // GRID_W).astype(F32), (pos % GRID_W).astype(F32)
    quarter = rot_dim // 4
    inv = ROPE_THETA ** (-jnp.arange(quarter, dtype=F32) / quarter)
    ang = jnp.concatenate([rows[:, None] * inv, cols[:, None] * inv], axis=-1)
    cos, sin = jnp.cos(ang), jnp.sin(ang)
    hi = LANES - lo - rot_dim
    cos_t = jnp.concatenate([jnp.ones((s, lo), F32), cos, cos, jnp.ones((s, hi), F32)], axis=1)
    sin_t = jnp.concatenate([jnp.zeros((s, lo), F32), -sin, sin, jnp.zeros((s, hi), F32)], axis=1)
    return cos_t, sin_t


def _moe_plan(eidx, bm):
    t = eidx.shape[0]
    ids = jnp.arange(N_EXPERTS, dtype=jnp.int32)
    sel = eidx[:, :, None] == ids
    hit = sel.astype(jnp.int32).sum(1)
    chunk = _row_tile(t, 256)
    tri = jnp.tril(jnp.ones((chunk, chunk), F32))
    within = jnp.einsum("ij,cjk->cik", tri, hit.astype(F32).reshape(t // chunk, chunk, N_EXPERTS),
                        precision=HIGHEST)
    totals = within[:, -1, :]
    incl = (within + (jnp.cumsum(totals, axis=0) - totals)[:, None, :]).reshape(t, N_EXPERTS)
    incl = incl.astype(jnp.int32)
    counts = incl[-1]
    rank = jnp.where(sel, (incl - hit)[:, None, :], 0).sum(-1)
    padded = (counts + bm - 1) // bm * bm
    pends = jnp.cumsum(padded)
    pstarts = pends - padded
    dest = jnp.where(sel, pstarts, 0).sum(-1) + rank
    n_assign = t * TOP_K
    nblk = -(-(n_assign + N_EXPERTS * (bm - 1)) // bm)
    blk_start = jnp.arange(nblk, dtype=jnp.int32) * bm
    blk_expert = jnp.minimum((blk_start[:, None] >= pends[None, :]).sum(1), N_EXPERTS - 1)
    n_used = (pends[-1] // bm).astype(jnp.int32).reshape(1)
    pad_lo = jnp.concatenate([pstarts + counts, pends[-1:]]).astype(jnp.int32)
    pad_hi = jnp.concatenate([pends, jnp.full((1,), nblk * bm, pends.dtype)]).astype(jnp.int32)
    return dest.astype(jnp.int32), blk_expert.astype(jnp.int32), n_used, pad_lo, pad_hi


def _moe(token_parts, eidx, w_in, b1, w_out, b2, layer):
    dest, blk_expert, n_used, pad_lo, pad_hi = _moe_plan(eidx, EXPERT_BLOCK)
    xs, w1, w2 = _dispatch(token_parts, dest, pad_lo, pad_hi, blk_expert.shape[0] * EXPERT_BLOCK,
                           w_in, w_out, layer)
    return _expert_mlp(xs, blk_expert, n_used, w1, b1, w2, b2), dest


def kernel(x, c, ctx, c_ctx, ada_w, ada_b, norm_mix_g, norm_ffn_g, ab_w_in, mla_q_norm_g, mla_wq_b,
           mla_kv_norm_g, mla_wkv_b, swa_sink, ab_w_out, c_w_in, c_q_norm_g, c_k_norm_g, c_w_out,
           router_w, router_b, moe_w_in, moe_b_in, moe_w_out, moe_b_out, final_norm_g):
    bn, s, d = x.shape
    n_ctx = ctx.shape[1]
    depth = ada_w.shape[0]
    t_lat, t_ctx = bn * s, bn * n_ctx
    row = lambda v: v.reshape(1, -1)

    n_mod = -(-(bn + 1) // SUBLANES) * SUBLANES
    c_all = jnp.concatenate([c, c_ctx[None], jnp.zeros((n_mod - bn - 1, d), F32)], axis=0)
    mod = _modulation(c_all, ada_w, ada_b).reshape(depth, n_mod, 6, 1, d)
    ctx_row = bn

    tab_a = _rope_tables(s, HEAD_DIM_A, 0)
    tab_b = _rope_tables(s, MLA_ROPE, MLA_NOPE)
    tab_c = _rope_tables(s, HEAD_DIM_C, 0)

    xl = x.reshape(t_lat, d)
    xc = ctx.reshape(t_ctx, d)
    for i in range(depth):
        with_ctx = i < depth - 1
        j = i // 2
        sh1, sc1, g1, sh2, sc2, g2 = [mod[i, :, m] for m in range(6)]
        gmix, gffn = row(norm_mix_g[i]), row(norm_ffn_g[i])
        if i % 2 == 0:
            w_in = ab_w_in[j]
            sizes = [N_HEADS_A * HEAD_DIM_A, N_KV_A * HEAD_DIM_A, N_KV_A * HEAD_DIM_A,
                     MLA_Q_RANK, MLA_KV_RANK, MLA_ROPE]
            offs = [0]
            for n in sizes:
                offs.append(offs[-1] + n)
            cols = [w_in[:, offs[m]:offs[m + 1]] for m in range(6)]
            krg = jnp.pad(cols[5], ((0, 0), (MLA_NOPE, LANES - MLA_NOPE - MLA_ROPE)))
            w1 = jnp.concatenate([_pad_heads(cols[0], N_HEADS_A, 1), _pad_heads(cols[1], N_KV_A, 1),
                                  _pad_heads(cols[2], N_KV_A, 1), cols[3], cols[4], krg], axis=1).astype(BF16)
            wq = _pad_heads(mla_wq_b[j], MLA_HEADS, 1).astype(BF16)
            wkv = mla_wkv_b[j].reshape(MLA_KV_RANK, MLA_HEADS, MLA_NOPE + MLA_V)
            wk = _pad_heads(wkv[:, :, :MLA_NOPE].reshape(MLA_KV_RANK, -1), MLA_HEADS, 1).astype(BF16)
            wv = _pad_heads(wkv[:, :, MLA_NOPE:].reshape(MLA_KV_RANK, -1), MLA_HEADS, 1).astype(BF16)
            ws = [w1, row(mla_q_norm_g[j]), wq, row(mla_kv_norm_g[j]), wk, wv]
            na = N_HEADS_A * HEAD_DIM_A
            wo_a = ab_w_out[j][:na].astype(BF16)
            wo_b = ab_w_out[j][na:].astype(BF16)
            sink = swa_sink[j]

            qa, ka, va, qb, kb, vb = _ab_in(xl, gmix, sc1, sh1, ws, tab_a + tab_b, s, None)
            qa_c, ka_c, va_c, qb_c, kb_c, vb_c = _ab_in(xc, gmix, sc1, sh1, ws, None, n_ctx, ctx_row)
            ga = N_HEADS_A // N_KV_A
            cfg_a = dict(batch=bn, n_kv_total=N_KV_A, n_kv=1, n_group=ga, out_dim=HEAD_DIM_A)
            cfg_b = dict(batch=bn, n_kv_total=MLA_HEADS, n_kv=MLA_STEP_HEADS, n_group=1, out_dim=MLA_V)
            oa = _attention(qa, ka, va, ka_c, va_c, sink, window=True, tq=512, **cfg_a)
            ob = _attention(qb, kb, vb, kb_c, vb_c, None, window=False, tq=1024, **cfg_b)
            o_lat, w_o = [oa, ob], [wo_a, wo_b]
            if with_ctx:
                oa_c = _attention(qa_c, None, None, ka_c, va_c, sink, window=False, tq=256, **cfg_a)
                ob_c = _attention(qb_c, None, None, kb_c, vb_c, None, window=False, tq=256, **cfg_b)
                o_ctx = [oa_c, ob_c]
        else:
            w_in = c_w_in[j].astype(BF16)
            qn, kn = row(c_q_norm_g[j]), row(c_k_norm_g[j])
            q, k, v = _c_in(xl, gmix, sc1, sh1, w_in, qn, kn, tab_c, s, None)
            q_c, k_c, v_c = _c_in(xc, gmix, sc1, sh1, w_in, qn, kn, None, n_ctx, ctx_row)
            gc = N_HEADS_C // N_KV_C
            cfg_c = dict(batch=bn, n_kv_total=N_KV_C, n_kv=1, n_group=gc, out_dim=HEAD_DIM_C,
                         window=False, tq=1024)
            o = _attention(q, k, v, k_c, v_c, None, **cfg_c)
            o_lat, w_o = [o], [c_w_out[j].astype(BF16)]
            if with_ctx:
                o_ctx = [_attention(q_c, None, None, k_c, v_c, None, **cfg_c)]

        rw = jnp.pad(router_w[i], ((0, 0), (0, LANES - N_EXPERTS)))
        rw_hi = rw.astype(BF16)
        rw = jnp.concatenate([rw_hi, (rw - rw_hi.astype(F32)).astype(BF16)], axis=1)
        rb = jnp.concatenate([router_b[i], jnp.full((LANES - N_EXPERTS,), NEG_INF, F32)]).reshape(1, LANES)
        xl, tok_l, idx_l, gate_l = _out_proj(o_lat, w_o, xl, g1, gffn, sc2, sh2, rw, rb, s, None)
        if with_ctx:
            xc, tok_c, idx_c, gate_c = _out_proj(o_ctx, w_o, xc, g1, gffn, sc2, sh2, rw, rb, n_ctx, ctx_row)
            tokens = [tok_l, tok_c]
            eidx = jnp.concatenate([idx_l, idx_c], axis=0)
            gates = jnp.concatenate([gate_l, gate_c], axis=0)
        else:
            tokens, eidx, gates = [tok_l], idx_l, gate_l

        n_tiles = moe_b_in.shape[-1] // PAIR_TILE
        b1e = moe_b_in[i].reshape(N_EXPERTS, n_tiles, LANES, 2).transpose(0, 1, 3, 2)
        b1e = b1e.reshape(N_EXPERTS, 1, -1)
        b2e = moe_b_out[i][:, None, :]
        ys, dest = _moe(tokens, eidx[:, :TOP_K], moe_w_in, b1e, moe_w_out, b2e, i)

        last = i == depth - 1
        xl = _combine(xl, ys, dest, gates, g2, row(final_norm_g) if last else None,
                      row_off=0, rows_per_batch=s, fixed_mod=None)
        if with_ctx:
            xc = _combine(xc, ys, dest, gates, g2, None,
                          row_off=t_lat, rows_per_batch=n_ctx, fixed_mod=ctx_row)
    return xl.reshape(bn, s, d)
```

```python
import functools

import jax
import jax.numpy as jnp
from jax import lax
from jax.experimental import pallas as pl
from jax.experimental.pallas import tpu as pltpu

GRID_W = 64
N_HEADS_A, N_KV_A, HEAD_DIM_A, WINDOW = 8, 2, 64, 128
MLA_HEADS, MLA_Q_RANK, MLA_KV_RANK, MLA_NOPE, MLA_ROPE, MLA_V = 8, 384, 256, 64, 32, 64
N_HEADS_C, N_KV_C, HEAD_DIM_C = 8, 2, 128
N_EXPERTS, TOP_K = 32, 4
MLA_STEP_HEADS = 4
SWIGLU_ALPHA, SWIGLU_LIMIT = 1.702, 7.0
ROPE_THETA, RMS_EPS, NEG_INF = 10000.0, 1e-6, -1e30
LOG2_E = 1.4426950408889634

LANES = 128
SUBLANES = 8
VMEM_LIMIT_BYTES = 56 * 1024 * 1024

F32 = jnp.float32
BF16 = jnp.bfloat16
HIGHEST = lax.Precision.HIGHEST


def _params(*sem):
    return pltpu.CompilerParams(dimension_semantics=sem, vmem_limit_bytes=VMEM_LIMIT_BYTES)


def _dot(a, b):
    return jnp.dot(a, b, preferred_element_type=F32)


def _dot_t(a, b):
    return lax.dot_general(a, b, (((1,), (1,)), ((), ())), preferred_element_type=F32)


def _rms(x, g):
    return x * lax.rsqrt(jnp.mean(x * x, axis=-1, keepdims=True) + RMS_EPS) * g


def _modnorm(x, g, sc, sh):
    return _rms(x, g) * (1.0 + sc) + sh


def _rope_group(x, cos, sin, half, lo):
    lane = lax.broadcasted_iota(jnp.int32, x.shape, 1)
    first = (lane >= lo) & (lane < lo + half)
    rot = jnp.where(first, pltpu.roll(x, LANES - half, 1), pltpu.roll(x, half, 1))
    return x * cos + rot * sin


def _groups(x):
    return [x[:, i * LANES:(i + 1) * LANES] for i in range(x.shape[1] // LANES)]


def _mod_kernel(c_ref, w_ref, b_ref, o_ref):
    c = c_ref[...]
    a = c * jax.nn.sigmoid(c)
    o_ref[0] = jnp.dot(a, w_ref[0], preferred_element_type=F32, precision=HIGHEST) + b_ref[0]


def _modulation(c_all, ada_w, ada_b):
    depth, d, n = ada_w.shape
    r = c_all.shape[0]
    nt = n // 4
    return pl.pallas_call(
        _mod_kernel,
        out_shape=jax.ShapeDtypeStruct((depth, r, n), F32),
        grid=(depth, n // nt),
        in_specs=[
            pl.BlockSpec((r, d), lambda i, j: (0, 0)),
            pl.BlockSpec((1, d, nt), lambda i, j: (i, 0, j)),
            pl.BlockSpec((1, 1, nt), lambda i, j: (i, 0, j)),
        ],
        out_specs=pl.BlockSpec((1, r, nt), lambda i, j: (i, 0, j)),
        compiler_params=_params("arbitrary", "arbitrary"),
        name="modulation",
    )(c_all, ada_w, ada_b.reshape(depth, 1, n))


def _ab_in_kernel(*refs, rope):
    if rope:
        (x_ref, g_ref, sc_ref, sh_ref, w1_ref, qg_ref, wq_ref, kvg_ref, wk_ref, wv_ref,
         ca_ref, sa_ref, cb_ref, sb_ref, qa_ref, ka_ref, va_ref, qb_ref, kb_ref, vb_ref) = refs
    else:
        (x_ref, g_ref, sc_ref, sh_ref, w1_ref, qg_ref, wq_ref, kvg_ref, wk_ref, wv_ref,
         qa_ref, ka_ref, va_ref, qb_ref, kb_ref, vb_ref) = refs
    h = _modnorm(x_ref[...], g_ref[...], sc_ref[0], sh_ref[0]).astype(BF16)
    p = _dot(h, w1_ref[...])
    nqa = N_HEADS_A * LANES
    nka = N_KV_A * LANES
    o = 0
    qa = p[:, o:o + nqa]; o += nqa
    ka = p[:, o:o + nka]; o += nka
    va = p[:, o:o + nka]; o += nka
    cq = p[:, o:o + MLA_Q_RANK]; o += MLA_Q_RANK
    ckv = p[:, o:o + MLA_KV_RANK]; o += MLA_KV_RANK
    krg = p[:, o:o + LANES]

    qb = _dot(_rms(cq, qg_ref[...]).astype(BF16), wq_ref[...])
    ckv_n = _rms(ckv, kvg_ref[...]).astype(BF16)
    kb = _dot(ckv_n, wk_ref[...])
    vb = _dot(ckv_n, wv_ref[...])

    scale_a = HEAD_DIM_A ** -0.5 * LOG2_E
    scale_b = (MLA_NOPE + MLA_ROPE) ** -0.5 * LOG2_E
    if rope:
        ca, sa, cb, sb = ca_ref[...], sa_ref[...], cb_ref[...], sb_ref[...]
        half_a, half_b = HEAD_DIM_A // 2, MLA_ROPE // 2
        qa_g = [_rope_group(t, ca, sa, half_a, 0) * scale_a for t in _groups(qa)]
        ka_g = [_rope_group(t, ca, sa, half_a, 0) for t in _groups(ka)]
        qb_g = [_rope_group(t, cb, sb, half_b, MLA_NOPE) * scale_b for t in _groups(qb)]
        krg = _rope_group(krg, cb, sb, half_b, MLA_NOPE)
    else:
        qa_g = [t * scale_a for t in _groups(qa)]
        ka_g = _groups(ka)
        qb_g = [t * scale_b for t in _groups(qb)]
    kb_g = [t + krg for t in _groups(kb)]
    qa_ref[...] = jnp.concatenate(qa_g, axis=1).astype(BF16)
    ka_ref[...] = jnp.concatenate(ka_g, axis=1).astype(BF16)
    va_ref[...] = va.astype(BF16)
    qb_ref[...] = jnp.concatenate(qb_g, axis=1).astype(BF16)
    kb_ref[...] = jnp.concatenate(kb_g, axis=1).astype(BF16)
    vb_ref[...] = vb.astype(BF16)


def _row_tile(t, pref=512):
    tm = pref
    while t % tm:
        tm //= 2
    return tm


def _mod_index(rows_per_batch, tm, fixed):
    if fixed is not None:
        return lambda i: (fixed, 0, 0)
    return lambda i: ((i * tm) // rows_per_batch, 0, 0)


def _ab_in(x, g, sc, sh, w, tables, rows_per_batch, fixed_mod):
    t, d = x.shape
    tm = _row_tile(min(t, rows_per_batch))
    rope = tables is not None
    midx = _mod_index(rows_per_batch, tm, fixed_mod)
    full = lambda a: pl.BlockSpec(a.shape, lambda i: (0,) * a.ndim)
    in_specs = [
        pl.BlockSpec((tm, d), lambda i: (i, 0)),
        full(g),
        pl.BlockSpec((1, 1, d), midx),
        pl.BlockSpec((1, 1, d), midx),
    ] + [full(a) for a in w]
    args = [x, g, sc, sh] + list(w)
    if rope:
        nt = rows_per_batch // tm
        for tab in tables:
            in_specs.append(pl.BlockSpec((tm, LANES), lambda i: (i % nt, 0)))
            args.append(tab)
    widths = [N_HEADS_A * LANES, N_KV_A * LANES, N_KV_A * LANES,
              MLA_HEADS * LANES, MLA_HEADS * LANES, MLA_HEADS * LANES]
    return pl.pallas_call(
        functools.partial(_ab_in_kernel, rope=rope),
        out_shape=[jax.ShapeDtypeStruct((t, n), BF16) for n in widths],
        grid=(t // tm,),
        in_specs=in_specs,
        out_specs=[pl.BlockSpec((tm, n), lambda i: (i, 0)) for n in widths],
        compiler_params=_params("arbitrary"),
        name="ab_in_rope" if rope else "ab_in",
    )(*args)


def _c_in_kernel(*refs, rope):
    if rope:
        x_ref, g_ref, sc_ref, sh_ref, w_ref, qn_ref, kn_ref, cc_ref, sc2_ref, q_ref, k_ref, v_ref = refs
    else:
        x_ref, g_ref, sc_ref, sh_ref, w_ref, qn_ref, kn_ref, q_ref, k_ref, v_ref = refs
    h = _modnorm(x_ref[...], g_ref[...], sc_ref[0], sh_ref[0]).astype(BF16)
    p = _dot(h, w_ref[...])
    nq = N_HEADS_C * HEAD_DIM_C
    nk = N_KV_C * HEAD_DIM_C
    q_g = [_rms(t, qn_ref[...]) for t in _groups(p[:, :nq])]
    k_g = [_rms(t, kn_ref[...]) for t in _groups(p[:, nq:nq + nk])]
    scale = HEAD_DIM_C ** -0.5 * LOG2_E
    if rope:
        cc, ss = cc_ref[...], sc2_ref[...]
        q_g = [_rope_group(t, cc, ss, HEAD_DIM_C // 2, 0) for t in q_g]
        k_g = [_rope_group(t, cc, ss, HEAD_DIM_C // 2, 0) for t in k_g]
    q_ref[...] = jnp.concatenate([t * scale for t in q_g], axis=1).astype(BF16)
    k_ref[...] = jnp.concatenate(k_g, axis=1).astype(BF16)
    v_ref[...] = p[:, nq + nk:].astype(BF16)


def _c_in(x, g, sc, sh, w, qn, kn, tables, rows_per_batch, fixed_mod):
    t, d = x.shape
    tm = _row_tile(min(t, rows_per_batch))
    rope = tables is not None
    midx = _mod_index(rows_per_batch, tm, fixed_mod)
    full = lambda a: pl.BlockSpec(a.shape, lambda i: (0,) * a.ndim)
    in_specs = [
        pl.BlockSpec((tm, d), lambda i: (i, 0)),
        full(g),
        pl.BlockSpec((1, 1, d), midx),
        pl.BlockSpec((1, 1, d), midx),
        full(w), full(qn), full(kn),
    ]
    args = [x, g, sc, sh, w, qn, kn]
    if rope:
        nt = rows_per_batch // tm
        for tab in tables:
            in_specs.append(pl.BlockSpec((tm, LANES), lambda i: (i % nt, 0)))
            args.append(tab)
    widths = [N_HEADS_C * HEAD_DIM_C, N_KV_C * HEAD_DIM_C, N_KV_C * HEAD_DIM_C]
    return pl.pallas_call(
        functools.partial(_c_in_kernel, rope=rope),
        out_shape=[jax.ShapeDtypeStruct((t, n), BF16) for n in widths],
        grid=(t // tm,),
        in_specs=in_specs,
        out_specs=[pl.BlockSpec((tm, n), lambda i: (i, 0)) for n in widths],
        compiler_params=_params("arbitrary"),
        name="c_in_rope" if rope else "c_in",
    )(*args)


def _attn_kernel(*refs, n_kv, n_group, out_dim, has_lat, has_sink, window, tq, s_lat, kw):
    refs = list(refs)
    sink_ref = refs.pop(0) if has_sink else None
    q_ref = refs.pop(0)
    if has_lat:
        k1_ref, v1_ref = refs.pop(0), refs.pop(0)
    k2_ref, v2_ref, o_ref = refs
    hk0 = pl.program_id(1) * n_kv
    qi = pl.program_id(2)
    if window:
        ws = pl.multiple_of(jnp.clip(qi * tq - WINDOW, 0, s_lat - kw), LANES)
        rows = pl.ds(ws, kw)
        qpos = qi * tq + lax.broadcasted_iota(jnp.int32, (tq, kw), 0)
        kpos = ws + lax.broadcasted_iota(jnp.int32, (tq, kw), 1)
        ok = jnp.abs(kpos - qpos) <= WINDOW
    else:
        rows = slice(None)
    outs = []
    for j in range(n_kv):
        lanes = slice(j * LANES, (j + 1) * LANES)
        k2, v2 = k2_ref[:, lanes], v2_ref[:, lanes]
        if has_lat:
            k1, v1 = k1_ref[rows, lanes], v1_ref[rows, lanes]
        for g in range(n_group):
            h = j * n_group + g
            q = q_ref[:, h * LANES:(h + 1) * LANES]
            s2 = _dot_t(q, k2)
            m = jnp.max(s2, axis=-1, keepdims=True)
            if has_lat:
                s1 = _dot_t(q, k1)
                if window:
                    s1 = jnp.where(ok, s1, NEG_INF)
                m = jnp.maximum(m, jnp.max(s1, axis=-1, keepdims=True))
            if has_sink:
                sk = sink_ref[hk0 * n_group + h] * LOG2_E
                m = jnp.maximum(m, sk)
            e2 = jnp.exp2(s2 - m)
            den = jnp.sum(e2, axis=-1, keepdims=True)
            acc = _dot(e2.astype(BF16), v2)
            if has_lat:
                e1 = jnp.exp2(s1 - m)
                den = den + jnp.sum(e1, axis=-1, keepdims=True)
                acc = acc + _dot(e1.astype(BF16), v1)
            if has_sink:
                den = den + jnp.exp2(sk - m)
            outs.append(acc / den)
    if out_dim == LANES:
        for h, o in enumerate(outs):
            o_ref[:, h * LANES:(h + 1) * LANES] = o.astype(BF16)
    else:
        lane = lax.broadcasted_iota(jnp.int32, (tq, LANES), 1)
        for p in range(len(outs) // 2):
            pair = jnp.where(lane < out_dim, outs[2 * p], pltpu.roll(outs[2 * p + 1], out_dim, 1))
            o_ref[:, p * LANES:(p + 1) * LANES] = pair.astype(BF16)


def _attention(q, k_lat, v_lat, k_ctx, v_ctx, sink, *, batch, n_kv_total, n_kv, n_group, out_dim,
               window, tq):
    tq_total = q.shape[0] // batch
    tq = min(tq, tq_total)
    n_ctx = k_ctx.shape[0] // batch
    has_lat = k_lat is not None
    has_sink = sink is not None
    s_lat = k_lat.shape[0] // batch if has_lat else 0
    kw = min(tq + 2 * WINDOW, s_lat) if window else 0
    nq = tq_total // tq
    n_heads = n_kv * n_group
    assert out_dim == LANES or (2 * out_dim == LANES and n_heads % 2 == 0)
    in_specs, args = [], []
    if has_sink:
        in_specs.append(pl.BlockSpec(memory_space=pltpu.SMEM))
        args.append(sink)
    in_specs.append(pl.BlockSpec((tq, n_heads * LANES), lambda b, h, i: (b * nq + i, h)))
    args.append(q)
    if has_lat:
        in_specs += [pl.BlockSpec((s_lat, n_kv * LANES), lambda b, h, i: (b, h))] * 2
        args += [k_lat, v_lat]
    in_specs += [pl.BlockSpec((n_ctx, n_kv * LANES), lambda b, h, i: (b, h))] * 2
    args += [k_ctx, v_ctx]
    n_steps = n_kv_total // n_kv
    return pl.pallas_call(
        functools.partial(_attn_kernel, n_kv=n_kv, n_group=n_group, out_dim=out_dim, has_lat=has_lat,
                          has_sink=has_sink, window=window, tq=tq, s_lat=s_lat, kw=kw),
        out_shape=jax.ShapeDtypeStruct((q.shape[0], n_steps * n_heads * out_dim), BF16),
        grid=(batch, n_steps, nq),
        in_specs=in_specs,
        out_specs=pl.BlockSpec((tq, n_heads * out_dim), lambda b, h, i: (b * nq + i, h)),
        compiler_params=_params("arbitrary", "arbitrary", "arbitrary"),
        name="attn_win" if window else ("attn_dense" if has_lat else "attn_ctx"),
    )(*args)


def _out_kernel(*refs, n_parts, tm):
    o_refs = refs[:n_parts]
    w_refs = refs[n_parts:2 * n_parts]
    (x_ref, g1_ref, gn_ref, sc_ref, sh_ref, rw_ref, rb_ref,
     xo_ref, tok_ref, idx_ref, gate_ref) = refs[2 * n_parts:]
    y = _dot(o_refs[0][...], w_refs[0][...])
    for o_r, w_r in zip(o_refs[1:], w_refs[1:]):
        y = y + _dot(o_r[...], w_r[...])
    x = x_ref[...] + g1_ref[0] * y
    xo_ref[...] = x
    tok = _modnorm(x, gn_ref[...], sc_ref[0], sh_ref[0])
    for j in range(tok.shape[1] // LANES):
        tok_ref[pl.ds(j, tm, stride=SUBLANES), :] = tok[:, j * LANES:(j + 1) * LANES]
    tok_hi = tok.astype(BF16)
    tok_lo = (tok - tok_hi.astype(F32)).astype(BF16)
    hh_hl = _dot(tok_hi, rw_ref[...])
    logits = (hh_hl[:, :LANES] + hh_hl[:, LANES:] + _dot(tok_lo, rw_ref[:, :LANES])) + rb_ref[...]
    lane = lax.broadcasted_iota(jnp.int32, logits.shape, 1).astype(F32)
    vals, idxs = [], []
    l = logits
    for _ in range(TOP_K):
        m = jnp.max(l, axis=-1, keepdims=True)
        idx = jnp.min(jnp.where(l == m, lane, float(LANES)), axis=-1, keepdims=True)
        vals.append(m)
        idxs.append(idx)
        l = jnp.where(lane == idx, -jnp.inf, l)
    es = [jnp.exp(v - vals[0]) for v in vals]
    den = es[0]
    for e in es[1:]:
        den = den + e
    lane8 = lax.broadcasted_iota(jnp.int32, (tm, SUBLANES), 1)
    io = jnp.zeros((tm, SUBLANES), jnp.int32)
    go = jnp.zeros((tm, SUBLANES), F32)
    for k in range(TOP_K):
        io = jnp.where(lane8 == k, idxs[k].astype(jnp.int32), io)
        go = jnp.where(lane8 == k, es[k] / den, go)
    idx_ref[...] = io
    gate_ref[...] = go


def _out_proj(o_parts, w_parts, x, g1, gn, sc, sh, rw, rb, rows_per_batch, fixed_mod):
    t, d = x.shape
    tm = _row_tile(min(t, rows_per_batch), 512)
    midx = _mod_index(rows_per_batch, tm, fixed_mod)
    full = lambda a: pl.BlockSpec(a.shape, lambda i: (0,) * a.ndim)
    n_parts = len(o_parts)
    in_specs = ([pl.BlockSpec((tm, o.shape[1]), lambda i: (i, 0)) for o in o_parts]
                + [full(w) for w in w_parts]
                + [pl.BlockSpec((tm, d), lambda i: (i, 0)),
                   pl.BlockSpec((1, 1, d), midx), full(gn),
                   pl.BlockSpec((1, 1, d), midx), pl.BlockSpec((1, 1, d), midx),
                   full(rw), full(rb)])
    return pl.pallas_call(
        functools.partial(_out_kernel, n_parts=n_parts, tm=tm),
        out_shape=[jax.ShapeDtypeStruct((t, d), F32),
                   jax.ShapeDtypeStruct((t * SUBLANES, d // SUBLANES), F32),
                   jax.ShapeDtypeStruct((t, SUBLANES), jnp.int32),
                   jax.ShapeDtypeStruct((t, SUBLANES), F32)],
        grid=(t // tm,),
        in_specs=in_specs,
        out_specs=[pl.BlockSpec((tm, d), lambda i: (i, 0)),
                   pl.BlockSpec((tm * SUBLANES, d // SUBLANES), lambda i: (i, 0)),
                   pl.BlockSpec((tm, SUBLANES), lambda i: (i, 0)),
                   pl.BlockSpec((tm, SUBLANES), lambda i: (i, 0))],
        compiler_params=_params("arbitrary"),
        name="out_proj",
    )(*o_parts, *w_parts, x, g1, gn, sc, sh, rw, rb)


DISPATCH_TILE = 512
PAIR_TILE = 2 * LANES
WEIGHT_SPLIT = 2


def _row(ref, r):
    return ref.at[pl.ds(pl.multiple_of(r * SUBLANES, SUBLANES), SUBLANES)]


def _dispatch_kernel(*refs, tm, part_tiles, n_experts):
    n_parts = len(part_tiles)
    plo_ref, phi_ref, dest_ref = refs[:3]
    tok_refs = refs[3:3 + n_parts]
    p_ref, w1_ref, w2_ref, xs_ref, o1_ref, o2_ref, zero_ref, st0, st1, sems, zsem = refs[3 + n_parts:]
    i = pl.program_id(0)
    n_tiles = sum(part_tiles)
    stage = (st0, st1)

    @pl.when(i == 0)
    def _():
        zero_ref[...] = jnp.zeros(zero_ref.shape, zero_ref.dtype)

        def fill_range(e, carry):
            lo, hi = plo_ref[e], phi_ref[e]

            def fill_chunk(c, carry):
                c_lo = lo + c * tm
                c_hi = jnp.minimum(c_lo + tm, hi)

                def start_row(r, carry):
                    pltpu.make_async_copy(zero_ref, _row(xs_ref, r), zsem).start()
                    return carry

                def wait_row(r, carry):
                    pltpu.make_async_copy(zero_ref, _row(xs_ref, r), zsem).wait()
                    return carry

                lax.fori_loop(c_lo, c_hi, start_row, carry)
                return lax.fori_loop(c_lo, c_hi, wait_row, carry)

            return lax.fori_loop(0, (hi - lo + tm - 1) // tm, fill_chunk, carry)

        lax.fori_loop(0, plo_ref.shape[0], fill_range, 0)

    def start_rows(tok_ref, slot):
        stage[slot][...] = tok_ref[...]

        def body(t, carry):
            for u in range(2):
                r = t * 2 + u
                src = _row(stage[slot], r)
                for k in range(TOP_K):
                    pltpu.make_async_copy(src, _row(xs_ref, dest_ref[0, 0, r * TOP_K + k]), sems.at[slot]).start(
                        priority=k % 2)
            return carry

        lax.fori_loop(0, tm // 2, body, 0)

    def wait_rows(slot):
        done = xs_ref.at[pl.ds(0, tm * TOP_K * SUBLANES)]
        pltpu.make_async_copy(done, done, sems.at[slot]).wait()

    first = 0
    for tok_ref, n in zip(tok_refs, part_tiles):
        for slot in range(2):
            pl.when((i >= first) & (i < first + n) & (i % 2 == slot))(
                functools.partial(start_rows, tok_ref, slot))
        first += n

    @pl.when(i < n_experts * WEIGHT_SPLIT)
    def _():
        perm = p_ref[...]
        for j in range(w1_ref.shape[3] // PAIR_TILE):
            cols = slice(j * PAIR_TILE, (j + 1) * PAIR_TILE)
            o1_ref[0, :, cols] = _dot(w1_ref[0, 0, :, cols].astype(BF16), perm).astype(BF16)
        o2_ref[0] = w2_ref[0, 0].astype(BF16)

    for slot in range(2):
        pl.when((i >= 1) & (i <= n_tiles) & ((i - 1) % 2 == slot))(functools.partial(wait_rows, slot))
        pl.when((i == pl.num_programs(0) - 1) & (i < n_tiles) & (i % 2 == slot))(
            functools.partial(wait_rows, slot))


def _dispatch(token_parts, dest, pad_lo, pad_hi, cap, w_in, w_out, layer):
    t = dest.shape[0]
    tm = _row_tile(min(p.shape[0] // SUBLANES for p in token_parts), DISPATCH_TILE)
    part_tiles = tuple(p.shape[0] // SUBLANES // tm for p in token_parts)
    n_tiles = sum(part_tiles)
    assert n_tiles * tm == t
    _, e, d, f2 = w_in.shape
    lane = jnp.arange(LANES)
    perm = jnp.zeros((PAIR_TILE, PAIR_TILE), F32).at[2 * lane, lane].set(1.0)
    perm = perm.at[2 * lane + 1, LANES + lane].set(1.0).astype(BF16)
    in_specs = [pl.BlockSpec((1, 1, tm * TOP_K), lambda i, lo, hi: (jnp.minimum(i, n_tiles - 1), 0, 0),
                             memory_space=pltpu.SMEM)]
    first = 0
    for n in part_tiles:
        in_specs.append(pl.BlockSpec(
            (tm * SUBLANES, LANES), lambda i, lo, hi, first=first, n=n: (jnp.clip(i - first, 0, n - 1), 0)))
        first += n
    ws = WEIGHT_SPLIT
    assert f2 % (ws * PAIR_TILE) == 0 and (f2 // 2) % (ws * SUBLANES * 2) == 0
    share = lambda i: jnp.minimum(i, e * ws - 1)
    in_specs += [pl.BlockSpec((PAIR_TILE, PAIR_TILE), lambda i, lo, hi: (0, 0)),
                 pl.BlockSpec((1, 1, d, f2 // ws), lambda i, lo, hi: (layer, share(i) // ws, 0, share(i) % ws)),
                 pl.BlockSpec((1, 1, f2 // 2 // ws, d),
                              lambda i, lo, hi: (layer, share(i) // ws, share(i) % ws, 0))]
    grid_spec = pltpu.PrefetchScalarGridSpec(
        num_scalar_prefetch=2,
        grid=(max(n_tiles, e * ws),),
        in_specs=in_specs,
        out_specs=[pl.BlockSpec(memory_space=pl.ANY),
                   pl.BlockSpec((1, d, f2 // ws), lambda i, lo, hi: (share(i) // ws, 0, share(i) % ws)),
                   pl.BlockSpec((1, f2 // 2 // ws, d), lambda i, lo, hi: (share(i) // ws, share(i) % ws, 0))],
        scratch_shapes=[pltpu.VMEM((SUBLANES, LANES), F32),
                        pltpu.VMEM((tm * SUBLANES, LANES), F32), pltpu.VMEM((tm * SUBLANES, LANES), F32),
                        pltpu.SemaphoreType.DMA((2,)), pltpu.SemaphoreType.DMA],
    )
    return pl.pallas_call(
        functools.partial(_dispatch_kernel, tm=tm, part_tiles=part_tiles, n_experts=e),
        out_shape=[jax.ShapeDtypeStruct((cap * SUBLANES, LANES), F32),
                   jax.ShapeDtypeStruct(w_in.shape[1:], BF16), jax.ShapeDtypeStruct(w_out.shape[1:], BF16)],
        grid_spec=grid_spec,
        compiler_params=_params("arbitrary"),
        name="dispatch",
    )(pad_lo, pad_hi, dest.reshape(n_tiles, 1, tm * TOP_K), *token_parts, perm, w_in, w_out)


EXPERT_BLOCK = 512


def _expert_kernel(be_ref, nb_ref, x_ref, w1_ref, b1_ref, w2_ref, b2_ref, o_ref, *, bm):
    del be_ref
    b = pl.program_id(0)
    n_j = w1_ref.shape[1] // LANES

    @pl.when(b < nb_ref[0])
    def _():
        x = jnp.concatenate([x_ref[pl.ds(j, bm, stride=SUBLANES), :] for j in range(n_j)], axis=1)
        u = _dot(x.astype(BF16), w1_ref[0]) + b1_ref[0]
        acts = []
        for j in range(u.shape[1] // PAIR_TILE):
            glu = jnp.minimum(u[:, j * PAIR_TILE:j * PAIR_TILE + LANES], SWIGLU_LIMIT)
            lin = jnp.clip(u[:, j * PAIR_TILE + LANES:(j + 1) * PAIR_TILE], -SWIGLU_LIMIT, SWIGLU_LIMIT)
            acts.append(glu * jax.nn.sigmoid(SWIGLU_ALPHA * glu) * (lin + 1.0))
        a = jnp.concatenate(acts, axis=1)
        y = _dot(a.astype(BF16), w2_ref[0]) + b2_ref[0]
        for j in range(y.shape[1] // LANES):
            o_ref[pl.ds(j, bm, stride=SUBLANES), :] = y[:, j * LANES:(j + 1) * LANES]

    @pl.when(b >= nb_ref[0])
    def _():
        o_ref[...] = jnp.zeros(o_ref.shape, o_ref.dtype)


def _expert_mlp(xs, blk_expert, n_used, w1, b1, w2, b2):
    bm = EXPERT_BLOCK
    nblk = blk_expert.shape[0]
    _, d, f2 = w1.shape
    grid_spec = pltpu.PrefetchScalarGridSpec(
        num_scalar_prefetch=2,
        grid=(nblk,),
        in_specs=[
            pl.BlockSpec((bm * SUBLANES, LANES), lambda b, be, nb: (b, 0)),
            pl.BlockSpec((1, d, f2), lambda b, be, nb: (be[b], 0, 0)),
            pl.BlockSpec((1, 1, f2), lambda b, be, nb: (be[b], 0, 0)),
            pl.BlockSpec((1, f2 // 2, d), lambda b, be, nb: (be[b], 0, 0)),
            pl.BlockSpec((1, 1, d), lambda b, be, nb: (be[b], 0, 0)),
        ],
        out_specs=pl.BlockSpec((bm * SUBLANES, LANES), lambda b, be, nb: (b, 0)),
    )
    return pl.pallas_call(
        functools.partial(_expert_kernel, bm=bm),
        out_shape=jax.ShapeDtypeStruct(xs.shape, F32),
        grid_spec=grid_spec,
        compiler_params=_params("arbitrary"),
        name="expert_mlp",
    )(blk_expert, n_used, xs, w1, b1, w2, b2)


def _combine_kernel(*refs, tm, final):
    if final:
        dest_ref, next_ref, x_ref, ys_ref, gate_ref, g2_ref, fg_ref, o_ref, yb0, yb1, sems = refs
    else:
        dest_ref, next_ref, x_ref, ys_ref, gate_ref, g2_ref, o_ref, yb0, yb1, sems = refs
    i = pl.program_id(0)
    ybufs = (yb0, yb1)

    def start_gather(d_ref, slot):
        def body(t, carry):
            for u in range(2):
                r = t * 2 + u
                for k in range(TOP_K):
                    pltpu.make_async_copy(_row(ys_ref, d_ref[0, 0, r * TOP_K + k]),
                                          _row(ybufs[slot], k * tm + r), sems.at[slot]).start(priority=k % 2)
            return carry

        lax.fori_loop(0, tm // 2, body, 0)

    def finish(slot):
        ybuf = ybufs[slot]
        pltpu.make_async_copy(ybuf, ybuf, sems.at[slot]).wait()
        gate = gate_ref[...]
        parts = []
        for j in range(x_ref.shape[1] // LANES):
            f = gate[:, 0:1] * ybuf[pl.ds(j, tm, stride=SUBLANES), :]
            for k in range(1, TOP_K):
                f = f + gate[:, k:k + 1] * ybuf[pl.ds(k * tm * SUBLANES + j, tm, stride=SUBLANES), :]
            parts.append(f)
        x = x_ref[...] + g2_ref[0] * jnp.concatenate(parts, axis=1)
        if final:
            x = _rms(x, fg_ref[...])
        o_ref[...] = x

    @pl.when(i == 0)
    def _():
        start_gather(dest_ref, 0)

    for slot in range(2):
        @pl.when(i % 2 == slot)
        def _(slot=slot):
            @pl.when(i + 1 < pl.num_programs(0))
            def _():
                start_gather(next_ref, 1 - slot)

            finish(slot)


def _combine(x, ys, dest, gates, g2, final_g, *, row_off, rows_per_batch, fixed_mod):
    t, d = x.shape
    tm = _row_tile(min(t, rows_per_batch), 256)
    midx = _mod_index(rows_per_batch, tm, fixed_mod)
    final = final_g is not None
    off = row_off // tm
    t_all = dest.shape[0]
    n_steps = t // tm
    dest_tiles = dest.reshape(t_all // tm, 1, tm * TOP_K)
    in_specs = [pl.BlockSpec((1, 1, tm * TOP_K), lambda i: (off + i, 0, 0), memory_space=pltpu.SMEM),
                pl.BlockSpec((1, 1, tm * TOP_K), lambda i: (off + jnp.minimum(i + 1, n_steps - 1), 0, 0),
                             memory_space=pltpu.SMEM),
                pl.BlockSpec((tm, d), lambda i: (i, 0)),
                pl.BlockSpec(memory_space=pl.ANY),
                pl.BlockSpec((tm, SUBLANES), lambda i: (off + i, 0)),
                pl.BlockSpec((1, 1, d), midx)]
    args = [dest_tiles, dest_tiles, x, ys, gates, g2]
    if final:
        in_specs.append(pl.BlockSpec(final_g.shape, lambda i: (0, 0)))
        args.append(final_g)
    return pl.pallas_call(
        functools.partial(_combine_kernel, tm=tm, final=final),
        out_shape=jax.ShapeDtypeStruct((t, d), F32),
        grid=(n_steps,),
        in_specs=in_specs,
        out_specs=pl.BlockSpec((tm, d), lambda i: (i, 0)),
        scratch_shapes=[pltpu.VMEM((TOP_K * tm * SUBLANES, LANES), F32),
                        pltpu.VMEM((TOP_K * tm * SUBLANES, LANES), F32),
                        pltpu.SemaphoreType.DMA((2,))],
        compiler_params=_params("arbitrary"),
        name="combine_final" if final else "combine",
    )(*args)


def _pad_heads(w, n_heads, axis):
    shape = list(w.shape)
    hd = shape[axis] // n_heads
    w = w.reshape(shape[:axis] + [n_heads, hd] + shape[axis + 1:])
    pad = [(0, 0)] * w.ndim
    pad[axis + 1] = (0, LANES - hd)
    w = jnp.pad(w, pad)
    shape[axis] = n_heads * LANES
    return w.reshape(shape)


def _rope_tables(s, rot_dim, lo):
    pos = jnp.arange(s, dtype=jnp.int32)
    rows, cols = (pos // GRID_W).astype(F32), (pos % GRID_W).astype(F32)
    quarter = rot_dim // 4
    inv = ROPE_THETA ** (-jnp.arange(quarter, dtype=F32) / quarter)
    ang = jnp.concatenate([rows[:, None] * inv, cols[:, None] * inv], axis=-1)
    cos, sin = jnp.cos(ang), jnp.sin(ang)
    hi = LANES - lo - rot_dim
    cos_t = jnp.concatenate([jnp.ones((s, lo), F32), cos, cos, jnp.ones((s, hi), F32)], axis=1)
    sin_t = jnp.concatenate([jnp.zeros((s, lo), F32), -sin, sin, jnp.zeros((s, hi), F32)], axis=1)
    return cos_t, sin_t


def _moe_plan(eidx, bm):
    t = eidx.shape[0]
    ids = jnp.arange(N_EXPERTS, dtype=jnp.int32)
    sel = eidx[:, :, None] == ids
    hit = sel.astype(jnp.int32).sum(1)
    chunk = _row_tile(t, 256)
    tri = jnp.tril(jnp.ones((chunk, chunk), F32))
    within = jnp.einsum("ij,cjk->cik", tri, hit.astype(F32).reshape(t // chunk, chunk, N_EXPERTS),
                        precision=HIGHEST)
    totals = within[:, -1, :]
    incl = (within + (jnp.cumsum(totals, axis=0) - totals)[:, None, :]).reshape(t, N_EXPERTS)
    incl = incl.astype(jnp.int32)
    counts = incl[-1]
    rank = jnp.where(sel, (incl - hit)[:, None, :], 0).sum(-1)
    padded = (counts + bm - 1) // bm * bm
    pends = jnp.cumsum(padded)
    pstarts = pends - padded
    dest = jnp.where(sel, pstarts, 0).sum(-1) + rank
    n_assign = t * TOP_K
    nblk = -(-(n_assign + N_EXPERTS * (bm - 1)) // bm)
    blk_start = jnp.arange(nblk, dtype=jnp.int32) * bm
    blk_expert = jnp.minimum((blk_start[:, None] >= pends[None, :]).sum(1), N_EXPERTS - 1)
    n_used = (pends[-1] // bm).astype(jnp.int32).reshape(1)
    pad_lo = jnp.concatenate([pstarts + counts, pends[-1:]]).astype(jnp.int32)
    pad_hi = jnp.concatenate([pends, jnp.full((1,), nblk * bm, pends.dtype)]).astype(jnp.int32)
    return dest.astype(jnp.int32), blk_expert.astype(jnp.int32), n_used, pad_lo, pad_hi


def _moe(token_parts, eidx, w_in, b1, w_out, b2, layer):
    dest, blk_expert, n_used, pad_lo, pad_hi = _moe_plan(eidx, EXPERT_BLOCK)
    xs, w1, w2 = _dispatch(token_parts, dest, pad_lo, pad_hi, blk_expert.shape[0] * EXPERT_BLOCK,
                           w_in, w_out, layer)
    return _expert_mlp(xs, blk_expert, n_used, w1, b1, w2, b2), dest


def kernel(x, c, ctx, c_ctx, ada_w, ada_b, norm_mix_g, norm_ffn_g, ab_w_in, mla_q_norm_g, mla_wq_b,
           mla_kv_norm_g, mla_wkv_b, swa_sink, ab_w_out, c_w_in, c_q_norm_g, c_k_norm_g, c_w_out,
           router_w, router_b, moe_w_in, moe_b_in, moe_w_out, moe_b_out, final_norm_g):
    bn, s, d = x.shape
    n_ctx = ctx.shape[1]
    depth = ada_w.shape[0]
    t_lat, t_ctx = bn * s, bn * n_ctx
    row = lambda v: v.reshape(1, -1)

    n_mod = -(-(bn + 1) // SUBLANES) * SUBLANES
    c_all = jnp.concatenate([c, c_ctx[None], jnp.zeros((n_mod - bn - 1, d), F32)], axis=0)
    mod = _modulation(c_all, ada_w, ada_b).reshape(depth, n_mod, 6, 1, d)
    ctx_row = bn

    tab_a = _rope_tables(s, HEAD_DIM_A, 0)
    tab_b = _rope_tables(s, MLA_ROPE, MLA_NOPE)
    tab_c = _rope_tables(s, HEAD_DIM_C, 0)

    xl = x.reshape(t_lat, d)
    xc = ctx.reshape(t_ctx, d)
    for i in range(depth):
        with_ctx = i < depth - 1
        j = i // 2
        sh1, sc1, g1, sh2, sc2, g2 = [mod[i, :, m] for m in range(6)]
        gmix, gffn = row(norm_mix_g[i]), row(norm_ffn_g[i])
        if i % 2 == 0:
            w_in = ab_w_in[j]
            sizes = [N_HEADS_A * HEAD_DIM_A, N_KV_A * HEAD_DIM_A, N_KV_A * HEAD_DIM_A,
                     MLA_Q_RANK, MLA_KV_RANK, MLA_ROPE]
            offs = [0]
            for n in sizes:
                offs.append(offs[-1] + n)
            cols = [w_in[:, offs[m]:offs[m + 1]] for m in range(6)]
            krg = jnp.pad(cols[5], ((0, 0), (MLA_NOPE, LANES - MLA_NOPE - MLA_ROPE)))
            w1 = jnp.concatenate([_pad_heads(cols[0], N_HEADS_A, 1), _pad_heads(cols[1], N_KV_A, 1),
                                  _pad_heads(cols[2], N_KV_A, 1), cols[3], cols[4], krg], axis=1).astype(BF16)
            wq = _pad_heads(mla_wq_b[j], MLA_HEADS, 1).astype(BF16)
            wkv = mla_wkv_b[j].reshape(MLA_KV_RANK, MLA_HEADS, MLA_NOPE + MLA_V)
            wk = _pad_heads(wkv[:, :, :MLA_NOPE].reshape(MLA_KV_RANK, -1), MLA_HEADS, 1).astype(BF16)
            wv = _pad_heads(wkv[:, :, MLA_NOPE:].reshape(MLA_KV_RANK, -1), MLA_HEADS, 1).astype(BF16)
            ws = [w1, row(mla_q_norm_g[j]), wq, row(mla_kv_norm_g[j]), wk, wv]
            na = N_HEADS_A * HEAD_DIM_A
            wo_a = ab_w_out[j][:na].astype(BF16)
            wo_b = ab_w_out[j][na:].astype(BF16)
            sink = swa_sink[j]

            qa, ka, va, qb, kb, vb = _ab_in(xl, gmix, sc1, sh1, ws, tab_a + tab_b, s, None)
            qa_c, ka_c, va_c, qb_c, kb_c, vb_c = _ab_in(xc, gmix, sc1, sh1, ws, None, n_ctx, ctx_row)
            ga = N_HEADS_A // N_KV_A
            cfg_a = dict(batch=bn, n_kv_total=N_KV_A, n_kv=1, n_group=ga, out_dim=HEAD_DIM_A)
            cfg_b = dict(batch=bn, n_kv_total=MLA_HEADS, n_kv=MLA_STEP_HEADS, n_group=1, out_dim=MLA_V)
            oa = _attention(qa, ka, va, ka_c, va_c, sink, window=True, tq=512, **cfg_a)
            ob = _attention(qb, kb, vb, kb_c, vb_c, None, window=False, tq=1024, **cfg_b)
            o_lat, w_o = [oa, ob], [wo_a, wo_b]
            if with_ctx:
                oa_c = _attention(qa_c, None, None, ka_c, va_c, sink, window=False, tq=256, **cfg_a)
                ob_c = _attention(qb_c, None, None, kb_c, vb_c, None, window=False, tq=256, **cfg_b)
                o_ctx = [oa_c, ob_c]
        else:
            w_in = c_w_in[j].astype(BF16)
            qn, kn = row(c_q_norm_g[j]), row(c_k_norm_g[j])
            q, k, v = _c_in(xl, gmix, sc1, sh1, w_in, qn, kn, tab_c, s, None)
            q_c, k_c, v_c = _c_in(xc, gmix, sc1, sh1, w_in, qn, kn, None, n_ctx, ctx_row)
            gc = N_HEADS_C // N_KV_C
            cfg_c = dict(batch=bn, n_kv_total=N_KV_C, n_kv=1, n_group=gc, out_dim=HEAD_DIM_C,
                         window=False, tq=1024)
            o = _attention(q, k, v, k_c, v_c, None, **cfg_c)
            o_lat, w_o = [o], [c_w_out[j].astype(BF16)]
            if with_ctx:
                o_ctx = [_attention(q_c, None, None, k_c, v_c, None, **cfg_c)]

        rw = jnp.pad(router_w[i], ((0, 0), (0, LANES - N_EXPERTS)))
        rw_hi = rw.astype(BF16)
        rw = jnp.concatenate([rw_hi, (rw - rw_hi.astype(F32)).astype(BF16)], axis=1)
        rb = jnp.concatenate([router_b[i], jnp.full((LANES - N_EXPERTS,), NEG_INF, F32)]).reshape(1, LANES)
        xl, tok_l, idx_l, gate_l = _out_proj(o_lat, w_o, xl, g1, gffn, sc2, sh2, rw, rb, s, None)
        if with_ctx:
            xc, tok_c, idx_c, gate_c = _out_proj(o_ctx, w_o, xc, g1, gffn, sc2, sh2, rw, rb, n_ctx, ctx_row)
            tokens = [tok_l, tok_c]
            eidx = jnp.concatenate([idx_l, idx_c], axis=0)
            gates = jnp.concatenate([gate_l, gate_c], axis=0)
        else:
            tokens, eidx, gates = [tok_l], idx_l, gate_l

        n_tiles = moe_b_in.shape[-1] // PAIR_TILE
        b1e = moe_b_in[i].reshape(N_EXPERTS, n_tiles, LANES, 2).transpose(0, 1, 3, 2)
        b1e = b1e.reshape(N_EXPERTS, 1, -1)
        b2e = moe_b_out[i][:, None, :]
        ys, dest = _moe(tokens, eidx[:, :TOP_K], moe_w_in, b1e, moe_w_out, b2e, i)

        last = i == depth - 1
        xl = _combine(xl, ys, dest, gates, g2, row(final_norm_g) if last else None,
                      row_off=0, rows_per_batch=s, fixed_mod=None)
        if with_ctx:
            xc = _combine(xc, ys, dest, gates, g2, None,
                          row_off=t_lat, rows_per_batch=n_ctx, fixed_mod=ctx_row)
    return xl.reshape(bn, s, d)
```

```python
import functools

import jax
import jax.numpy as jnp
from jax import lax
from jax.experimental import pallas as pl
from jax.experimental.pallas import tpu as pltpu

GRID_W = 64
N_HEADS_A, N_KV_A, HEAD_DIM_A, WINDOW = 8, 2, 64, 128
MLA_HEADS, MLA_Q_RANK, MLA_KV_RANK, MLA_NOPE, MLA_ROPE, MLA_V = 8, 384, 256, 64, 32, 64
N_HEADS_C, N_KV_C, HEAD_DIM_C = 8, 2, 128
N_EXPERTS, TOP_K = 32, 4
MLA_STEP_HEADS = 4
SWIGLU_ALPHA, SWIGLU_LIMIT = 1.702, 7.0
ROPE_THETA, RMS_EPS, NEG_INF = 10000.0, 1e-6, -1e30
LOG2_E = 1.4426950408889634

LANES = 128
SUBLANES = 8
VMEM_LIMIT_BYTES = 56 * 1024 * 1024

F32 = jnp.float32
BF16 = jnp.bfloat16
HIGHEST = lax.Precision.HIGHEST


def _params(*sem):
    return pltpu.CompilerParams(dimension_semantics=sem, vmem_limit_bytes=VMEM_LIMIT_BYTES)


def _dot(a, b):
    return jnp.dot(a, b, preferred_element_type=F32)


def _dot_t(a, b):
    return lax.dot_general(a, b, (((1,), (1,)), ((), ())), preferred_element_type=F32)


def _rms(x, g):
    return x * lax.rsqrt(jnp.mean(x * x, axis=-1, keepdims=True) + RMS_EPS) * g


def _modnorm(x, g, sc, sh):
    return _rms(x, g) * (1.0 + sc) + sh


def _rope_group(x, cos, sin, half, lo):
    lane = lax.broadcasted_iota(jnp.int32, x.shape, 1)
    first = (lane >= lo) & (lane < lo + half)
    rot = jnp.where(first, pltpu.roll(x, LANES - half, 1), pltpu.roll(x, half, 1))
    return x * cos + rot * sin


def _groups(x):
    return [x[:, i * LANES:(i + 1) * LANES] for i in range(x.shape[1] // LANES)]


def _mod_kernel(c_ref, w_ref, b_ref, o_ref):
    c = c_ref[...]
    a = c * jax.nn.sigmoid(c)
    o_ref[0] = jnp.dot(a, w_ref[0], preferred_element_type=F32, precision=HIGHEST) + b_ref[0]


def _modulation(c_all, ada_w, ada_b):
    depth, d, n = ada_w.shape
    r = c_all.shape[0]
    nt = n // 4
    return pl.pallas_call(
        _mod_kernel,
        out_shape=jax.ShapeDtypeStruct((depth, r, n), F32),
        grid=(depth, n // nt),
        in_specs=[
            pl.BlockSpec((r, d), lambda i, j: (0, 0)),
            pl.BlockSpec((1, d, nt), lambda i, j: (i, 0, j)),
            pl.BlockSpec((1, 1, nt), lambda i, j: (i, 0, j)),
        ],
        out_specs=pl.BlockSpec((1, r, nt), lambda i, j: (i, 0, j)),
        compiler_params=_params("arbitrary", "arbitrary"),
        name="modulation",
    )(c_all, ada_w, ada_b.reshape(depth, 1, n))


def _ab_in_kernel(*refs, rope):
    if rope:
        (x_ref, g_ref, sc_ref, sh_ref, w1_ref, qg_ref, wq_ref, kvg_ref, wk_ref, wv_ref,
         ca_ref, sa_ref, cb_ref, sb_ref, qa_ref, ka_ref, va_ref, qb_ref, kb_ref, vb_ref) = refs
    else:
        (x_ref, g_ref, sc_ref, sh_ref, w1_ref, qg_ref, wq_ref, kvg_ref, wk_ref, wv_ref,
         qa_ref, ka_ref, va_ref, qb_ref, kb_ref, vb_ref) = refs
    h = _modnorm(x_ref[...], g_ref[...], sc_ref[0], sh_ref[0]).astype(BF16)
    p = _dot(h, w1_ref[...])
    nqa = N_HEADS_A * LANES
    nka = N_KV_A * LANES
    o = 0
    qa = p[:, o:o + nqa]; o += nqa
    ka = p[:, o:o + nka]; o += nka
    va = p[:, o:o + nka]; o += nka
    cq = p[:, o:o + MLA_Q_RANK]; o += MLA_Q_RANK
    ckv = p[:, o:o + MLA_KV_RANK]; o += MLA_KV_RANK
    krg = p[:, o:o + LANES]

    qb = _dot(_rms(cq, qg_ref[...]).astype(BF16), wq_ref[...])
    ckv_n = _rms(ckv, kvg_ref[...]).astype(BF16)
    kb = _dot(ckv_n, wk_ref[...])
    vb = _dot(ckv_n, wv_ref[...])

    scale_a = HEAD_DIM_A ** -0.5 * LOG2_E
    scale_b = (MLA_NOPE + MLA_ROPE) ** -0.5 * LOG2_E
    if rope:
        ca, sa, cb, sb = ca_ref[...], sa_ref[...], cb_ref[...], sb_ref[...]
        half_a, half_b = HEAD_DIM_A // 2, MLA_ROPE // 2
        qa_g = [_rope_group(t, ca, sa, half_a, 0) * scale_a for t in _groups(qa)]
        ka_g = [_rope_group(t, ca, sa, half_a, 0) for t in _groups(ka)]
        qb_g = [_rope_group(t, cb, sb, half_b, MLA_NOPE) * scale_b for t in _groups(qb)]
        krg = _rope_group(krg, cb, sb, half_b, MLA_NOPE)
    else:
        qa_g = [t * scale_a for t in _groups(qa)]
        ka_g = _groups(ka)
        qb_g = [t * scale_b for t in _groups(qb)]
    kb_g = [t + krg for t in _groups(kb)]
    qa_ref[...] = jnp.concatenate(qa_g, axis=1).astype(BF16)
    ka_ref[...] = jnp.concatenate(ka_g, axis=1).astype(BF16)
    va_ref[...] = va.astype(BF16)
    qb_ref[...] = jnp.concatenate(qb_g, axis=1).astype(BF16)
    kb_ref[...] = jnp.concatenate(kb_g, axis=1).astype(BF16)
    vb_ref[...] = vb.astype(BF16)


def _row_tile(t, pref=512):
    tm = pref
    while t % tm:
        tm //= 2
    return tm


def _mod_index(rows_per_batch, tm, fixed):
    if fixed is not None:
        return lambda i: (fixed, 0, 0)
    return lambda i: ((i * tm) // rows_per_batch, 0, 0)


def _ab_in(x, g, sc, sh, w, tables, rows_per_batch, fixed_mod):
    t, d = x.shape
    tm = _row_tile(min(t, rows_per_batch))
    rope = tables is not None
    midx = _mod_index(rows_per_batch, tm, fixed_mod)
    full = lambda a: pl.BlockSpec(a.shape, lambda i: (0,) * a.ndim)
    in_specs = [
        pl.BlockSpec((tm, d), lambda i: (i, 0)),
        full(g),
        pl.BlockSpec((1, 1, d), midx),
        pl.BlockSpec((1, 1, d), midx),
    ] + [full(a) for a in w]
    args = [x, g, sc, sh] + list(w)
    if rope:
        nt = rows_per_batch // tm
        for tab in tables:
            in_specs.append(pl.BlockSpec((tm, LANES), lambda i: (i % nt, 0)))
            args.append(tab)
    widths = [N_HEADS_A * LANES, N_KV_A * LANES, N_KV_A * LANES,
              MLA_HEADS * LANES, MLA_HEADS * LANES, MLA_HEADS * LANES]
    return pl.pallas_call(
        functools.partial(_ab_in_kernel, rope=rope),
        out_shape=[jax.ShapeDtypeStruct((t, n), BF16) for n in widths],
        grid=(t // tm,),
        in_specs=in_specs,
        out_specs=[pl.BlockSpec((tm, n), lambda i: (i, 0)) for n in widths],
        compiler_params=_params("arbitrary"),
        name="ab_in_rope" if rope else "ab_in",
    )(*args)


def _c_in_kernel(*refs, rope):
    if rope:
        x_ref, g_ref, sc_ref, sh_ref, w_ref, qn_ref, kn_ref, cc_ref, sc2_ref, q_ref, k_ref, v_ref = refs
    else:
        x_ref, g_ref, sc_ref, sh_ref, w_ref, qn_ref, kn_ref, q_ref, k_ref, v_ref = refs
    h = _modnorm(x_ref[...], g_ref[...], sc_ref[0], sh_ref[0]).astype(BF16)
    p = _dot(h, w_ref[...])
    nq = N_HEADS_C * HEAD_DIM_C
    nk = N_KV_C * HEAD_DIM_C
    q_g = [_rms(t, qn_ref[...]) for t in _groups(p[:, :nq])]
    k_g = [_rms(t, kn_ref[...]) for t in _groups(p[:, nq:nq + nk])]
    scale = HEAD_DIM_C ** -0.5 * LOG2_E
    if rope:
        cc, ss = cc_ref[...], sc2_ref[...]
        q_g = [_rope_group(t, cc, ss, HEAD_DIM_C // 2, 0) for t in q_g]
        k_g = [_rope_group(t, cc, ss, HEAD_DIM_C // 2, 0) for t in k_g]
    q_ref[...] = jnp.concatenate([t * scale for t in q_g], axis=1).astype(BF16)
    k_ref[...] = jnp.concatenate(k_g, axis=1).astype(BF16)
    v_ref[...] = p[:, nq + nk:].astype(BF16)


def _c_in(x, g, sc, sh, w, qn, kn, tables, rows_per_batch, fixed_mod):
    t, d = x.shape
    tm = _row_tile(min(t, rows_per_batch))
    rope = tables is not None
    midx = _mod_index(rows_per_batch, tm, fixed_mod)
    full = lambda a: pl.BlockSpec(a.shape, lambda i: (0,) * a.ndim)
    in_specs = [
        pl.BlockSpec((tm, d), lambda i: (i, 0)),
        full(g),
        pl.BlockSpec((1, 1, d), midx),
        pl.BlockSpec((1, 1, d), midx),
        full(w), full(qn), full(kn),
    ]
    args = [x, g, sc, sh, w, qn, kn]
    if rope:
        nt = rows_per_batch // tm
        for tab in tables:
            in_specs.append(pl.BlockSpec((tm, LANES), lambda i: (i % nt, 0)))
            args.append(tab)
    widths = [N_HEADS_C * HEAD_DIM_C, N_KV_C * HEAD_DIM_C, N_KV_C * HEAD_DIM_C]
    return pl.pallas_call(
        functools.partial(_c_in_kernel, rope=rope),
        out_shape=[jax.ShapeDtypeStruct((t, n), BF16) for n in widths],
        grid=(t // tm,),
        in_specs=in_specs,
        out_specs=[pl.BlockSpec((tm, n), lambda i: (i, 0)) for n in widths],
        compiler_params=_params("arbitrary"),
        name="c_in_rope" if rope else "c_in",
    )(*args)


def _attn_kernel(*refs, n_kv, n_group, out_dim, has_lat, has_sink, window, tq, s_lat, kw):
    refs = list(refs)
    sink_ref = refs.pop(0) if has_sink else None
    q_ref = refs.pop(0)
    if has_lat:
        k1_ref, v1_ref = refs.pop(0), refs.pop(0)
    k2_ref, v2_ref, o_ref = refs
    hk0 = pl.program_id(1) * n_kv
    qi = pl.program_id(2)
    if window:
        ws = pl.multiple_of(jnp.clip(qi * tq - WINDOW, 0, s_lat - kw), LANES)
        rows = pl.ds(ws, kw)
        qpos = qi * tq + lax.broadcasted_iota(jnp.int32, (tq, kw), 0)
        kpos = ws + lax.broadcasted_iota(jnp.int32, (tq, kw), 1)
        ok = jnp.abs(kpos - qpos) <= WINDOW
    else:
        rows = slice(None)
    outs = []
    for j in range(n_kv):
        lanes = slice(j * LANES, (j + 1) * LANES)
        k2, v2 = k2_ref[:, lanes], v2_ref[:, lanes]
        if has_lat:
            k1, v1 = k1_ref[rows, lanes], v1_ref[rows, lanes]
        for g in range(n_group):
            h = j * n_group + g
            q = q_ref[:, h * LANES:(h + 1) * LANES]
            s2 = _dot_t(q, k2)
            m = jnp.max(s2, axis=-1, keepdims=True)
            if has_lat:
                s1 = _dot_t(q, k1)
                if window:
                    s1 = jnp.where(ok, s1, NEG_INF)
                m = jnp.maximum(m, jnp.max(s1, axis=-1, keepdims=True))
            if has_sink:
                sk = sink_ref[hk0 * n_group + h] * LOG2_E
                m = jnp.maximum(m, sk)
            e2 = jnp.exp2(s2 - m)
            den = jnp.sum(e2, axis=-1, keepdims=True)
            acc = _dot(e2.astype(BF16), v2)
            if has_lat:
                e1 = jnp.exp2(s1 - m)
                den = den + jnp.sum(e1, axis=-1, keepdims=True)
                acc = acc + _dot(e1.astype(BF16), v1)
            if has_sink:
                den = den + jnp.exp2(sk - m)
            outs.append(acc / den)
    if out_dim == LANES:
        for h, o in enumerate(outs):
            o_ref[:, h * LANES:(h + 1) * LANES] = o.astype(BF16)
    else:
        lane = lax.broadcasted_iota(jnp.int32, (tq, LANES), 1)
        for p in range(len(outs) // 2):
            pair = jnp.where(lane < out_dim, outs[2 * p], pltpu.roll(outs[2 * p + 1], out_dim, 1))
            o_ref[:, p * LANES:(p + 1) * LANES] = pair.astype(BF16)


def _attention(q, k_lat, v_lat, k_ctx, v_ctx, sink, *, batch, n_kv_total, n_kv, n_group, out_dim,
               window, tq):
    tq_total = q.shape[0] // batch
    tq = min(tq, tq_total)
    n_ctx = k_ctx.shape[0] // batch
    has_lat = k_lat is not None
    has_sink = sink is not None
    s_lat = k_lat.shape[0] // batch if has_lat else 0
    kw = min(tq + 2 * WINDOW, s_lat) if window else 0
    nq = tq_total // tq
    n_heads = n_kv * n_group
    assert out_dim == LANES or (2 * out_dim == LANES and n_heads % 2 == 0)
    in_specs, args = [], []
    if has_sink:
        in_specs.append(pl.BlockSpec(memory_space=pltpu.SMEM))
        args.append(sink)
    in_specs.append(pl.BlockSpec((tq, n_heads * LANES), lambda b, h, i: (b * nq + i, h)))
    args.append(q)
    if has_lat:
        in_specs += [pl.BlockSpec((s_lat, n_kv * LANES), lambda b, h, i: (b, h))] * 2
        args += [k_lat, v_lat]
    in_specs += [pl.BlockSpec((n_ctx, n_kv * LANES), lambda b, h, i: (b, h))] * 2
    args += [k_ctx, v_ctx]
    n_steps = n_kv_total // n_kv
    return pl.pallas_call(
        functools.partial(_attn_kernel, n_kv=n_kv, n_group=n_group, out_dim=out_dim, has_lat=has_lat,
                          has_sink=has_sink, window=window, tq=tq, s_lat=s_lat, kw=kw),
        out_shape=jax.ShapeDtypeStruct((q.shape[0], n_steps * n_heads * out_dim), BF16),
        grid=(batch, n_steps, nq),
        in_specs=in_specs,
        out_specs=pl.BlockSpec((tq, n_heads * out_dim), lambda b, h, i: (b * nq + i, h)),
        compiler_params=_params("arbitrary", "arbitrary", "arbitrary"),
        name="attn_win" if window else ("attn_dense" if has_lat else "attn_ctx"),
    )(*args)


def _out_kernel(*refs, n_parts, tm):
    o_refs = refs[:n_parts]
    w_refs = refs[n_parts:2 * n_parts]
    (x_ref, g1_ref, gn_ref, sc_ref, sh_ref, rw_ref, rb_ref,
     xo_ref, tok_ref, idx_ref, gate_ref) = refs[2 * n_parts:]
    y = _dot(o_refs[0][...], w_refs[0][...])
    for o_r, w_r in zip(o_refs[1:], w_refs[1:]):
        y = y + _dot(o_r[...], w_r[...])
    x = x_ref[...] + g1_ref[0] * y
    xo_ref[...] = x
    tok = _modnorm(x, gn_ref[...], sc_ref[0], sh_ref[0])
    for j in range(tok.shape[1] // LANES):
        tok_ref[pl.ds(j, tm, stride=SUBLANES), :] = tok[:, j * LANES:(j + 1) * LANES]
    tok_hi = tok.astype(BF16)
    tok_lo = (tok - tok_hi.astype(F32)).astype(BF16)
    hh_hl = _dot(tok_hi, rw_ref[...])
    logits = (hh_hl[:, :LANES] + hh_hl[:, LANES:] + _dot(tok_lo, rw_ref[:, :LANES])) + rb_ref[...]
    lane = lax.broadcasted_iota(jnp.int32, logits.shape, 1).astype(F32)
    vals, idxs = [], []
    l = logits
    for _ in range(TOP_K):
        m = jnp.max(l, axis=-1, keepdims=True)
        idx = jnp.min(jnp.where(l == m, lane, float(LANES)), axis=-1, keepdims=True)
        vals.append(m)
        idxs.append(idx)
        l = jnp.where(lane == idx, -jnp.inf, l)
    es = [jnp.exp(v - vals[0]) for v in vals]
    den = es[0]
    for e in es[1:]:
        den = den + e
    lane8 = lax.broadcasted_iota(jnp.int32, (tm, SUBLANES), 1)
    io = jnp.zeros((tm, SUBLANES), jnp.int32)
    go = jnp.zeros((tm, SUBLANES), F32)
    for k in range(TOP_K):
        io = jnp.where(lane8 == k, idxs[k].astype(jnp.int32), io)
        go = jnp.where(lane8 == k, es[k] / den, go)
    idx_ref[...] = io
    gate_ref[...] = go


def _out_proj(o_parts, w_parts, x, g1, gn, sc, sh, rw, rb, rows_per_batch, fixed_mod):
    t, d = x.shape
    tm = _row_tile(min(t, rows_per_batch), 512)
    midx = _mod_index(rows_per_batch, tm, fixed_mod)
    full = lambda a: pl.BlockSpec(a.shape, lambda i: (0,) * a.ndim)
    n_parts = len(o_parts)
    in_specs = ([pl.BlockSpec((tm, o.shape[1]), lambda i: (i, 0)) for o in o_parts]
                + [full(w) for w in w_parts]
                + [pl.BlockSpec((tm, d), lambda i: (i, 0)),
                   pl.BlockSpec((1, 1, d), midx), full(gn),
                   pl.BlockSpec((1, 1, d), midx), pl.BlockSpec((1, 1, d), midx),
                   full(rw), full(rb)])
    return pl.pallas_call(
        functools.partial(_out_kernel, n_parts=n_parts, tm=tm),
        out_shape=[jax.ShapeDtypeStruct((t, d), F32),
                   jax.ShapeDtypeStruct((t * SUBLANES, d // SUBLANES), F32),
                   jax.ShapeDtypeStruct((t, SUBLANES), jnp.int32),
                   jax.ShapeDtypeStruct((t, SUBLANES), F32)],
        grid=(t // tm,),
        in_specs=in_specs,
        out_specs=[pl.BlockSpec((tm, d), lambda i: (i, 0)),
                   pl.BlockSpec((tm * SUBLANES, d // SUBLANES), lambda i: (i, 0)),
                   pl.BlockSpec((tm, SUBLANES), lambda i: (i, 0)),
                   pl.BlockSpec((tm, SUBLANES), lambda i: (i, 0))],
        compiler_params=_params("arbitrary"),
        name="out_proj",
    )(*o_parts, *w_parts, x, g1, gn, sc, sh, rw, rb)


DISPATCH_TILE = 512
PAIR_TILE = 2 * LANES
WEIGHT_SPLIT = 2


def _row(ref, r):
    return ref.at[pl.ds(pl.multiple_of(r * SUBLANES, SUBLANES), SUBLANES)]


def _dispatch_kernel(*refs, tm, part_tiles, n_experts):
    n_parts = len(part_tiles)
    plo_ref, phi_ref, dest_ref = refs[:3]
    tok_refs = refs[3:3 + n_parts]
    p_ref, w1_ref, w2_ref, xs_ref, o1_ref, o2_ref, zero_ref, st0, st1, sems, zsem = refs[3 + n_parts:]
    i = pl.program_id(0)
    n_tiles = sum(part_tiles)
    stage = (st0, st1)

    @pl.when(i == 0)
    def _():
        zero_ref[...] = jnp.zeros(zero_ref.shape, zero_ref.dtype)

        def fill_range(e, carry):
            lo, hi = plo_ref[e], phi_ref[e]

            def fill_chunk(c, carry):
                c_lo = lo + c * tm
                c_hi = jnp.minimum(c_lo + tm, hi)

                def start_row(r, carry):
                    pltpu.make_async_copy(zero_ref, _row(xs_ref, r), zsem).start()
                    return carry

                def wait_row(r, carry):
                    pltpu.make_async_copy(zero_ref, _row(xs_ref, r), zsem).wait()
                    return carry

                lax.fori_loop(c_lo, c_hi, start_row, carry)
                return lax.fori_loop(c_lo, c_hi, wait_row, carry)

            return lax.fori_loop(0, (hi - lo + tm - 1) // tm, fill_chunk, carry)

        lax.fori_loop(0, plo_ref.shape[0], fill_range, 0)

    def start_rows(tok_ref, slot):
        stage[slot][...] = tok_ref[...]

        def body(t, carry):
            for u in range(2):
                r = t * 2 + u
                src = _row(stage[slot], r)
                for k in range(TOP_K):
                    pltpu.make_async_copy(src, _row(xs_ref, dest_ref[0, 0, r * TOP_K + k]), sems.at[slot]).start(
                        priority=k % 2)
            return carry

        lax.fori_loop(0, tm // 2, body, 0)

    def wait_rows(slot):
        done = xs_ref.at[pl.ds(0, tm * TOP_K * SUBLANES)]
        pltpu.make_async_copy(done, done, sems.at[slot]).wait()

    first = 0
    for tok_ref, n in zip(tok_refs, part_tiles):
        for slot in range(2):
            pl.when((i >= first) & (i < first + n) & (i % 2 == slot))(
                functools.partial(start_rows, tok_ref, slot))
        first += n

    @pl.when(i < n_experts * WEIGHT_SPLIT)
    def _():
        perm = p_ref[...]
        for j in range(w1_ref.shape[3] // PAIR_TILE):
            cols = slice(j * PAIR_TILE, (j + 1) * PAIR_TILE)
            o1_ref[0, :, cols] = _dot(w1_ref[0, 0, :, cols].astype(BF16), perm).astype(BF16)
        o2_ref[0] = w2_ref[0, 0].astype(BF16)

    for slot in range(2):
        pl.when((i >= 1) & (i <= n_tiles) & ((i - 1) % 2 == slot))(functools.partial(wait_rows, slot))
        pl.when((i == pl.num_programs(0) - 1) & (i < n_tiles) & (i % 2 == slot))(
            functools.partial(wait_rows, slot))


def _dispatch(token_parts, dest, pad_lo, pad_hi, cap, w_in, w_out, layer):
    t = dest.shape[0]
    tm = _row_tile(min(p.shape[0] // SUBLANES for p in token_parts), DISPATCH_TILE)
    part_tiles = tuple(p.shape[0] // SUBLANES // tm for p in token_parts)
    n_tiles = sum(part_tiles)
    assert n_tiles * tm == t
    _, e, d, f2 = w_in.shape
    lane = jnp.arange(LANES)
    perm = jnp.zeros((PAIR_TILE, PAIR_TILE), F32).at[2 * lane, lane].set(1.0)
    perm = perm.at[2 * lane + 1, LANES + lane].set(1.0).astype(BF16)
    in_specs = [pl.BlockSpec((1, 1, tm * TOP_K), lambda i, lo, hi: (jnp.minimum(i, n_tiles - 1), 0, 0),
                             memory_space=pltpu.SMEM)]
    first = 0
    for n in part_tiles:
        in_specs.append(pl.BlockSpec(
            (tm * SUBLANES, LANES), lambda i, lo, hi, first=first, n=n: (jnp.clip(i - first, 0, n - 1), 0)))
        first += n
    ws = WEIGHT_SPLIT
    assert f2 % (ws * PAIR_TILE) == 0 and (f2 // 2) % (ws * SUBLANES * 2) == 0
    share = lambda i: jnp.minimum(i, e * ws - 1)
    in_specs += [pl.BlockSpec((PAIR_TILE, PAIR_TILE), lambda i, lo, hi: (0, 0)),
                 pl.BlockSpec((1, 1, d, f2 // ws), lambda i, lo, hi: (layer, share(i) // ws, 0, share(i) % ws)),
                 pl.BlockSpec((1, 1, f2 // 2 // ws, d),
                              lambda i, lo, hi: (layer, share(i) // ws, share(i) % ws, 0))]
    grid_spec = pltpu.PrefetchScalarGridSpec(
        num_scalar_prefetch=2,
        grid=(max(n_tiles, e * ws),),
        in_specs=in_specs,
        out_specs=[pl.BlockSpec(memory_space=pl.ANY),
                   pl.BlockSpec((1, d, f2 // ws), lambda i, lo, hi: (share(i) // ws, 0, share(i) % ws)),
                   pl.BlockSpec((1, f2 // 2 // ws, d), lambda i, lo, hi: (share(i) // ws, share(i) % ws, 0))],
        scratch_shapes=[pltpu.VMEM((SUBLANES, LANES), F32),
                        pltpu.VMEM((tm * SUBLANES, LANES), F32), pltpu.VMEM((tm * SUBLANES, LANES), F32),
                        pltpu.SemaphoreType.DMA((2,)), pltpu.SemaphoreType.DMA],
    )
    return pl.pallas_call(
        functools.partial(_dispatch_kernel, tm=tm, part_tiles=part_tiles, n_experts=e),
        out_shape=[jax.ShapeDtypeStruct((cap * SUBLANES, LANES), F32),
                   jax.ShapeDtypeStruct(w_in.shape[1:], BF16), jax.ShapeDtypeStruct(w_out.shape[1:], BF16)],
        grid_spec=grid_spec,
        compiler_params=_params("arbitrary"),
        name="dispatch",
    )(pad_lo, pad_hi, dest.reshape(n_tiles, 1, tm * TOP_K), *token_parts, perm, w_in, w_out)


EXPERT_BLOCK = 512


def _expert_kernel(be_ref, nb_ref, x_ref, w1_ref, b1_ref, w2_ref, b2_ref, o_ref, *, bm):
    del be_ref
    b = pl.program_id(0)
    n_j = w1_ref.shape[1] // LANES

    @pl.when(b < nb_ref[0])
    def _():
        x = jnp.concatenate([x_ref[pl.ds(j, bm, stride=SUBLANES), :] for j in range(n_j)], axis=1)
        u = _dot(x.astype(BF16), w1_ref[0]) + b1_ref[0]
        acts = []
        for j in range(u.shape[1] // PAIR_TILE):
            glu = jnp.minimum(u[:, j * PAIR_TILE:j * PAIR_TILE + LANES], SWIGLU_LIMIT)
            lin = jnp.clip(u[:, j * PAIR_TILE + LANES:(j + 1) * PAIR_TILE], -SWIGLU_LIMIT, SWIGLU_LIMIT)
            acts.append(glu * jax.nn.sigmoid(SWIGLU_ALPHA * glu) * (lin + 1.0))
        a = jnp.concatenate(acts, axis=1)
        y = _dot(a.astype(BF16), w2_ref[0]) + b2_ref[0]
        for j in range(y.shape[1] // LANES):
            o_ref[pl.ds(j, bm, stride=SUBLANES), :] = y[:, j * LANES:(j + 1) * LANES]

    @pl.when(b >= nb_ref[0])
    def _():
        o_ref[...] = jnp.zeros(o_ref.shape, o_ref.dtype)


def _expert_mlp(xs, blk_expert, n_used, w1, b1, w2, b2):
    bm = EXPERT_BLOCK
    nblk = blk_expert.shape[0]
    _, d, f2 = w1.shape
    grid_spec = pltpu.PrefetchScalarGridSpec(
        num_scalar_prefetch=2,
        grid=(nblk,),
        in_specs=[
            pl.BlockSpec((bm * SUBLANES, LANES), lambda b, be, nb: (b, 0)),
            pl.BlockSpec((1, d, f2), lambda b, be, nb: (be[b], 0, 0)),
            pl.BlockSpec((1, 1, f2), lambda b, be, nb: (be[b], 0, 0)),
            pl.BlockSpec((1, f2 // 2, d), lambda b, be, nb: (be[b], 0, 0)),
            pl.BlockSpec((1, 1, d), lambda b, be, nb: (be[b], 0, 0)),
        ],
        out_specs=pl.BlockSpec((bm * SUBLANES, LANES), lambda b, be, nb: (b, 0)),
    )
    return pl.pallas_call(
        functools.partial(_expert_kernel, bm=bm),
        out_shape=jax.ShapeDtypeStruct(xs.shape, F32),
        grid_spec=grid_spec,
        compiler_params=_params("arbitrary"),
        name="expert_mlp",
    )(blk_expert, n_used, xs, w1, b1, w2, b2)


def _combine_kernel(*refs, tm, final):
    if final:
        dest_ref, next_ref, x_ref, ys_ref, gate_ref, g2_ref, fg_ref, o_ref, yb0, yb1, sems = refs
    else:
        dest_ref, next_ref, x_ref, ys_ref, gate_ref, g2_ref, o_ref, yb0, yb1, sems = refs
    i = pl.program_id(0)
    ybufs = (yb0, yb1)

    def start_gather(d_ref, slot):
        def body(t, carry):
            for u in range(2):
                r = t * 2 + u
                for k in range(TOP_K):
                    pltpu.make_async_copy(_row(ys_ref, d_ref[0, 0, r * TOP_K + k]),
                                          _row(ybufs[slot], k * tm + r), sems.at[slot]).start(priority=k % 2)
            return carry

        lax.fori_loop(0, tm // 2, body, 0)

    def finish(slot):
        ybuf = ybufs[slot]
        pltpu.make_async_copy(ybuf, ybuf, sems.at[slot]).wait()
        gate = gate_ref[...]
        parts = []
        for j in range(x_ref.shape[1] // LANES):
            f = gate[:, 0:1] * ybuf[pl.ds(j, tm, stride=SUBLANES), :]
            for k in range(1, TOP_K):
                f = f + gate[:, k:k + 1] * ybuf[pl.ds(k * tm * SUBLANES + j, tm, stride=SUBLANES), :]
            parts.append(f)
        x = x_ref[...] + g2_ref[0] * jnp.concatenate(parts, axis=1)
        if final:
            x = _rms(x, fg_ref[...])
        o_ref[...] = x

    @pl.when(i == 0)
    def _():
        start_gather(dest_ref, 0)

    for slot in range(2):
        @pl.when(i % 2 == slot)
        def _(slot=slot):
            @pl.when(i + 1 < pl.num_programs(0))
            def _():
                start_gather(next_ref, 1 - slot)

            finish(slot)


def _combine(x, ys, dest, gates, g2, final_g, *, row_off, rows_per_batch, fixed_mod):
    t, d = x.shape
    tm = _row_tile(min(t, rows_per_batch), 512)
    midx = _mod_index(rows_per_batch, tm, fixed_mod)
    final = final_g is not None
    off = row_off // tm
    t_all = dest.shape[0]
    n_steps = t // tm
    dest_tiles = dest.reshape(t_all // tm, 1, tm * TOP_K)
    in_specs = [pl.BlockSpec((1, 1, tm * TOP_K), lambda i: (off + i, 0, 0), memory_space=pltpu.SMEM),
                pl.BlockSpec((1, 1, tm * TOP_K), lambda i: (off + jnp.minimum(i + 1, n_steps - 1), 0, 0),
                             memory_space=pltpu.SMEM),
                pl.BlockSpec((tm, d), lambda i: (i, 0)),
                pl.BlockSpec(memory_space=pl.ANY),
                pl.BlockSpec((tm, SUBLANES), lambda i: (off + i, 0)),
                pl.BlockSpec((1, 1, d), midx)]
    args = [dest_tiles, dest_tiles, x, ys, gates, g2]
    if final:
        in_specs.append(pl.BlockSpec(final_g.shape, lambda i: (0, 0)))
        args.append(final_g)
    return pl.pallas_call(
        functools.partial(_combine_kernel, tm=tm, final=final),
        out_shape=jax.ShapeDtypeStruct((t, d), F32),
        grid=(n_steps,),
        in_specs=in_specs,
        out_specs=pl.BlockSpec((tm, d), lambda i: (i, 0)),
        scratch_shapes=[pltpu.VMEM((TOP_K * tm * SUBLANES, LANES), F32),
                        pltpu.VMEM((TOP_K * tm * SUBLANES, LANES), F32),
                        pltpu.SemaphoreType.DMA((2,))],
        compiler_params=_params("arbitrary"),
        name="combine_final" if final else "combine",
    )(*args)


def _pad_heads(w, n_heads, axis):
    shape = list(w.shape)
    hd = shape[axis] // n_heads
    w = w.reshape(shape[:axis] + [n_heads, hd] + shape[axis + 1:])
    pad = [(0, 0)] * w.ndim
    pad[axis + 1] = (0, LANES - hd)
    w = jnp.pad(w, pad)
    shape[axis] = n_heads * LANES
    return w.reshape(shape)


def _rope_tables(s, rot_dim, lo):
    pos = jnp.arange(s, dtype=jnp.int32)
    rows, cols = (pos // GRID_W).astype(F32), (pos % GRID_W).astype(F32)
    quarter = rot_dim // 4
    inv = ROPE_THETA ** (-jnp.arange(quarter, dtype=F32) / quarter)
    ang = jnp.concatenate([rows[:, None] * inv, cols[:, None] * inv], axis=-1)
    cos, sin = jnp.cos(ang), jnp.sin(ang)
    hi = LANES - lo - rot_dim
    cos_t = jnp.concatenate([jnp.ones((s, lo), F32), cos, cos, jnp.ones((s, hi), F32)], axis=1)
    sin_t = jnp.concatenate([jnp.zeros((s, lo), F32), -sin, sin, jnp.zeros((s, hi), F32)], axis=1)
    return cos_t, sin_t


def _moe_plan(eidx, bm):
    t = eidx.shape[0]
    ids = jnp.arange(N_EXPERTS, dtype=jnp.int32)
    sel = eidx[:, :, None] == ids
    hit = sel.astype(jnp.int32).sum(1)
    chunk = _row_tile(t, 256)
    tri = jnp.tril(jnp.ones((chunk, chunk), F32))
    within = jnp.einsum("ij,cjk->cik", tri, hit.astype(F32).reshape(t // chunk, chunk, N_EXPERTS),
                        precision=HIGHEST)
    totals = within[:, -1, :]
    incl = (within + (jnp.cumsum(totals, axis=0) - totals)[:, None, :]).reshape(t, N_EXPERTS)
    incl = incl.astype(jnp.int32)
    counts = incl[-1]
    rank = jnp.where(sel, (incl - hit)[:, None, :], 0).sum(-1)
    padded = (counts + bm - 1) // bm * bm
    pends = jnp.cumsum(padded)
    pstarts = pends - padded
    dest = jnp.where(sel, pstarts, 0).sum(-1) + rank
    n_assign = t * TOP_K
    nblk = -(-(n_assign + N_EXPERTS * (bm - 1)) // bm)
    blk_start = jnp.arange(nblk, dtype=jnp.int32) * bm
    blk_expert = jnp.minimum((blk_start[:, None] >= pends[None, :]).sum(1), N_EXPERTS - 1)
    n_used = (pends[-1] // bm).astype(jnp.int32).reshape(1)
    pad_lo = jnp.concatenate([pstarts + counts, pends[-1:]]).astype(jnp.int32)
    pad_hi = jnp.concatenate([pends, jnp.full((1,), nblk * bm, pends.dtype)]).astype(jnp.int32)
    return dest.astype(jnp.int32), blk_expert.astype(jnp.int32), n_used, pad_lo, pad_hi


def _moe(token_parts, eidx, w_in, b1, w_out, b2, layer):
    dest, blk_expert, n_used, pad_lo, pad_hi = _moe_plan(eidx, EXPERT_BLOCK)
    xs, w1, w2 = _dispatch(token_parts, dest, pad_lo, pad_hi, blk_expert.shape[0] * EXPERT_BLOCK,
                           w_in, w_out, layer)
    return _expert_mlp(xs, blk_expert, n_used, w1, b1, w2, b2), dest


def kernel(x, c, ctx, c_ctx, ada_w, ada_b, norm_mix_g, norm_ffn_g, ab_w_in, mla_q_norm_g, mla_wq_b,
           mla_kv_norm_g, mla_wkv_b, swa_sink, ab_w_out, c_w_in, c_q_norm_g, c_k_norm_g, c_w_out,
           router_w, router_b, moe_w_in, moe_b_in, moe_w_out, moe_b_out, final_norm_g):
    bn, s, d = x.shape
    n_ctx = ctx.shape[1]
    depth = ada_w.shape[0]
    t_lat, t_ctx = bn * s, bn * n_ctx
    row = lambda v: v.reshape(1, -1)

    n_mod = -(-(bn + 1) // SUBLANES) * SUBLANES
    c_all = jnp.concatenate([c, c_ctx[None], jnp.zeros((n_mod - bn - 1, d), F32)], axis=0)
    mod = _modulation(c_all, ada_w, ada_b).reshape(depth, n_mod, 6, 1, d)
    ctx_row = bn

    tab_a = _rope_tables(s, HEAD_DIM_A, 0)
    tab_b = _rope_tables(s, MLA_ROPE, MLA_NOPE)
    tab_c = _rope_tables(s, HEAD_DIM_C, 0)

    xl = x.reshape(t_lat, d)
    xc = ctx.reshape(t_ctx, d)
    for i in range(depth):
        with_ctx = i < depth - 1
        j = i // 2
        sh1, sc1, g1, sh2, sc2, g2 = [mod[i, :, m] for m in range(6)]
        gmix, gffn = row(norm_mix_g[i]), row(norm_ffn_g[i])
        if i % 2 == 0:
            w_in = ab_w_in[j]
            sizes = [N_HEADS_A * HEAD_DIM_A, N_KV_A * HEAD_DIM_A, N_KV_A * HEAD_DIM_A,
                     MLA_Q_RANK, MLA_KV_RANK, MLA_ROPE]
            offs = [0]
            for n in sizes:
                offs.append(offs[-1] + n)
            cols = [w_in[:, offs[m]:offs[m + 1]] for m in range(6)]
            krg = jnp.pad(cols[5], ((0, 0), (MLA_NOPE, LANES - MLA_NOPE - MLA_ROPE)))
            w1 = jnp.concatenate([_pad_heads(cols[0], N_HEADS_A, 1), _pad_heads(cols[1], N_KV_A, 1),
                                  _pad_heads(cols[2], N_KV_A, 1), cols[3], cols[4], krg], axis=1).astype(BF16)
            wq = _pad_heads(mla_wq_b[j], MLA_HEADS, 1).astype(BF16)
            wkv = mla_wkv_b[j].reshape(MLA_KV_RANK, MLA_HEADS, MLA_NOPE + MLA_V)
            wk = _pad_heads(wkv[:, :, :MLA_NOPE].reshape(MLA_KV_RANK, -1), MLA_HEADS, 1).astype(BF16)
            wv = _pad_heads(wkv[:, :, MLA_NOPE:].reshape(MLA_KV_RANK, -1), MLA_HEADS, 1).astype(BF16)
            ws = [w1, row(mla_q_norm_g[j]), wq, row(mla_kv_norm_g[j]), wk, wv]
            na = N_HEADS_A * HEAD_DIM_A
            wo_a = ab_w_out[j][:na].astype(BF16)
            wo_b = ab_w_out[j][na:].astype(BF16)
            sink = swa_sink[j]

            qa, ka, va, qb, kb, vb = _ab_in(xl, gmix, sc1, sh1, ws, tab_a + tab_b, s, None)
            qa_c, ka_c, va_c, qb_c, kb_c, vb_c = _ab_in(xc, gmix, sc1, sh1, ws, None, n_ctx, ctx_row)
            ga = N_HEADS_A // N_KV_A
            cfg_a = dict(batch=bn, n_kv_total=N_KV_A, n_kv=N_KV_A, n_group=ga, out_dim=HEAD_DIM_A)
            cfg_b = dict(batch=bn, n_kv_total=MLA_HEADS, n_kv=MLA_STEP_HEADS, n_group=1, out_dim=MLA_V)
            oa = _attention(qa, ka, va, ka_c, va_c, sink, window=True, tq=512, **cfg_a)
            ob = _attention(qb, kb, vb, kb_c, vb_c, None, window=False, tq=1024, **cfg_b)
            o_lat, w_o = [oa, ob], [wo_a, wo_b]
            if with_ctx:
                oa_c = _attention(qa_c, None, None, ka_c, va_c, sink, window=False, tq=256, **cfg_a)
                ob_c = _attention(qb_c, None, None, kb_c, vb_c, None, window=False, tq=256, **cfg_b)
                o_ctx = [oa_c, ob_c]
        else:
            w_in = c_w_in[j].astype(BF16)
            qn, kn = row(c_q_norm_g[j]), row(c_k_norm_g[j])
            q, k, v = _c_in(xl, gmix, sc1, sh1, w_in, qn, kn, tab_c, s, None)
            q_c, k_c, v_c = _c_in(xc, gmix, sc1, sh1, w_in, qn, kn, None, n_ctx, ctx_row)
            gc = N_HEADS_C // N_KV_C
            cfg_c = dict(batch=bn, n_kv_total=N_KV_C, n_kv=1, n_group=gc, out_dim=HEAD_DIM_C,
                         window=False, tq=1024)
            o = _attention(q, k, v, k_c, v_c, None, **cfg_c)
            o_lat, w_o = [o], [c_w_out[j].astype(BF16)]
            if with_ctx:
                o_ctx = [_attention(q_c, None, None, k_c, v_c, None, **cfg_c)]

        rw = jnp.pad(router_w[i], ((0, 0), (0, LANES - N_EXPERTS)))
        rw_hi = rw.astype(BF16)
        rw = jnp.concatenate([rw_hi, (rw - rw_hi.astype(F32)).astype(BF16)], axis=1)
        rb = jnp.concatenate([router_b[i], jnp.full((LANES - N_EXPERTS,), NEG_INF, F32)]).reshape(1, LANES)
        xl, tok_l, idx_l, gate_l = _out_proj(o_lat, w_o, xl, g1, gffn, sc2, sh2, rw, rb, s, None)
        if with_ctx:
            xc, tok_c, idx_c, gate_c = _out_proj(o_ctx, w_o, xc, g1, gffn, sc2, sh2, rw, rb, n_ctx, ctx_row)
            tokens = [tok_l, tok_c]
            eidx = jnp.concatenate([idx_l, idx_c], axis=0)
            gates = jnp.concatenate([gate_l, gate_c], axis=0)
        else:
            tokens, eidx, gates = [tok_l], idx_l, gate_l

        n_tiles = moe_b_in.shape[-1] // PAIR_TILE
        b1e = moe_b_in[i].reshape(N_EXPERTS, n_tiles, LANES, 2).transpose(0, 1, 3, 2)
        b1e = b1e.reshape(N_EXPERTS, 1, -1)
        b2e = moe_b_out[i][:, None, :]
        ys, dest = _moe(tokens, eidx[:, :TOP_K], moe_w_in, b1e, moe_w_out, b2e, i)

        last = i == depth - 1
        xl = _combine(xl, ys, dest, gates, g2, row(final_norm_g) if last else None,
                      row_off=0, rows_per_batch=s, fixed_mod=None)
        if with_ctx:
            xc = _combine(xc, ys, dest, gates, g2, None,
                          row_off=t_lat, rows_per_batch=n_ctx, fixed_mod=ctx_row)
    return xl.reshape(bn, s, d)
```

```python
import functools

import jax
import jax.numpy as jnp
from jax import lax
from jax.experimental import pallas as pl
from jax.experimental.pallas import tpu as pltpu

GRID_W = 64
N_HEADS_A, N_KV_A, HEAD_DIM_A, WINDOW = 8, 2, 64, 128
MLA_HEADS, MLA_Q_RANK, MLA_KV_RANK, MLA_NOPE, MLA_ROPE, MLA_V = 8, 384, 256, 64, 32, 64
N_HEADS_C, N_KV_C, HEAD_DIM_C = 8, 2, 128
N_EXPERTS, TOP_K = 32, 4
MLA_STEP_HEADS = 4
SWIGLU_ALPHA, SWIGLU_LIMIT = 1.702, 7.0
ROPE_THETA, RMS_EPS, NEG_INF = 10000.0, 1e-6, -1e30
LOG2_E = 1.4426950408889634

LANES = 128
SUBLANES = 8
VMEM_LIMIT_BYTES = 56 * 1024 * 1024

F32 = jnp.float32
BF16 = jnp.bfloat16
HIGHEST = lax.Precision.HIGHEST


def _params(*sem):
    return pltpu.CompilerParams(dimension_semantics=sem, vmem_limit_bytes=VMEM_LIMIT_BYTES)


def _dot(a, b):
    return jnp.dot(a, b, preferred_element_type=F32)


def _dot_t(a, b):
    return lax.dot_general(a, b, (((1,), (1,)), ((), ())), preferred_element_type=F32)


def _rms(x, g):
    return x * lax.rsqrt(jnp.mean(x * x, axis=-1, keepdims=True) + RMS_EPS) * g


def _modnorm(x, g, sc, sh):
    return _rms(x, g) * (1.0 + sc) + sh


def _rope_group(x, cos, sin, half, lo):
    lane = lax.broadcasted_iota(jnp.int32, x.shape, 1)
    first = (lane >= lo) & (lane < lo + half)
    rot = jnp.where(first, pltpu.roll(x, LANES - half, 1), pltpu.roll(x, half, 1))
    return x * cos + rot * sin


def _groups(x):
    return [x[:, i * LANES:(i + 1) * LANES] for i in range(x.shape[1] // LANES)]


def _mod_kernel(c_ref, w_ref, b_ref, o_ref):
    c = c_ref[...]
    a = c * jax.nn.sigmoid(c)
    o_ref[0] = jnp.dot(a, w_ref[0], preferred_element_type=F32, precision=HIGHEST) + b_ref[0]


def _modulation(c_all, ada_w, ada_b):
    depth, d, n = ada_w.shape
    r = c_all.shape[0]
    nt = n // 4
    return pl.pallas_call(
        _mod_kernel,
        out_shape=jax.ShapeDtypeStruct((depth, r, n), F32),
        grid=(depth, n // nt),
        in_specs=[
            pl.BlockSpec((r, d), lambda i, j: (0, 0)),
            pl.BlockSpec((1, d, nt), lambda i, j: (i, 0, j)),
            pl.BlockSpec((1, 1, nt), lambda i, j: (i, 0, j)),
        ],
        out_specs=pl.BlockSpec((1, r, nt), lambda i, j: (i, 0, j)),
        compiler_params=_params("arbitrary", "arbitrary"),
        name="modulation",
    )(c_all, ada_w, ada_b.reshape(depth, 1, n))


def _ab_in_kernel(*refs, rope):
    if rope:
        (x_ref, g_ref, sc_ref, sh_ref, w1_ref, qg_ref, wq_ref, kvg_ref, wk_ref, wv_ref,
         ca_ref, sa_ref, cb_ref, sb_ref, qa_ref, ka_ref, va_ref, qb_ref, kb_ref, vb_ref) = refs
    else:
        (x_ref, g_ref, sc_ref, sh_ref, w1_ref, qg_ref, wq_ref, kvg_ref, wk_ref, wv_ref,
         qa_ref, ka_ref, va_ref, qb_ref, kb_ref, vb_ref) = refs
    h = _modnorm(x_ref[...], g_ref[...], sc_ref[0], sh_ref[0]).astype(BF16)
    p = _dot(h, w1_ref[...])
    nqa = N_HEADS_A * LANES
    nka = N_KV_A * LANES
    o = 0
    qa = p[:, o:o + nqa]; o += nqa
    ka = p[:, o:o + nka]; o += nka
    va = p[:, o:o + nka]; o += nka
    cq = p[:, o:o + MLA_Q_RANK]; o += MLA_Q_RANK
    ckv = p[:, o:o + MLA_KV_RANK]; o += MLA_KV_RANK
    krg = p[:, o:o + LANES]

    qb = _dot(_rms(cq, qg_ref[...]).astype(BF16), wq_ref[...])
    ckv_n = _rms(ckv, kvg_ref[...]).astype(BF16)
    kb = _dot(ckv_n, wk_ref[...])
    vb = _dot(ckv_n, wv_ref[...])

    scale_a = HEAD_DIM_A ** -0.5 * LOG2_E
    scale_b = (MLA_NOPE + MLA_ROPE) ** -0.5 * LOG2_E
    if rope:
        ca, sa, cb, sb = ca_ref[...], sa_ref[...], cb_ref[...], sb_ref[...]
        half_a, half_b = HEAD_DIM_A // 2, MLA_ROPE // 2
        qa_g = [_rope_group(t, ca, sa, half_a, 0) * scale_a for t in _groups(qa)]
        ka_g = [_rope_group(t, ca, sa, half_a, 0) for t in _groups(ka)]
        qb_g = [_rope_group(t, cb, sb, half_b, MLA_NOPE) * scale_b for t in _groups(qb)]
        krg = _rope_group(krg, cb, sb, half_b, MLA_NOPE)
    else:
        qa_g = [t * scale_a for t in _groups(qa)]
        ka_g = _groups(ka)
        qb_g = [t * scale_b for t in _groups(qb)]
    kb_g = [t + krg for t in _groups(kb)]
    qa_ref[...] = jnp.concatenate(qa_g, axis=1).astype(BF16)
    ka_ref[...] = jnp.concatenate(ka_g, axis=1).astype(BF16)
    va_ref[...] = va.astype(BF16)
    qb_ref[...] = jnp.concatenate(qb_g, axis=1).astype(BF16)
    kb_ref[...] = jnp.concatenate(kb_g, axis=1).astype(BF16)
    vb_ref[...] = vb.astype(BF16)


def _row_tile(t, pref=512, also=()):
    tm = pref
    while any(n % tm for n in (t, *also)):
        tm //= 2
    return tm


def _stream_tile(t, rows_per_batch, fixed_mod, pref=512, also=()):
    return _row_tile(t if fixed_mod is not None else min(t, rows_per_batch), pref, also)


def _mod_index(rows_per_batch, tm, fixed):
    if fixed is not None:
        return lambda i: (fixed, 0, 0)
    return lambda i: ((i * tm) // rows_per_batch, 0, 0)


def _ab_in(x, g, sc, sh, w, tables, rows_per_batch, fixed_mod):
    t, d = x.shape
    tm = _stream_tile(t, rows_per_batch, fixed_mod)
    rope = tables is not None
    midx = _mod_index(rows_per_batch, tm, fixed_mod)
    full = lambda a: pl.BlockSpec(a.shape, lambda i: (0,) * a.ndim)
    in_specs = [
        pl.BlockSpec((tm, d), lambda i: (i, 0)),
        full(g),
        pl.BlockSpec((1, 1, d), midx),
        pl.BlockSpec((1, 1, d), midx),
    ] + [full(a) for a in w]
    args = [x, g, sc, sh] + list(w)
    if rope:
        nt = rows_per_batch // tm
        for tab in tables:
            in_specs.append(pl.BlockSpec((tm, LANES), lambda i: (i % nt, 0)))
            args.append(tab)
    widths = [N_HEADS_A * LANES, N_KV_A * LANES, N_KV_A * LANES,
              MLA_HEADS * LANES, MLA_HEADS * LANES, MLA_HEADS * LANES]
    return pl.pallas_call(
        functools.partial(_ab_in_kernel, rope=rope),
        out_shape=[jax.ShapeDtypeStruct((t, n), BF16) for n in widths],
        grid=(t // tm,),
        in_specs=in_specs,
        out_specs=[pl.BlockSpec((tm, n), lambda i: (i, 0)) for n in widths],
        compiler_params=_params("arbitrary"),
        name="ab_in_rope" if rope else "ab_in",
    )(*args)


def _c_in_kernel(*refs, rope):
    if rope:
        x_ref, g_ref, sc_ref, sh_ref, w_ref, qn_ref, kn_ref, cc_ref, sc2_ref, q_ref, k_ref, v_ref = refs
    else:
        x_ref, g_ref, sc_ref, sh_ref, w_ref, qn_ref, kn_ref, q_ref, k_ref, v_ref = refs
    h = _modnorm(x_ref[...], g_ref[...], sc_ref[0], sh_ref[0]).astype(BF16)
    p = _dot(h, w_ref[...])
    nq = N_HEADS_C * HEAD_DIM_C
    nk = N_KV_C * HEAD_DIM_C
    q_g = [_rms(t, qn_ref[...]) for t in _groups(p[:, :nq])]
    k_g = [_rms(t, kn_ref[...]) for t in _groups(p[:, nq:nq + nk])]
    scale = HEAD_DIM_C ** -0.5 * LOG2_E
    if rope:
        cc, ss = cc_ref[...], sc2_ref[...]
        q_g = [_rope_group(t, cc, ss, HEAD_DIM_C // 2, 0) for t in q_g]
        k_g = [_rope_group(t, cc, ss, HEAD_DIM_C // 2, 0) for t in k_g]
    q_ref[...] = jnp.concatenate([t * scale for t in q_g], axis=1).astype(BF16)
    k_ref[...] = jnp.concatenate(k_g, axis=1).astype(BF16)
    v_ref[...] = p[:, nq + nk:].astype(BF16)


def _c_in(x, g, sc, sh, w, qn, kn, tables, rows_per_batch, fixed_mod):
    t, d = x.shape
    tm = _stream_tile(t, rows_per_batch, fixed_mod)
    rope = tables is not None
    midx = _mod_index(rows_per_batch, tm, fixed_mod)
    full = lambda a: pl.BlockSpec(a.shape, lambda i: (0,) * a.ndim)
    in_specs = [
        pl.BlockSpec((tm, d), lambda i: (i, 0)),
        full(g),
        pl.BlockSpec((1, 1, d), midx),
        pl.BlockSpec((1, 1, d), midx),
        full(w), full(qn), full(kn),
    ]
    args = [x, g, sc, sh, w, qn, kn]
    if rope:
        nt = rows_per_batch // tm
        for tab in tables:
            in_specs.append(pl.BlockSpec((tm, LANES), lambda i: (i % nt, 0)))
            args.append(tab)
    widths = [N_HEADS_C * HEAD_DIM_C, N_KV_C * HEAD_DIM_C, N_KV_C * HEAD_DIM_C]
    return pl.pallas_call(
        functools.partial(_c_in_kernel, rope=rope),
        out_shape=[jax.ShapeDtypeStruct((t, n), BF16) for n in widths],
        grid=(t // tm,),
        in_specs=in_specs,
        out_specs=[pl.BlockSpec((tm, n), lambda i: (i, 0)) for n in widths],
        compiler_params=_params("arbitrary"),
        name="c_in_rope" if rope else "c_in",
    )(*args)


def _attn_kernel(*refs, n_kv, n_group, out_dim, has_lat, has_sink, window, tq, s_lat, kw):
    refs = list(refs)
    sink_ref = refs.pop(0) if has_sink else None
    q_ref = refs.pop(0)
    if has_lat:
        k1_ref, v1_ref = refs.pop(0), refs.pop(0)
    k2_ref, v2_ref, o_ref = refs
    hk0 = pl.program_id(1) * n_kv
    qi = pl.program_id(2)
    if window:
        ws = pl.multiple_of(jnp.clip(qi * tq - WINDOW, 0, s_lat - kw), LANES)
        rows = pl.ds(ws, kw)
        qpos = qi * tq + lax.broadcasted_iota(jnp.int32, (tq, kw), 0)
        kpos = ws + lax.broadcasted_iota(jnp.int32, (tq, kw), 1)
        ok = jnp.abs(kpos - qpos) <= WINDOW
    else:
        rows = slice(None)
    outs = []
    for j in range(n_kv):
        lanes = slice(j * LANES, (j + 1) * LANES)
        k2, v2 = k2_ref[:, lanes], v2_ref[:, lanes]
        if has_lat:
            k1, v1 = k1_ref[rows, lanes], v1_ref[rows, lanes]
        for g in range(n_group):
            h = j * n_group + g
            q = q_ref[:, h * LANES:(h + 1) * LANES]
            s2 = _dot_t(q, k2)
            m = jnp.max(s2, axis=-1, keepdims=True)
            if has_lat:
                s1 = _dot_t(q, k1)
                if window:
                    s1 = jnp.where(ok, s1, NEG_INF)
                m = jnp.maximum(m, jnp.max(s1, axis=-1, keepdims=True))
            if has_sink:
                sk = sink_ref[hk0 * n_group + h] * LOG2_E
                m = jnp.maximum(m, sk)
            e2 = jnp.exp2(s2 - m)
            den = jnp.sum(e2, axis=-1, keepdims=True)
            acc = _dot(e2.astype(BF16), v2)
            if has_lat:
                e1 = jnp.exp2(s1 - m)
                den = den + jnp.sum(e1, axis=-1, keepdims=True)
                acc = acc + _dot(e1.astype(BF16), v1)
            if has_sink:
                den = den + jnp.exp2(sk - m)
            outs.append(acc / den)
    if out_dim == LANES:
        for h, o in enumerate(outs):
            o_ref[:, h * LANES:(h + 1) * LANES] = o.astype(BF16)
    else:
        lane = lax.broadcasted_iota(jnp.int32, (tq, LANES), 1)
        for p in range(len(outs) // 2):
            pair = jnp.where(lane < out_dim, outs[2 * p], pltpu.roll(outs[2 * p + 1], out_dim, 1))
            o_ref[:, p * LANES:(p + 1) * LANES] = pair.astype(BF16)


def _attention(q, k_lat, v_lat, k_ctx, v_ctx, sink, *, batch, n_kv_total, n_kv, n_group, out_dim,
               window, tq):
    tq_total = q.shape[0] // batch
    tq = min(tq, tq_total)
    n_ctx = k_ctx.shape[0] // batch
    has_lat = k_lat is not None
    has_sink = sink is not None
    s_lat = k_lat.shape[0] // batch if has_lat else 0
    kw = min(tq + 2 * WINDOW, s_lat) if window else 0
    nq = tq_total // tq
    n_heads = n_kv * n_group
    assert out_dim == LANES or (2 * out_dim == LANES and n_heads % 2 == 0)
    in_specs, args = [], []
    if has_sink:
        in_specs.append(pl.BlockSpec(memory_space=pltpu.SMEM))
        args.append(sink)
    in_specs.append(pl.BlockSpec((tq, n_heads * LANES), lambda b, h, i: (b * nq + i, h)))
    args.append(q)
    if has_lat:
        in_specs += [pl.BlockSpec((s_lat, n_kv * LANES), lambda b, h, i: (b, h))] * 2
        args += [k_lat, v_lat]
    in_specs += [pl.BlockSpec((n_ctx, n_kv * LANES), lambda b, h, i: (b, h))] * 2
    args += [k_ctx, v_ctx]
    n_steps = n_kv_total // n_kv
    return pl.pallas_call(
        functools.partial(_attn_kernel, n_kv=n_kv, n_group=n_group, out_dim=out_dim, has_lat=has_lat,
                          has_sink=has_sink, window=window, tq=tq, s_lat=s_lat, kw=kw),
        out_shape=jax.ShapeDtypeStruct((q.shape[0], n_steps * n_heads * out_dim), BF16),
        grid=(batch, n_steps, nq),
        in_specs=in_specs,
        out_specs=pl.BlockSpec((tq, n_heads * out_dim), lambda b, h, i: (b * nq + i, h)),
        compiler_params=_params("arbitrary", "arbitrary", "arbitrary"),
        name="attn_win" if window else ("attn_dense" if has_lat else "attn_ctx"),
    )(*args)


def _out_kernel(*refs, n_parts, tm):
    o_refs = refs[:n_parts]
    w_refs = refs[n_parts:2 * n_parts]
    (x_ref, g1_ref, gn_ref, sc_ref, sh_ref, rw_ref, rb_ref,
     xo_ref, tok_ref, idx_ref, gate_ref) = refs[2 * n_parts:]
    y = _dot(o_refs[0][...], w_refs[0][...])
    for o_r, w_r in zip(o_refs[1:], w_refs[1:]):
        y = y + _dot(o_r[...], w_r[...])
    x = x_ref[...] + g1_ref[0] * y
    xo_ref[...] = x
    tok = _modnorm(x, gn_ref[...], sc_ref[0], sh_ref[0])
    for j in range(tok.shape[1] // LANES):
        tok_ref[pl.ds(j, tm, stride=SUBLANES), :] = tok[:, j * LANES:(j + 1) * LANES]
    tok_hi = tok.astype(BF16)
    tok_lo = (tok - tok_hi.astype(F32)).astype(BF16)
    hh_hl = _dot(tok_hi, rw_ref[...])
    logits = (hh_hl[:, :LANES] + hh_hl[:, LANES:] + _dot(tok_lo, rw_ref[:, :LANES])) + rb_ref[...]
    lane = lax.broadcasted_iota(jnp.int32, logits.shape, 1).astype(F32)
    vals, idxs = [], []
    l = logits
    for _ in range(TOP_K):
        m = jnp.max(l, axis=-1, keepdims=True)
        idx = jnp.min(jnp.where(l == m, lane, float(LANES)), axis=-1, keepdims=True)
        vals.append(m)
        idxs.append(idx)
        l = jnp.where(lane == idx, -jnp.inf, l)
    es = [jnp.exp(v - vals[0]) for v in vals]
    den = es[0]
    for e in es[1:]:
        den = den + e
    lane8 = lax.broadcasted_iota(jnp.int32, (tm, SUBLANES), 1)
    io = jnp.zeros((tm, SUBLANES), jnp.int32)
    go = jnp.zeros((tm, SUBLANES), F32)
    for k in range(TOP_K):
        io = jnp.where(lane8 == k, idxs[k].astype(jnp.int32), io)
        go = jnp.where(lane8 == k, es[k] / den, go)
    idx_ref[...] = io
    gate_ref[...] = go


def _out_proj(o_parts, w_parts, x, g1, gn, sc, sh, rw, rb, rows_per_batch, fixed_mod):
    t, d = x.shape
    tm = _stream_tile(t, rows_per_batch, fixed_mod)
    midx = _mod_index(rows_per_batch, tm, fixed_mod)
    full = lambda a: pl.BlockSpec(a.shape, lambda i: (0,) * a.ndim)
    n_parts = len(o_parts)
    in_specs = ([pl.BlockSpec((tm, o.shape[1]), lambda i: (i, 0)) for o in o_parts]
                + [full(w) for w in w_parts]
                + [pl.BlockSpec((tm, d), lambda i: (i, 0)),
                   pl.BlockSpec((1, 1, d), midx), full(gn),
                   pl.BlockSpec((1, 1, d), midx), pl.BlockSpec((1, 1, d), midx),
                   full(rw), full(rb)])
    return pl.pallas_call(
        functools.partial(_out_kernel, n_parts=n_parts, tm=tm),
        out_shape=[jax.ShapeDtypeStruct((t, d), F32),
                   jax.ShapeDtypeStruct((t * SUBLANES, d // SUBLANES), F32),
                   jax.ShapeDtypeStruct((t, SUBLANES), jnp.int32),
                   jax.ShapeDtypeStruct((t, SUBLANES), F32)],
        grid=(t // tm,),
        in_specs=in_specs,
        out_specs=[pl.BlockSpec((tm, d), lambda i: (i, 0)),
                   pl.BlockSpec((tm * SUBLANES, d // SUBLANES), lambda i: (i, 0)),
                   pl.BlockSpec((tm, SUBLANES), lambda i: (i, 0)),
                   pl.BlockSpec((tm, SUBLANES), lambda i: (i, 0))],
        compiler_params=_params("arbitrary"),
        name="out_proj",
    )(*o_parts, *w_parts, x, g1, gn, sc, sh, rw, rb)


DISPATCH_TILE = 512
PAIR_TILE = 2 * LANES
WEIGHT_SPLIT = 2


def _row(ref, r):
    return ref.at[pl.ds(pl.multiple_of(r * SUBLANES, SUBLANES), SUBLANES)]


def _dispatch_kernel(*refs, tm, part_tiles, n_experts):
    n_parts = len(part_tiles)
    plo_ref, phi_ref, dest_ref = refs[:3]
    tok_refs = refs[3:3 + n_parts]
    p_ref, w1_ref, w2_ref, xs_ref, o1_ref, o2_ref, zero_ref, st0, st1, sems, zsem = refs[3 + n_parts:]
    i = pl.program_id(0)
    n_tiles = sum(part_tiles)
    stage = (st0, st1)

    @pl.when(i == 0)
    def _():
        zero_ref[...] = jnp.zeros(zero_ref.shape, zero_ref.dtype)

        def fill_range(e, carry):
            lo, hi = plo_ref[e], phi_ref[e]

            def fill_chunk(c, carry):
                c_lo = lo + c * tm
                c_hi = jnp.minimum(c_lo + tm, hi)

                def start_row(r, carry):
                    pltpu.make_async_copy(zero_ref, _row(xs_ref, r), zsem).start()
                    return carry

                def wait_row(r, carry):
                    pltpu.make_async_copy(zero_ref, _row(xs_ref, r), zsem).wait()
                    return carry

                lax.fori_loop(c_lo, c_hi, start_row, carry)
                return lax.fori_loop(c_lo, c_hi, wait_row, carry)

            return lax.fori_loop(0, (hi - lo + tm - 1) // tm, fill_chunk, carry)

        lax.fori_loop(0, plo_ref.shape[0], fill_range, 0)

    def start_rows(tok_ref, slot):
        stage[slot][...] = tok_ref[...]

        def body(t, carry):
            for u in range(2):
                r = t * 2 + u
                src = _row(stage[slot], r)
                for k in range(TOP_K):
                    pltpu.make_async_copy(src, _row(xs_ref, dest_ref[0, 0, r * TOP_K + k]), sems.at[slot]).start(
                        priority=k % 2)
            return carry

        lax.fori_loop(0, tm // 2, body, 0)

    def wait_rows(slot):
        done = xs_ref.at[pl.ds(0, tm * TOP_K * SUBLANES)]
        pltpu.make_async_copy(done, done, sems.at[slot]).wait()

    first = 0
    for tok_ref, n in zip(tok_refs, part_tiles):
        for slot in range(2):
            pl.when((i >= first) & (i < first + n) & (i % 2 == slot))(
                functools.partial(start_rows, tok_ref, slot))
        first += n

    @pl.when(i < n_experts * WEIGHT_SPLIT)
    def _():
        perm = p_ref[...]
        for j in range(w1_ref.shape[3] // PAIR_TILE):
            cols = slice(j * PAIR_TILE, (j + 1) * PAIR_TILE)
            o1_ref[0, :, cols] = _dot(w1_ref[0, 0, :, cols].astype(BF16), perm).astype(BF16)
        o2_ref[0] = w2_ref[0, 0].astype(BF16)

    for slot in range(2):
        pl.when((i >= 1) & (i <= n_tiles) & ((i - 1) % 2 == slot))(functools.partial(wait_rows, slot))
        pl.when((i == pl.num_programs(0) - 1) & (i < n_tiles) & (i % 2 == slot))(
            functools.partial(wait_rows, slot))


def _dispatch(token_parts, dest, pad_lo, pad_hi, cap, w_in, w_out, layer):
    t = dest.shape[0]
    tm = _row_tile(min(p.shape[0] // SUBLANES for p in token_parts), DISPATCH_TILE)
    part_tiles = tuple(p.shape[0] // SUBLANES // tm for p in token_parts)
    n_tiles = sum(part_tiles)
    assert n_tiles * tm == t
    _, e, d, f2 = w_in.shape
    lane = jnp.arange(LANES)
    perm = jnp.zeros((PAIR_TILE, PAIR_TILE), F32).at[2 * lane, lane].set(1.0)
    perm = perm.at[2 * lane + 1, LANES + lane].set(1.0).astype(BF16)
    in_specs = [pl.BlockSpec((1, 1, tm * TOP_K), lambda i, lo, hi: (jnp.minimum(i, n_tiles - 1), 0, 0),
                             memory_space=pltpu.SMEM)]
    first = 0
    for n in part_tiles:
        in_specs.append(pl.BlockSpec(
            (tm * SUBLANES, LANES), lambda i, lo, hi, first=first, n=n: (jnp.clip(i - first, 0, n - 1), 0)))
        first += n
    ws = WEIGHT_SPLIT
    assert f2 % (ws * PAIR_TILE) == 0 and (f2 // 2) % (ws * SUBLANES * 2) == 0
    share = lambda i: jnp.minimum(i, e * ws - 1)
    in_specs += [pl.BlockSpec((PAIR_TILE, PAIR_TILE), lambda i, lo, hi: (0, 0)),
                 pl.BlockSpec((1, 1, d, f2 // ws), lambda i, lo, hi: (layer, share(i) // ws, 0, share(i) % ws)),
                 pl.BlockSpec((1, 1, f2 // 2 // ws, d),
                              lambda i, lo, hi: (layer, share(i) // ws, share(i) % ws, 0))]
    grid_spec = pltpu.PrefetchScalarGridSpec(
        num_scalar_prefetch=2,
        grid=(max(n_tiles, e * ws),),
        in_specs=in_specs,
        out_specs=[pl.BlockSpec(memory_space=pl.ANY),
                   pl.BlockSpec((1, d, f2 // ws), lambda i, lo, hi: (share(i) // ws, 0, share(i) % ws)),
                   pl.BlockSpec((1, f2 // 2 // ws, d), lambda i, lo, hi: (share(i) // ws, share(i) % ws, 0))],
        scratch_shapes=[pltpu.VMEM((SUBLANES, LANES), F32),
                        pltpu.VMEM((tm * SUBLANES, LANES), F32), pltpu.VMEM((tm * SUBLANES, LANES), F32),
                        pltpu.SemaphoreType.DMA((2,)), pltpu.SemaphoreType.DMA],
    )
    return pl.pallas_call(
        functools.partial(_dispatch_kernel, tm=tm, part_tiles=part_tiles, n_experts=e),
        out_shape=[jax.ShapeDtypeStruct((cap * SUBLANES, LANES), F32),
                   jax.ShapeDtypeStruct(w_in.shape[1:], BF16), jax.ShapeDtypeStruct(w_out.shape[1:], BF16)],
        grid_spec=grid_spec,
        compiler_params=_params("arbitrary"),
        name="dispatch",
    )(pad_lo, pad_hi, dest.reshape(n_tiles, 1, tm * TOP_K), *token_parts, perm, w_in, w_out)


EXPERT_BLOCK = 512


def _expert_kernel(be_ref, nb_ref, x_ref, w1_ref, b1_ref, w2_ref, b2_ref, o_ref, *, bm):
    del be_ref
    b = pl.program_id(0)
    n_j = w1_ref.shape[1] // LANES

    @pl.when(b < nb_ref[0])
    def _():
        x = jnp.concatenate([x_ref[pl.ds(j, bm, stride=SUBLANES), :] for j in range(n_j)], axis=1)
        u = _dot(x.astype(BF16), w1_ref[0]) + b1_ref[0]
        acts = []
        for j in range(u.shape[1] // PAIR_TILE):
            glu = jnp.minimum(u[:, j * PAIR_TILE:j * PAIR_TILE + LANES], SWIGLU_LIMIT)
            lin = jnp.clip(u[:, j * PAIR_TILE + LANES:(j + 1) * PAIR_TILE], -SWIGLU_LIMIT, SWIGLU_LIMIT)
            acts.append(glu * jax.nn.sigmoid(SWIGLU_ALPHA * glu) * (lin + 1.0))
        a = jnp.concatenate(acts, axis=1)
        y = _dot(a.astype(BF16), w2_ref[0]) + b2_ref[0]
        for j in range(y.shape[1] // LANES):
            o_ref[pl.ds(j, bm, stride=SUBLANES), :] = y[:, j * LANES:(j + 1) * LANES]

    @pl.when(b >= nb_ref[0])
    def _():
        o_ref[...] = jnp.zeros(o_ref.shape, o_ref.dtype)


def _expert_mlp(xs, blk_expert, n_used, w1, b1, w2, b2):
    bm = EXPERT_BLOCK
    nblk = blk_expert.shape[0]
    _, d, f2 = w1.shape
    grid_spec = pltpu.PrefetchScalarGridSpec(
        num_scalar_prefetch=2,
        grid=(nblk,),
        in_specs=[
            pl.BlockSpec((bm * SUBLANES, LANES), lambda b, be, nb: (b, 0)),
            pl.BlockSpec((1, d, f2), lambda b, be, nb: (be[b], 0, 0)),
            pl.BlockSpec((1, 1, f2), lambda b, be, nb: (be[b], 0, 0)),
            pl.BlockSpec((1, f2 // 2, d), lambda b, be, nb: (be[b], 0, 0)),
            pl.BlockSpec((1, 1, d), lambda b, be, nb: (be[b], 0, 0)),
        ],
        out_specs=pl.BlockSpec((bm * SUBLANES, LANES), lambda b, be, nb: (b, 0)),
    )
    return pl.pallas_call(
        functools.partial(_expert_kernel, bm=bm),
        out_shape=jax.ShapeDtypeStruct(xs.shape, F32),
        grid_spec=grid_spec,
        compiler_params=_params("arbitrary"),
        name="expert_mlp",
    )(blk_expert, n_used, xs, w1, b1, w2, b2)


def _combine_kernel(*refs, tm, final):
    if final:
        dest_ref, next_ref, x_ref, ys_ref, gate_ref, g2_ref, fg_ref, o_ref, yb0, yb1, sems = refs
    else:
        dest_ref, next_ref, x_ref, ys_ref, gate_ref, g2_ref, o_ref, yb0, yb1, sems = refs
    i = pl.program_id(0)
    ybufs = (yb0, yb1)

    def start_gather(d_ref, slot):
        def body(t, carry):
            for u in range(2):
                r = t * 2 + u
                for k in range(TOP_K):
                    pltpu.make_async_copy(_row(ys_ref, d_ref[0, 0, r * TOP_K + k]),
                                          _row(ybufs[slot], k * tm + r), sems.at[slot]).start(priority=k % 2)
            return carry

        lax.fori_loop(0, tm // 2, body, 0)

    def finish(slot):
        ybuf = ybufs[slot]
        pltpu.make_async_copy(ybuf, ybuf, sems.at[slot]).wait()
        gate = gate_ref[...]
        parts = []
        for j in range(x_ref.shape[1] // LANES):
            f = gate[:, 0:1] * ybuf[pl.ds(j, tm, stride=SUBLANES), :]
            for k in range(1, TOP_K):
                f = f + gate[:, k:k + 1] * ybuf[pl.ds(k * tm * SUBLANES + j, tm, stride=SUBLANES), :]
            parts.append(f)
        x = x_ref[...] + g2_ref[0] * jnp.concatenate(parts, axis=1)
        if final:
            x = _rms(x, fg_ref[...])
        o_ref[...] = x

    @pl.when(i == 0)
    def _():
        start_gather(dest_ref, 0)

    for slot in range(2):
        @pl.when(i % 2 == slot)
        def _(slot=slot):
            @pl.when(i + 1 < pl.num_programs(0))
            def _():
                start_gather(next_ref, 1 - slot)

            finish(slot)


def _combine(x, ys, dest, gates, g2, final_g, *, row_off, rows_per_batch, fixed_mod):
    t, d = x.shape
    tm = _stream_tile(t, rows_per_batch, fixed_mod, also=(dest.shape[0], row_off))
    midx = _mod_index(rows_per_batch, tm, fixed_mod)
    final = final_g is not None
    off = row_off // tm
    t_all = dest.shape[0]
    n_steps = t // tm
    dest_tiles = dest.reshape(t_all // tm, 1, tm * TOP_K)
    in_specs = [pl.BlockSpec((1, 1, tm * TOP_K), lambda i: (off + i, 0, 0), memory_space=pltpu.SMEM),
                pl.BlockSpec((1, 1, tm * TOP_K), lambda i: (off + jnp.minimum(i + 1, n_steps - 1), 0, 0),
                             memory_space=pltpu.SMEM),
                pl.BlockSpec((tm, d), lambda i: (i, 0)),
                pl.BlockSpec(memory_space=pl.ANY),
                pl.BlockSpec((tm, SUBLANES), lambda i: (off + i, 0)),
                pl.BlockSpec((1, 1, d), midx)]
    args = [dest_tiles, dest_tiles, x, ys, gates, g2]
    if final:
        in_specs.append(pl.BlockSpec(final_g.shape, lambda i: (0, 0)))
        args.append(final_g)
    return pl.pallas_call(
        functools.partial(_combine_kernel, tm=tm, final=final),
        out_shape=jax.ShapeDtypeStruct((t, d), F32),
        grid=(n_steps,),
        in_specs=in_specs,
        out_specs=pl.BlockSpec((tm, d), lambda i: (i, 0)),
        scratch_shapes=[pltpu.VMEM((TOP_K * tm * SUBLANES, LANES), F32),
                        pltpu.VMEM((TOP_K * tm * SUBLANES, LANES), F32),
                        pltpu.SemaphoreType.DMA((2,))],
        compiler_params=_params("arbitrary"),
        name="combine_final" if final else "combine",
    )(*args)


def _pad_heads(w, n_heads, axis):
    shape = list(w.shape)
    hd = shape[axis] // n_heads
    w = w.reshape(shape[:axis] + [n_heads, hd] + shape[axis + 1:])
    pad = [(0, 0)] * w.ndim
    pad[axis + 1] = (0, LANES - hd)
    w = jnp.pad(w, pad)
    shape[axis] = n_heads * LANES
    return w.reshape(shape)


def _rope_tables(s, rot_dim, lo):
    pos = jnp.arange(s, dtype=jnp.int32)
    rows, cols = (pos // GRID_W).astype(F32), (pos % GRID_W).astype(F32)
    quarter = rot_dim // 4
    inv = ROPE_THETA ** (-jnp.arange(quarter, dtype=F32) / quarter)
    ang = jnp.concatenate([rows[:, None] * inv, cols[:, None] * inv], axis=-1)
    cos, sin = jnp.cos(ang), jnp.sin(ang)
    hi = LANES - lo - rot_dim
    cos_t = jnp.concatenate([jnp.ones((s, lo), F32), cos, cos, jnp.ones((s, hi), F32)], axis=1)
    sin_t = jnp.concatenate([jnp.zeros((s, lo), F32), -sin, sin, jnp.zeros((s, hi), F32)], axis=1)
    return cos_t, sin_t


def _moe_plan(eidx, bm):
    t = eidx.shape[0]
    ids = jnp.arange(N_EXPERTS, dtype=jnp.int32)
    sel = eidx[:, :, None] == ids
    hit = sel.astype(jnp.int32).sum(1)
    chunk = _row_tile(t, 256)
    tri = jnp.tril(jnp.ones((chunk, chunk), F32))
    within = jnp.einsum("ij,cjk->cik", tri, hit.astype(F32).reshape(t // chunk, chunk, N_EXPERTS),
                        precision=HIGHEST)
    totals = within[:, -1, :]
    incl = (within + (jnp.cumsum(totals, axis=0) - totals)[:, None, :]).reshape(t, N_EXPERTS)
    incl = incl.astype(jnp.int32)
    counts = incl[-1]
    rank = jnp.where(sel, (incl - hit)[:, None, :], 0).sum(-1)
    padded = (counts + bm - 1) // bm * bm
    pends = jnp.cumsum(padded)
    pstarts = pends - padded
    dest = jnp.where(sel, pstarts, 0).sum(-1) + rank
    n_assign = t * TOP_K
    nblk = -(-(n_assign + N_EXPERTS * (bm - 1)) // bm)
    blk_start = jnp.arange(nblk, dtype=jnp.int32) * bm
    blk_expert = jnp.minimum((blk_start[:, None] >= pends[None, :]).sum(1), N_EXPERTS - 1)
    n_used = (pends[-1] // bm).astype(jnp.int32).reshape(1)
    pad_lo = jnp.concatenate([pstarts + counts, pends[-1:]]).astype(jnp.int32)
    pad_hi = jnp.concatenate([pends, jnp.full((1,), nblk * bm, pends.dtype)]).astype(jnp.int32)
    return dest.astype(jnp.int32), blk_expert.astype(jnp.int32), n_used, pad_lo, pad_hi


def _moe(token_parts, eidx, w_in, b1, w_out, b2, layer):
    dest, blk_expert, n_used, pad_lo, pad_hi = _moe_plan(eidx, EXPERT_BLOCK)
    xs, w1, w2 = _dispatch(token_parts, dest, pad_lo, pad_hi, blk_expert.shape[0] * EXPERT_BLOCK,
                           w_in, w_out, layer)
    return _expert_mlp(xs, blk_expert, n_used, w1, b1, w2, b2), dest


def kernel(x, c, ctx, c_ctx, ada_w, ada_b, norm_mix_g, norm_ffn_g, ab_w_in, mla_q_norm_g, mla_wq_b,
           mla_kv_norm_g, mla_wkv_b, swa_sink, ab_w_out, c_w_in, c_q_norm_g, c_k_norm_g, c_w_out,
           router_w, router_b, moe_w_in, moe_b_in, moe_w_out, moe_b_out, final_norm_g):
    bn, s, d = x.shape
    n_ctx = ctx.shape[1]
    depth = ada_w.shape[0]
    t_lat, t_ctx = bn * s, bn * n_ctx
    row = lambda v: v.reshape(1, -1)

    n_mod = -(-(bn + 1) // SUBLANES) * SUBLANES
    c_all = jnp.concatenate([c, c_ctx[None], jnp.zeros((n_mod - bn - 1, d), F32)], axis=0)
    mod = _modulation(c_all, ada_w, ada_b).reshape(depth, n_mod, 6, 1, d)
    ctx_row = bn

    tab_a = _rope_tables(s, HEAD_DIM_A, 0)
    tab_b = _rope_tables(s, MLA_ROPE, MLA_NOPE)
    tab_c = _rope_tables(s, HEAD_DIM_C, 0)

    xl = x.reshape(t_lat, d)
    xc = ctx.reshape(t_ctx, d)
    for i in range(depth):
        with_ctx = i < depth - 1
        j = i // 2
        sh1, sc1, g1, sh2, sc2, g2 = [mod[i, :, m] for m in range(6)]
        gmix, gffn = row(norm_mix_g[i]), row(norm_ffn_g[i])
        if i % 2 == 0:
            w_in = ab_w_in[j]
            sizes = [N_HEADS_A * HEAD_DIM_A, N_KV_A * HEAD_DIM_A, N_KV_A * HEAD_DIM_A,
                     MLA_Q_RANK, MLA_KV_RANK, MLA_ROPE]
            offs = [0]
            for n in sizes:
                offs.append(offs[-1] + n)
            cols = [w_in[:, offs[m]:offs[m + 1]] for m in range(6)]
            krg = jnp.pad(cols[5], ((0, 0), (MLA_NOPE, LANES - MLA_NOPE - MLA_ROPE)))
            w1 = jnp.concatenate([_pad_heads(cols[0], N_HEADS_A, 1), _pad_heads(cols[1], N_KV_A, 1),
                                  _pad_heads(cols[2], N_KV_A, 1), cols[3], cols[4], krg], axis=1).astype(BF16)
            wq = _pad_heads(mla_wq_b[j], MLA_HEADS, 1).astype(BF16)
            wkv = mla_wkv_b[j].reshape(MLA_KV_RANK, MLA_HEADS, MLA_NOPE + MLA_V)
            wk = _pad_heads(wkv[:, :, :MLA_NOPE].reshape(MLA_KV_RANK, -1), MLA_HEADS, 1).astype(BF16)
            wv = _pad_heads(wkv[:, :, MLA_NOPE:].reshape(MLA_KV_RANK, -1), MLA_HEADS, 1).astype(BF16)
            ws = [w1, row(mla_q_norm_g[j]), wq, row(mla_kv_norm_g[j]), wk, wv]
            na = N_HEADS_A * HEAD_DIM_A
            wo_a = ab_w_out[j][:na].astype(BF16)
            wo_b = ab_w_out[j][na:].astype(BF16)
            sink = swa_sink[j]

            qa, ka, va, qb, kb, vb = _ab_in(xl, gmix, sc1, sh1, ws, tab_a + tab_b, s, None)
            qa_c, ka_c, va_c, qb_c, kb_c, vb_c = _ab_in(xc, gmix, sc1, sh1, ws, None, n_ctx, ctx_row)
            ga = N_HEADS_A // N_KV_A
            cfg_a = dict(batch=bn, n_kv_total=N_KV_A, n_kv=N_KV_A, n_group=ga, out_dim=HEAD_DIM_A)
            cfg_b = dict(batch=bn, n_kv_total=MLA_HEADS, n_kv=MLA_STEP_HEADS, n_group=1, out_dim=MLA_V)
            oa = _attention(qa, ka, va, ka_c, va_c, sink, window=True, tq=512, **cfg_a)
            ob = _attention(qb, kb, vb, kb_c, vb_c, None, window=False, tq=1024, **cfg_b)
            o_lat, w_o = [oa, ob], [wo_a, wo_b]
            if with_ctx:
                oa_c = _attention(qa_c, None, None, ka_c, va_c, sink, window=False, tq=256, **cfg_a)
                ob_c = _attention(qb_c, None, None, kb_c, vb_c, None, window=False, tq=256, **cfg_b)
                o_ctx = [oa_c, ob_c]
        else:
            w_in = c_w_in[j].astype(BF16)
            qn, kn = row(c_q_norm_g[j]), row(c_k_norm_g[j])
            q, k, v = _c_in(xl, gmix, sc1, sh1, w_in, qn, kn, tab_c, s, None)
            q_c, k_c, v_c = _c_in(xc, gmix, sc1, sh1, w_in, qn, kn, None, n_ctx, ctx_row)
            gc = N_HEADS_C // N_KV_C
            cfg_c = dict(batch=bn, n_kv_total=N_KV_C, n_kv=1, n_group=gc, out_dim=HEAD_DIM_C,
                         window=False, tq=1024)
            o = _attention(q, k, v, k_c, v_c, None, **cfg_c)
            o_lat, w_o = [o], [c_w_out[j].astype(BF16)]
            if with_ctx:
                o_ctx = [_attention(q_c, None, None, k_c, v_c, None, **cfg_c)]

        rw = jnp.pad(router_w[i], ((0, 0), (0, LANES - N_EXPERTS)))
        rw_hi = rw.astype(BF16)
        rw = jnp.concatenate([rw_hi, (rw - rw_hi.astype(F32)).astype(BF16)], axis=1)
        rb = jnp.concatenate([router_b[i], jnp.full((LANES - N_EXPERTS,), NEG_INF, F32)]).reshape(1, LANES)
        xl, tok_l, idx_l, gate_l = _out_proj(o_lat, w_o, xl, g1, gffn, sc2, sh2, rw, rb, s, None)
        if with_ctx:
            xc, tok_c, idx_c, gate_c = _out_proj(o_ctx, w_o, xc, g1, gffn, sc2, sh2, rw, rb, n_ctx, ctx_row)
            tokens = [tok_l, tok_c]
            eidx = jnp.concatenate([idx_l, idx_c], axis=0)
            gates = jnp.concatenate([gate_l, gate_c], axis=0)
        else:
            tokens, eidx, gates = [tok_l], idx_l, gate_l

        n_tiles = moe_b_in.shape[-1] // PAIR_TILE
        b1e = moe_b_in[i].reshape(N_EXPERTS, n_tiles, LANES, 2).transpose(0, 1, 3, 2)
        b1e = b1e.reshape(N_EXPERTS, 1, -1)
        b2e = moe_b_out[i][:, None, :]
        ys, dest = _moe(tokens, eidx[:, :TOP_K], moe_w_in, b1e, moe_w_out, b2e, i)

        last = i == depth - 1
        xl = _combine(xl, ys, dest, gates, g2, row(final_norm_g) if last else None,
                      row_off=0, rows_per_batch=s, fixed_mod=None)
        if with_ctx:
            xc = _combine(xc, ys, dest, gates, g2, None,
                          row_off=t_lat, rows_per_batch=n_ctx, fixed_mod=ctx_row)
    return xl.reshape(bn, s, d)
```

```python
import functools

import jax
import jax.numpy as jnp
from jax import lax
from jax.experimental import pallas as pl
from jax.experimental.pallas import tpu as pltpu

GRID_W = 64
N_HEADS_A, N_KV_A, HEAD_DIM_A, WINDOW = 8, 2, 64, 128
MLA_HEADS, MLA_Q_RANK, MLA_KV_RANK, MLA_NOPE, MLA_ROPE, MLA_V = 8, 384, 256, 64, 32, 64
N_HEADS_C, N_KV_C, HEAD_DIM_C = 8, 2, 128
N_EXPERTS, TOP_K = 32, 4
MLA_STEP_HEADS = 4
SWIGLU_ALPHA, SWIGLU_LIMIT = 1.702, 7.0
ROPE_THETA, RMS_EPS, NEG_INF = 10000.0, 1e-6, -1e30
LOG2_E = 1.4426950408889634

LANES = 128
SUBLANES = 8
VMEM_LIMIT_BYTES = 56 * 1024 * 1024

F32 = jnp.float32
BF16 = jnp.bfloat16
HIGHEST = lax.Precision.HIGHEST


def _params(*sem):
    return pltpu.CompilerParams(dimension_semantics=sem, vmem_limit_bytes=VMEM_LIMIT_BYTES)


def _dot(a, b):
    return jnp.dot(a, b, preferred_element_type=F32)


def _dot_t(a, b):
    return lax.dot_general(a, b, (((1,), (1,)), ((), ())), preferred_element_type=F32)


def _rms(x, g):
    return x * lax.rsqrt(jnp.mean(x * x, axis=-1, keepdims=True) + RMS_EPS) * g


def _modnorm(x, g, sc, sh):
    return _rms(x, g) * (1.0 + sc) + sh


def _rope_group(x, cos, sin, half, lo):
    lane = lax.broadcasted_iota(jnp.int32, x.shape, 1)
    first = (lane >= lo) & (lane < lo + half)
    rot = jnp.where(first, pltpu.roll(x, LANES - half, 1), pltpu.roll(x, half, 1))
    return x * cos + rot * sin


def _groups(x):
    return [x[:, i * LANES:(i + 1) * LANES] for i in range(x.shape[1] // LANES)]


def _mod_kernel(c_ref, w_ref, b_ref, o_ref):
    c = c_ref[...]
    a = c * jax.nn.sigmoid(c)
    o_ref[0] = jnp.dot(a, w_ref[0], preferred_element_type=F32, precision=HIGHEST) + b_ref[0]


def _modulation(c_all, ada_w, ada_b):
    depth, d, n = ada_w.shape
    r = c_all.shape[0]
    nt = n // 4
    return pl.pallas_call(
        _mod_kernel,
        out_shape=jax.ShapeDtypeStruct((depth, r, n), F32),
        grid=(depth, n // nt),
        in_specs=[
            pl.BlockSpec((r, d), lambda i, j: (0, 0)),
            pl.BlockSpec((1, d, nt), lambda i, j: (i, 0, j)),
            pl.BlockSpec((1, 1, nt), lambda i, j: (i, 0, j)),
        ],
        out_specs=pl.BlockSpec((1, r, nt), lambda i, j: (i, 0, j)),
        compiler_params=_params("arbitrary", "arbitrary"),
        name="modulation",
    )(c_all, ada_w, ada_b.reshape(depth, 1, n))


def _ab_in_kernel(*refs, rope):
    if rope:
        (x_ref, g_ref, sc_ref, sh_ref, w1_ref, qg_ref, wq_ref, kvg_ref, wk_ref, wv_ref,
         ca_ref, sa_ref, cb_ref, sb_ref, qa_ref, ka_ref, va_ref, qb_ref, kb_ref, vb_ref) = refs
    else:
        (x_ref, g_ref, sc_ref, sh_ref, w1_ref, qg_ref, wq_ref, kvg_ref, wk_ref, wv_ref,
         qa_ref, ka_ref, va_ref, qb_ref, kb_ref, vb_ref) = refs
    h = _modnorm(x_ref[...], g_ref[...], sc_ref[0], sh_ref[0]).astype(BF16)
    p = _dot(h, w1_ref[...])
    nqa = N_HEADS_A * LANES
    nka = N_KV_A * LANES
    o = 0
    qa = p[:, o:o + nqa]; o += nqa
    ka = p[:, o:o + nka]; o += nka
    va = p[:, o:o + nka]; o += nka
    cq = p[:, o:o + MLA_Q_RANK]; o += MLA_Q_RANK
    ckv = p[:, o:o + MLA_KV_RANK]; o += MLA_KV_RANK
    krg = p[:, o:o + LANES]

    qb = _dot(_rms(cq, qg_ref[...]).astype(BF16), wq_ref[...])
    ckv_n = _rms(ckv, kvg_ref[...]).astype(BF16)
    kb = _dot(ckv_n, wk_ref[...])
    vb = _dot(ckv_n, wv_ref[...])

    scale_a = HEAD_DIM_A ** -0.5 * LOG2_E
    scale_b = (MLA_NOPE + MLA_ROPE) ** -0.5 * LOG2_E
    if rope:
        ca, sa, cb, sb = ca_ref[...], sa_ref[...], cb_ref[...], sb_ref[...]
        half_a, half_b = HEAD_DIM_A // 2, MLA_ROPE // 2
        qa_g = [_rope_group(t, ca, sa, half_a, 0) * scale_a for t in _groups(qa)]
        ka_g = [_rope_group(t, ca, sa, half_a, 0) for t in _groups(ka)]
        qb_g = [_rope_group(t, cb, sb, half_b, MLA_NOPE) * scale_b for t in _groups(qb)]
        krg = _rope_group(krg, cb, sb, half_b, MLA_NOPE)
    else:
        qa_g = [t * scale_a for t in _groups(qa)]
        ka_g = _groups(ka)
        qb_g = [t * scale_b for t in _groups(qb)]
    kb_g = [t + krg for t in _groups(kb)]
    qa_ref[...] = jnp.concatenate(qa_g, axis=1).astype(BF16)
    ka_ref[...] = jnp.concatenate(ka_g, axis=1).astype(BF16)
    va_ref[...] = va.astype(BF16)
    qb_ref[...] = jnp.concatenate(qb_g, axis=1).astype(BF16)
    kb_ref[...] = jnp.concatenate(kb_g, axis=1).astype(BF16)
    vb_ref[...] = vb.astype(BF16)


def _row_tile(t, pref=512, also=()):
    tm = pref
    while any(n % tm for n in (t, *also)):
        tm //= 2
    return tm


def _stream_tile(t, rows_per_batch, fixed_mod, pref=512, also=()):
    return _row_tile(t if fixed_mod is not None else min(t, rows_per_batch), pref, also)


def _mod_index(rows_per_batch, tm, fixed):
    if fixed is not None:
        return lambda i: (fixed, 0, 0)
    return lambda i: ((i * tm) // rows_per_batch, 0, 0)


def _ab_in(x, g, sc, sh, w, tables, rows_per_batch, fixed_mod):
    t, d = x.shape
    tm = _stream_tile(t, rows_per_batch, fixed_mod)
    rope = tables is not None
    midx = _mod_index(rows_per_batch, tm, fixed_mod)
    full = lambda a: pl.BlockSpec(a.shape, lambda i: (0,) * a.ndim)
    in_specs = [
        pl.BlockSpec((tm, d), lambda i: (i, 0)),
        full(g),
        pl.BlockSpec((1, 1, d), midx),
        pl.BlockSpec((1, 1, d), midx),
    ] + [full(a) for a in w]
    args = [x, g, sc, sh] + list(w)
    if rope:
        nt = rows_per_batch // tm
        for tab in tables:
            in_specs.append(pl.BlockSpec((tm, LANES), lambda i: (i % nt, 0)))
            args.append(tab)
    widths = [N_HEADS_A * LANES, N_KV_A * LANES, N_KV_A * LANES,
              MLA_HEADS * LANES, MLA_HEADS * LANES, MLA_HEADS * LANES]
    return pl.pallas_call(
        functools.partial(_ab_in_kernel, rope=rope),
        out_shape=[jax.ShapeDtypeStruct((t, n), BF16) for n in widths],
        grid=(t // tm,),
        in_specs=in_specs,
        out_specs=[pl.BlockSpec((tm, n), lambda i: (i, 0)) for n in widths],
        compiler_params=_params("arbitrary"),
        name="ab_in_rope" if rope else "ab_in",
    )(*args)


def _c_in_kernel(*refs, rope):
    if rope:
        x_ref, g_ref, sc_ref, sh_ref, w_ref, qn_ref, kn_ref, cc_ref, sc2_ref, q_ref, k_ref, v_ref = refs
    else:
        x_ref, g_ref, sc_ref, sh_ref, w_ref, qn_ref, kn_ref, q_ref, k_ref, v_ref = refs
    h = _modnorm(x_ref[...], g_ref[...], sc_ref[0], sh_ref[0]).astype(BF16)
    p = _dot(h, w_ref[...])
    nq = N_HEADS_C * HEAD_DIM_C
    nk = N_KV_C * HEAD_DIM_C
    q_g = [_rms(t, qn_ref[...]) for t in _groups(p[:, :nq])]
    k_g = [_rms(t, kn_ref[...]) for t in _groups(p[:, nq:nq + nk])]
    scale = HEAD_DIM_C ** -0.5 * LOG2_E
    if rope:
        cc, ss = cc_ref[...], sc2_ref[...]
        q_g = [_rope_group(t, cc, ss, HEAD_DIM_C // 2, 0) for t in q_g]
        k_g = [_rope_group(t, cc, ss, HEAD_DIM_C // 2, 0) for t in k_g]
    q_ref[...] = jnp.concatenate([t * scale for t in q_g], axis=1).astype(BF16)
    k_ref[...] = jnp.concatenate(k_g, axis=1).astype(BF16)
    v_ref[...] = p[:, nq + nk:].astype(BF16)


def _c_in(x, g, sc, sh, w, qn, kn, tables, rows_per_batch, fixed_mod):
    t, d = x.shape
    tm = _stream_tile(t, rows_per_batch, fixed_mod)
    rope = tables is not None
    midx = _mod_index(rows_per_batch, tm, fixed_mod)
    full = lambda a: pl.BlockSpec(a.shape, lambda i: (0,) * a.ndim)
    in_specs = [
        pl.BlockSpec((tm, d), lambda i: (i, 0)),
        full(g),
        pl.BlockSpec((1, 1, d), midx),
        pl.BlockSpec((1, 1, d), midx),
        full(w), full(qn), full(kn),
    ]
    args = [x, g, sc, sh, w, qn, kn]
    if rope:
        nt = rows_per_batch // tm
        for tab in tables:
            in_specs.append(pl.BlockSpec((tm, LANES), lambda i: (i % nt, 0)))
            args.append(tab)
    widths = [N_HEADS_C * HEAD_DIM_C, N_KV_C * HEAD_DIM_C, N_KV_C * HEAD_DIM_C]
    return pl.pallas_call(
        functools.partial(_c_in_kernel, rope=rope),
        out_shape=[jax.ShapeDtypeStruct((t, n), BF16) for n in widths],
        grid=(t // tm,),
        in_specs=in_specs,
        out_specs=[pl.BlockSpec((tm, n), lambda i: (i, 0)) for n in widths],
        compiler_params=_params("arbitrary"),
        name="c_in_rope" if rope else "c_in",
    )(*args)


def _attn_kernel(*refs, n_kv, n_group, out_dim, has_lat, has_sink, window, tq, s_lat, kw):
    refs = list(refs)
    sink_ref = refs.pop(0) if has_sink else None
    q_ref = refs.pop(0)
    if has_lat:
        k1_ref, v1_ref = refs.pop(0), refs.pop(0)
    k2_ref, v2_ref, o_ref = refs
    hk0 = pl.program_id(1) * n_kv
    qi = pl.program_id(2)
    if window:
        ws = pl.multiple_of(jnp.clip(qi * tq - WINDOW, 0, s_lat - kw), LANES)
        rows = pl.ds(ws, kw)
        qpos = qi * tq + lax.broadcasted_iota(jnp.int32, (tq, kw), 0)
        kpos = ws + lax.broadcasted_iota(jnp.int32, (tq, kw), 1)
        ok = jnp.abs(kpos - qpos) <= WINDOW
    else:
        rows = slice(None)
    outs = []
    for j in range(n_kv):
        lanes = slice(j * LANES, (j + 1) * LANES)
        k2, v2 = k2_ref[:, lanes], v2_ref[:, lanes]
        if has_lat:
            k1, v1 = k1_ref[rows, lanes], v1_ref[rows, lanes]
        for g in range(n_group):
            h = j * n_group + g
            q = q_ref[:, h * LANES:(h + 1) * LANES]
            s2 = _dot_t(q, k2)
            m = jnp.max(s2, axis=-1, keepdims=True)
            if has_lat:
                s1 = _dot_t(q, k1)
                if window:
                    s1 = jnp.where(ok, s1, NEG_INF)
                m = jnp.maximum(m, jnp.max(s1, axis=-1, keepdims=True))
            if has_sink:
                sk = sink_ref[hk0 * n_group + h] * LOG2_E
                m = jnp.maximum(m, sk)
            e2 = jnp.exp2(s2 - m)
            den = jnp.sum(e2, axis=-1, keepdims=True)
            acc = _dot(e2.astype(BF16), v2)
            if has_lat:
                e1 = jnp.exp2(s1 - m)
                den = den + jnp.sum(e1, axis=-1, keepdims=True)
                acc = acc + _dot(e1.astype(BF16), v1)
            if has_sink:
                den = den + jnp.exp2(sk - m)
            outs.append(acc / den)
    if out_dim == LANES:
        for h, o in enumerate(outs):
            o_ref[:, h * LANES:(h + 1) * LANES] = o.astype(BF16)
    else:
        lane = lax.broadcasted_iota(jnp.int32, (tq, LANES), 1)
        for p in range(len(outs) // 2):
            pair = jnp.where(lane < out_dim, outs[2 * p], pltpu.roll(outs[2 * p + 1], out_dim, 1))
            o_ref[:, p * LANES:(p + 1) * LANES] = pair.astype(BF16)


def _attention(q, k_lat, v_lat, k_ctx, v_ctx, sink, *, batch, n_kv_total, n_kv, n_group, out_dim,
               window, tq):
    tq_total = q.shape[0] // batch
    tq = min(tq, tq_total)
    n_ctx = k_ctx.shape[0] // batch
    has_lat = k_lat is not None
    has_sink = sink is not None
    s_lat = k_lat.shape[0] // batch if has_lat else 0
    kw = min(tq + 2 * WINDOW, s_lat) if window else 0
    nq = tq_total // tq
    n_heads = n_kv * n_group
    assert out_dim == LANES or (2 * out_dim == LANES and n_heads % 2 == 0)
    in_specs, args = [], []
    if has_sink:
        in_specs.append(pl.BlockSpec(memory_space=pltpu.SMEM))
        args.append(sink)
    in_specs.append(pl.BlockSpec((tq, n_heads * LANES), lambda b, h, i: (b * nq + i, h)))
    args.append(q)
    if has_lat:
        in_specs += [pl.BlockSpec((s_lat, n_kv * LANES), lambda b, h, i: (b, h))] * 2
        args += [k_lat, v_lat]
    in_specs += [pl.BlockSpec((n_ctx, n_kv * LANES), lambda b, h, i: (b, h))] * 2
    args += [k_ctx, v_ctx]
    n_steps = n_kv_total // n_kv
    return pl.pallas_call(
        functools.partial(_attn_kernel, n_kv=n_kv, n_group=n_group, out_dim=out_dim, has_lat=has_lat,
                          has_sink=has_sink, window=window, tq=tq, s_lat=s_lat, kw=kw),
        out_shape=jax.ShapeDtypeStruct((q.shape[0], n_steps * n_heads * out_dim), BF16),
        grid=(batch, n_steps, nq),
        in_specs=in_specs,
        out_specs=pl.BlockSpec((tq, n_heads * out_dim), lambda b, h, i: (b * nq + i, h)),
        compiler_params=_params("arbitrary", "arbitrary", "arbitrary"),
        name="attn_win" if window else ("attn_dense" if has_lat else "attn_ctx"),
    )(*args)


def _out_kernel(*refs, n_parts, tm):
    o_refs = refs[:n_parts]
    w_refs = refs[n_parts:2 * n_parts]
    (x_ref, g1_ref, gn_ref, sc_ref, sh_ref, rw_ref, rb_ref,
     xo_ref, tok_ref, idx_ref, gate_ref) = refs[2 * n_parts:]
    y = _dot(o_refs[0][...], w_refs[0][...])
    for o_r, w_r in zip(o_refs[1:], w_refs[1:]):
        y = y + _dot(o_r[...], w_r[...])
    x = x_ref[...] + g1_ref[0] * y
    xo_ref[...] = x
    tok = _modnorm(x, gn_ref[...], sc_ref[0], sh_ref[0])
    for j in range(tok.shape[1] // LANES):
        tok_ref[pl.ds(j, tm, stride=SUBLANES), :] = tok[:, j * LANES:(j + 1) * LANES]
    tok_hi = tok.astype(BF16)
    tok_lo = (tok - tok_hi.astype(F32)).astype(BF16)
    hh_hl = _dot(tok_hi, rw_ref[...])
    logits = (hh_hl[:, :LANES] + hh_hl[:, LANES:] + _dot(tok_lo, rw_ref[:, :LANES])) + rb_ref[...]
    lane = lax.broadcasted_iota(jnp.int32, logits.shape, 1).astype(F32)
    vals, idxs = [], []
    l = logits
    for _ in range(TOP_K):
        m = jnp.max(l, axis=-1, keepdims=True)
        idx = jnp.min(jnp.where(l == m, lane, float(LANES)), axis=-1, keepdims=True)
        vals.append(m)
        idxs.append(idx)
        l = jnp.where(lane == idx, -jnp.inf, l)
    es = [jnp.exp(v - vals[0]) for v in vals]
    den = es[0]
    for e in es[1:]:
        den = den + e
    lane8 = lax.broadcasted_iota(jnp.int32, (tm, SUBLANES), 1)
    io = jnp.zeros((tm, SUBLANES), jnp.int32)
    go = jnp.zeros((tm, SUBLANES), F32)
    for k in range(TOP_K):
        io = jnp.where(lane8 == k, idxs[k].astype(jnp.int32), io)
        go = jnp.where(lane8 == k, es[k] / den, go)
    idx_ref[...] = io
    gate_ref[...] = go


def _out_proj(o_parts, w_parts, x, g1, gn, sc, sh, rw, rb, rows_per_batch, fixed_mod):
    t, d = x.shape
    tm = _stream_tile(t, rows_per_batch, fixed_mod)
    midx = _mod_index(rows_per_batch, tm, fixed_mod)
    full = lambda a: pl.BlockSpec(a.shape, lambda i: (0,) * a.ndim)
    n_parts = len(o_parts)
    in_specs = ([pl.BlockSpec((tm, o.shape[1]), lambda i: (i, 0)) for o in o_parts]
                + [full(w) for w in w_parts]
                + [pl.BlockSpec((tm, d), lambda i: (i, 0)),
                   pl.BlockSpec((1, 1, d), midx), full(gn),
                   pl.BlockSpec((1, 1, d), midx), pl.BlockSpec((1, 1, d), midx),
                   full(rw), full(rb)])
    return pl.pallas_call(
        functools.partial(_out_kernel, n_parts=n_parts, tm=tm),
        out_shape=[jax.ShapeDtypeStruct((t, d), F32),
                   jax.ShapeDtypeStruct((t * SUBLANES, d // SUBLANES), F32),
                   jax.ShapeDtypeStruct((t, SUBLANES), jnp.int32),
                   jax.ShapeDtypeStruct((t, SUBLANES), F32)],
        grid=(t // tm,),
        in_specs=in_specs,
        out_specs=[pl.BlockSpec((tm, d), lambda i: (i, 0)),
                   pl.BlockSpec((tm * SUBLANES, d // SUBLANES), lambda i: (i, 0)),
                   pl.BlockSpec((tm, SUBLANES), lambda i: (i, 0)),
                   pl.BlockSpec((tm, SUBLANES), lambda i: (i, 0))],
        compiler_params=_params("arbitrary"),
        name="out_proj",
    )(*o_parts, *w_parts, x, g1, gn, sc, sh, rw, rb)


DISPATCH_TILE = 512
PAIR_TILE = 2 * LANES
WEIGHT_SPLIT = 2
ZERO_ROWS = 64


def _row(ref, r):
    return ref.at[pl.ds(pl.multiple_of(r * SUBLANES, SUBLANES), SUBLANES)]


def _dispatch_kernel(*refs, tm, part_tiles, n_experts):
    n_parts = len(part_tiles)
    plo_ref, phi_ref, dest_ref = refs[:3]
    tok_refs = refs[3:3 + n_parts]
    p_ref, w1_ref, w2_ref, xs_ref, o1_ref, o2_ref, zero_ref, st0, st1, sems, zsem = refs[3 + n_parts:]
    i = pl.program_id(0)
    n_tiles = sum(part_tiles)
    stage = (st0, st1)

    @pl.when(i == 0)
    def _():
        zero_ref[...] = jnp.zeros(zero_ref.shape, zero_ref.dtype)

        def fill_range(e, carry):
            lo, hi = plo_ref[e], phi_ref[e]
            n_chunks = lax.shift_right_logical(hi - lo, ZERO_ROWS.bit_length() - 1)
            rest = lo + n_chunks * ZERO_ROWS

            def chunk_copy(c):
                first = pl.multiple_of((lo + c * ZERO_ROWS) * SUBLANES, SUBLANES)
                return pltpu.make_async_copy(zero_ref, xs_ref.at[pl.ds(first, ZERO_ROWS * SUBLANES)], zsem)

            def row_copy(r):
                return pltpu.make_async_copy(zero_ref.at[pl.ds(0, SUBLANES)], _row(xs_ref, r), zsem)

            for op in ("start", "wait"):
                lax.fori_loop(0, n_chunks, lambda c, cr, op=op: (getattr(chunk_copy(c), op)(), cr)[1], carry)
                lax.fori_loop(rest, hi, lambda r, cr, op=op: (getattr(row_copy(r), op)(), cr)[1], carry)
            return carry

        lax.fori_loop(0, plo_ref.shape[0], fill_range, 0)

    def start_rows(tok_ref, slot):
        stage[slot][...] = tok_ref[...]

        def body(t, carry):
            for u in range(2):
                r = t * 2 + u
                src = _row(stage[slot], r)
                for k in range(TOP_K):
                    pltpu.make_async_copy(src, _row(xs_ref, dest_ref[0, 0, r * TOP_K + k]), sems.at[slot]).start(
                        priority=k % 2)
            return carry

        lax.fori_loop(0, tm // 2, body, 0)

    def wait_rows(slot):
        done = xs_ref.at[pl.ds(0, tm * TOP_K * SUBLANES)]
        pltpu.make_async_copy(done, done, sems.at[slot]).wait()

    first = 0
    for tok_ref, n in zip(tok_refs, part_tiles):
        for slot in range(2):
            pl.when((i >= first) & (i < first + n) & (i % 2 == slot))(
                functools.partial(start_rows, tok_ref, slot))
        first += n

    @pl.when(i < n_experts * WEIGHT_SPLIT)
    def _():
        perm = p_ref[...]
        for j in range(w1_ref.shape[3] // PAIR_TILE):
            cols = slice(j * PAIR_TILE, (j + 1) * PAIR_TILE)
            o1_ref[0, :, cols] = _dot(w1_ref[0, 0, :, cols].astype(BF16), perm).astype(BF16)
        o2_ref[0] = w2_ref[0, 0].astype(BF16)

    for slot in range(2):
        pl.when((i >= 1) & (i <= n_tiles) & ((i - 1) % 2 == slot))(functools.partial(wait_rows, slot))
        pl.when((i == pl.num_programs(0) - 1) & (i < n_tiles) & (i % 2 == slot))(
            functools.partial(wait_rows, slot))


def _dispatch(token_parts, dest, pad_lo, pad_hi, cap, w_in, w_out, layer):
    t = dest.shape[0]
    tm = _row_tile(min(p.shape[0] // SUBLANES for p in token_parts), DISPATCH_TILE)
    part_tiles = tuple(p.shape[0] // SUBLANES // tm for p in token_parts)
    n_tiles = sum(part_tiles)
    assert n_tiles * tm == t
    _, e, d, f2 = w_in.shape
    lane = jnp.arange(LANES)
    perm = jnp.zeros((PAIR_TILE, PAIR_TILE), F32).at[2 * lane, lane].set(1.0)
    perm = perm.at[2 * lane + 1, LANES + lane].set(1.0).astype(BF16)
    in_specs = [pl.BlockSpec((1, 1, tm * TOP_K), lambda i, lo, hi: (jnp.minimum(i, n_tiles - 1), 0, 0),
                             memory_space=pltpu.SMEM)]
    first = 0
    for n in part_tiles:
        in_specs.append(pl.BlockSpec(
            (tm * SUBLANES, LANES), lambda i, lo, hi, first=first, n=n: (jnp.clip(i - first, 0, n - 1), 0)))
        first += n
    ws = WEIGHT_SPLIT
    assert f2 % (ws * PAIR_TILE) == 0 and (f2 // 2) % (ws * SUBLANES * 2) == 0
    share = lambda i: jnp.minimum(i, e * ws - 1)
    in_specs += [pl.BlockSpec((PAIR_TILE, PAIR_TILE), lambda i, lo, hi: (0, 0)),
                 pl.BlockSpec((1, 1, d, f2 // ws), lambda i, lo, hi: (layer, share(i) // ws, 0, share(i) % ws)),
                 pl.BlockSpec((1, 1, f2 // 2 // ws, d),
                              lambda i, lo, hi: (layer, share(i) // ws, share(i) % ws, 0))]
    grid_spec = pltpu.PrefetchScalarGridSpec(
        num_scalar_prefetch=2,
        grid=(max(n_tiles, e * ws),),
        in_specs=in_specs,
        out_specs=[pl.BlockSpec(memory_space=pl.ANY),
                   pl.BlockSpec((1, d, f2 // ws), lambda i, lo, hi: (share(i) // ws, 0, share(i) % ws)),
                   pl.BlockSpec((1, f2 // 2 // ws, d), lambda i, lo, hi: (share(i) // ws, share(i) % ws, 0))],
        scratch_shapes=[pltpu.VMEM((ZERO_ROWS * SUBLANES, LANES), F32),
                        pltpu.VMEM((tm * SUBLANES, LANES), F32), pltpu.VMEM((tm * SUBLANES, LANES), F32),
                        pltpu.SemaphoreType.DMA((2,)), pltpu.SemaphoreType.DMA],
    )
    return pl.pallas_call(
        functools.partial(_dispatch_kernel, tm=tm, part_tiles=part_tiles, n_experts=e),
        out_shape=[jax.ShapeDtypeStruct((cap * SUBLANES, LANES), F32),
                   jax.ShapeDtypeStruct(w_in.shape[1:], BF16), jax.ShapeDtypeStruct(w_out.shape[1:], BF16)],
        grid_spec=grid_spec,
        compiler_params=_params("arbitrary"),
        name="dispatch",
    )(pad_lo, pad_hi, dest.reshape(n_tiles, 1, tm * TOP_K), *token_parts, perm, w_in, w_out)


EXPERT_BLOCK = 512


def _expert_kernel(be_ref, nb_ref, x_ref, w1_ref, b1_ref, w2_ref, b2_ref, o_ref, *, bm):
    del be_ref
    b = pl.program_id(0)
    n_j = w1_ref.shape[1] // LANES

    @pl.when(b < nb_ref[0])
    def _():
        x = jnp.concatenate([x_ref[pl.ds(j, bm, stride=SUBLANES), :] for j in range(n_j)], axis=1)
        u = _dot(x.astype(BF16), w1_ref[0]) + b1_ref[0]
        acts = []
        for j in range(u.shape[1] // PAIR_TILE):
            glu = jnp.minimum(u[:, j * PAIR_TILE:j * PAIR_TILE + LANES], SWIGLU_LIMIT)
            lin = jnp.clip(u[:, j * PAIR_TILE + LANES:(j + 1) * PAIR_TILE], -SWIGLU_LIMIT, SWIGLU_LIMIT)
            acts.append(glu * jax.nn.sigmoid(SWIGLU_ALPHA * glu) * (lin + 1.0))
        a = jnp.concatenate(acts, axis=1)
        y = _dot(a.astype(BF16), w2_ref[0]) + b2_ref[0]
        for j in range(y.shape[1] // LANES):
            o_ref[pl.ds(j, bm, stride=SUBLANES), :] = y[:, j * LANES:(j + 1) * LANES]

    @pl.when(b >= nb_ref[0])
    def _():
        o_ref[...] = jnp.zeros(o_ref.shape, o_ref.dtype)


def _expert_mlp(xs, blk_expert, n_used, w1, b1, w2, b2):
    bm = EXPERT_BLOCK
    nblk = blk_expert.shape[0]
    _, d, f2 = w1.shape
    grid_spec = pltpu.PrefetchScalarGridSpec(
        num_scalar_prefetch=2,
        grid=(nblk,),
        in_specs=[
            pl.BlockSpec((bm * SUBLANES, LANES), lambda b, be, nb: (b, 0)),
            pl.BlockSpec((1, d, f2), lambda b, be, nb: (be[b], 0, 0)),
            pl.BlockSpec((1, 1, f2), lambda b, be, nb: (be[b], 0, 0)),
            pl.BlockSpec((1, f2 // 2, d), lambda b, be, nb: (be[b], 0, 0)),
            pl.BlockSpec((1, 1, d), lambda b, be, nb: (be[b], 0, 0)),
        ],
        out_specs=pl.BlockSpec((bm * SUBLANES, LANES), lambda b, be, nb: (b, 0)),
    )
    return pl.pallas_call(
        functools.partial(_expert_kernel, bm=bm),
        out_shape=jax.ShapeDtypeStruct(xs.shape, F32),
        grid_spec=grid_spec,
        compiler_params=_params("arbitrary"),
        name="expert_mlp",
    )(blk_expert, n_used, xs, w1, b1, w2, b2)


def _combine_kernel(*refs, tm, final):
    if final:
        dest_ref, next_ref, x_ref, ys_ref, gate_ref, g2_ref, fg_ref, o_ref, yb0, yb1, sems = refs
    else:
        dest_ref, next_ref, x_ref, ys_ref, gate_ref, g2_ref, o_ref, yb0, yb1, sems = refs
    i = pl.program_id(0)
    ybufs = (yb0, yb1)

    def start_gather(d_ref, slot):
        def body(t, carry):
            for u in range(2):
                r = t * 2 + u
                for k in range(TOP_K):
                    pltpu.make_async_copy(_row(ys_ref, d_ref[0, 0, r * TOP_K + k]),
                                          _row(ybufs[slot], k * tm + r), sems.at[slot]).start(priority=k % 2)
            return carry

        lax.fori_loop(0, tm // 2, body, 0)

    def finish(slot):
        ybuf = ybufs[slot]
        pltpu.make_async_copy(ybuf, ybuf, sems.at[slot]).wait()
        gate = gate_ref[...]
        parts = []
        for j in range(x_ref.shape[1] // LANES):
            f = gate[:, 0:1] * ybuf[pl.ds(j, tm, stride=SUBLANES), :]
            for k in range(1, TOP_K):
                f = f + gate[:, k:k + 1] * ybuf[pl.ds(k * tm * SUBLANES + j, tm, stride=SUBLANES), :]
            parts.append(f)
        x = x_ref[...] + g2_ref[0] * jnp.concatenate(parts, axis=1)
        if final:
            x = _rms(x, fg_ref[...])
        o_ref[...] = x

    @pl.when(i == 0)
    def _():
        start_gather(dest_ref, 0)

    for slot in range(2):
        @pl.when(i % 2 == slot)
        def _(slot=slot):
            @pl.when(i + 1 < pl.num_programs(0))
            def _():
                start_gather(next_ref, 1 - slot)

            finish(slot)


def _combine(x, ys, dest, gates, g2, final_g, *, row_off, rows_per_batch, fixed_mod):
    t, d = x.shape
    tm = _stream_tile(t, rows_per_batch, fixed_mod, also=(dest.shape[0], row_off))
    midx = _mod_index(rows_per_batch, tm, fixed_mod)
    final = final_g is not None
    off = row_off // tm
    t_all = dest.shape[0]
    n_steps = t // tm
    dest_tiles = dest.reshape(t_all // tm, 1, tm * TOP_K)
    in_specs = [pl.BlockSpec((1, 1, tm * TOP_K), lambda i: (off + i, 0, 0), memory_space=pltpu.SMEM),
                pl.BlockSpec((1, 1, tm * TOP_K), lambda i: (off + jnp.minimum(i + 1, n_steps - 1), 0, 0),
                             memory_space=pltpu.SMEM),
                pl.BlockSpec((tm, d), lambda i: (i, 0)),
                pl.BlockSpec(memory_space=pl.ANY),
                pl.BlockSpec((tm, SUBLANES), lambda i: (off + i, 0)),
                pl.BlockSpec((1, 1, d), midx)]
    args = [dest_tiles, dest_tiles, x, ys, gates, g2]
    if final:
        in_specs.append(pl.BlockSpec(final_g.shape, lambda i: (0, 0)))
        args.append(final_g)
    return pl.pallas_call(
        functools.partial(_combine_kernel, tm=tm, final=final),
        out_shape=jax.ShapeDtypeStruct((t, d), F32),
        grid=(n_steps,),
        in_specs=in_specs,
        out_specs=pl.BlockSpec((tm, d), lambda i: (i, 0)),
        scratch_shapes=[pltpu.VMEM((TOP_K * tm * SUBLANES, LANES), F32),
                        pltpu.VMEM((TOP_K * tm * SUBLANES, LANES), F32),
                        pltpu.SemaphoreType.DMA((2,))],
        compiler_params=_params("arbitrary"),
        name="combine_final" if final else "combine",
    )(*args)


def _pad_heads(w, n_heads, axis):
    shape = list(w.shape)
    hd = shape[axis] // n_heads
    w = w.reshape(shape[:axis] + [n_heads, hd] + shape[axis + 1:])
    pad = [(0, 0)] * w.ndim
    pad[axis + 1] = (0, LANES - hd)
    w = jnp.pad(w, pad)
    shape[axis] = n_heads * LANES
    return w.reshape(shape)


def _rope_tables(s, rot_dim, lo):
    pos = jnp.arange(s, dtype=jnp.int32)
    rows, cols = (pos // GRID_W).astype(F32), (pos % GRID_W).astype(F32)
    quarter = rot_dim // 4
    inv = ROPE_THETA ** (-jnp.arange(quarter, dtype=F32) / quarter)
    ang = jnp.concatenate([rows[:, None] * inv, cols[:, None] * inv], axis=-1)
    cos, sin = jnp.cos(ang), jnp.sin(ang)
    hi = LANES - lo - rot_dim
    cos_t = jnp.concatenate([jnp.ones((s, lo), F32), cos, cos, jnp.ones((s, hi), F32)], axis=1)
    sin_t = jnp.concatenate([jnp.zeros((s, lo), F32), -sin, sin, jnp.zeros((s, hi), F32)], axis=1)
    return cos_t, sin_t


def _moe_plan(eidx, bm):
    t = eidx.shape[0]
    ids = jnp.arange(N_EXPERTS, dtype=jnp.int32)
    sel = eidx[:, :, None] == ids
    hit = sel.astype(jnp.int32).sum(1)
    chunk = _row_tile(t, 256)
    tri = jnp.tril(jnp.ones((chunk, chunk), F32))
    within = jnp.einsum("ij,cjk->cik", tri, hit.astype(F32).reshape(t // chunk, chunk, N_EXPERTS),
                        precision=HIGHEST)
    totals = within[:, -1, :]
    incl = (within + (jnp.cumsum(totals, axis=0) - totals)[:, None, :]).reshape(t, N_EXPERTS)
    incl = incl.astype(jnp.int32)
    counts = incl[-1]
    rank = jnp.where(sel, (incl - hit)[:, None, :], 0).sum(-1)
    padded = (counts + bm - 1) // bm * bm
    pends = jnp.cumsum(padded)
    pstarts = pends - padded
    dest = jnp.where(sel, pstarts, 0).sum(-1) + rank
    n_assign = t * TOP_K
    nblk = -(-(n_assign + N_EXPERTS * (bm - 1)) // bm)
    blk_start = jnp.arange(nblk, dtype=jnp.int32) * bm
    blk_expert = jnp.minimum((blk_start[:, None] >= pends[None, :]).sum(1), N_EXPERTS - 1)
    n_used = (pends[-1] // bm).astype(jnp.int32).reshape(1)
    pad_lo = jnp.concatenate([pstarts + counts, pends[-1:]]).astype(jnp.int32)
    pad_hi = jnp.concatenate([pends, jnp.full((1,), nblk * bm, pends.dtype)]).astype(jnp.int32)
    return dest.astype(jnp.int32), blk_expert.astype(jnp.int32), n_used, pad_lo, pad_hi


def _moe(token_parts, eidx, w_in, b1, w_out, b2, layer):
    dest, blk_expert, n_used, pad_lo, pad_hi = _moe_plan(eidx, EXPERT_BLOCK)
    xs, w1, w2 = _dispatch(token_parts, dest, pad_lo, pad_hi, blk_expert.shape[0] * EXPERT_BLOCK,
                           w_in, w_out, layer)
    return _expert_mlp(xs, blk_expert, n_used, w1, b1, w2, b2), dest


def kernel(x, c, ctx, c_ctx, ada_w, ada_b, norm_mix_g, norm_ffn_g, ab_w_in, mla_q_norm_g, mla_wq_b,
           mla_kv_norm_g, mla_wkv_b, swa_sink, ab_w_out, c_w_in, c_q_norm_g, c_k_norm_g, c_w_out,
           router_w, router_b, moe_w_in, moe_b_in, moe_w_out, moe_b_out, final_norm_g):
    bn, s, d = x.shape
    n_ctx = ctx.shape[1]
    depth = ada_w.shape[0]
    t_lat, t_ctx = bn * s, bn * n_ctx
    row = lambda v: v.reshape(1, -1)

    n_mod = -(-(bn + 1) // SUBLANES) * SUBLANES
    c_all = jnp.concatenate([c, c_ctx[None], jnp.zeros((n_mod - bn - 1, d), F32)], axis=0)
    mod = _modulation(c_all, ada_w, ada_b).reshape(depth, n_mod, 6, 1, d)
    ctx_row = bn

    tab_a = _rope_tables(s, HEAD_DIM_A, 0)
    tab_b = _rope_tables(s, MLA_ROPE, MLA_NOPE)
    tab_c = _rope_tables(s, HEAD_DIM_C, 0)

    xl = x.reshape(t_lat, d)
    xc = ctx.reshape(t_ctx, d)
    for i in range(depth):
        with_ctx = i < depth - 1
        j = i // 2
        sh1, sc1, g1, sh2, sc2, g2 = [mod[i, :, m] for m in range(6)]
        gmix, gffn = row(norm_mix_g[i]), row(norm_ffn_g[i])
        if i % 2 == 0:
            w_in = ab_w_in[j]
            sizes = [N_HEADS_A * HEAD_DIM_A, N_KV_A * HEAD_DIM_A, N_KV_A * HEAD_DIM_A,
                     MLA_Q_RANK, MLA_KV_RANK, MLA_ROPE]
            offs = [0]
            for n in sizes:
                offs.append(offs[-1] + n)
            cols = [w_in[:, offs[m]:offs[m + 1]] for m in range(6)]
            krg = jnp.pad(cols[5], ((0, 0), (MLA_NOPE, LANES - MLA_NOPE - MLA_ROPE)))
            w1 = jnp.concatenate([_pad_heads(cols[0], N_HEADS_A, 1), _pad_heads(cols[1], N_KV_A, 1),
                                  _pad_heads(cols[2], N_KV_A, 1), cols[3], cols[4], krg], axis=1).astype(BF16)
            wq = _pad_heads(mla_wq_b[j], MLA_HEADS, 1).astype(BF16)
            wkv = mla_wkv_b[j].reshape(MLA_KV_RANK, MLA_HEADS, MLA_NOPE + MLA_V)
            wk = _pad_heads(wkv[:, :, :MLA_NOPE].reshape(MLA_KV_RANK, -1), MLA_HEADS, 1).astype(BF16)
            wv = _pad_heads(wkv[:, :, MLA_NOPE:].reshape(MLA_KV_RANK, -1), MLA_HEADS, 1).astype(BF16)
            ws = [w1, row(mla_q_norm_g[j]), wq, row(mla_kv_norm_g[j]), wk, wv]
            na = N_HEADS_A * HEAD_DIM_A
            wo_a = ab_w_out[j][:na].astype(BF16)
            wo_b = ab_w_out[j][na:].astype(BF16)
            sink = swa_sink[j]

            qa, ka, va, qb, kb, vb = _ab_in(xl, gmix, sc1, sh1, ws, tab_a + tab_b, s, None)
            qa_c, ka_c, va_c, qb_c, kb_c, vb_c = _ab_in(xc, gmix, sc1, sh1, ws, None, n_ctx, ctx_row)
            ga = N_HEADS_A // N_KV_A
            cfg_a = dict(batch=bn, n_kv_total=N_KV_A, n_kv=N_KV_A, n_group=ga, out_dim=HEAD_DIM_A)
            cfg_b = dict(batch=bn, n_kv_total=MLA_HEADS, n_kv=MLA_STEP_HEADS, n_group=1, out_dim=MLA_V)
            oa = _attention(qa, ka, va, ka_c, va_c, sink, window=True, tq=512, **cfg_a)
            ob = _attention(qb, kb, vb, kb_c, vb_c, None, window=False, tq=1024, **cfg_b)
            o_lat, w_o = [oa, ob], [wo_a, wo_b]
            if with_ctx:
                oa_c = _attention(qa_c, None, None, ka_c, va_c, sink, window=False, tq=256, **cfg_a)
                ob_c = _attention(qb_c, None, None, kb_c, vb_c, None, window=False, tq=256, **cfg_b)
                o_ctx = [oa_c, ob_c]
        else:
            w_in = c_w_in[j].astype(BF16)
            qn, kn = row(c_q_norm_g[j]), row(c_k_norm_g[j])
            q, k, v = _c_in(xl, gmix, sc1, sh1, w_in, qn, kn, tab_c, s, None)
            q_c, k_c, v_c = _c_in(xc, gmix, sc1, sh1, w_in, qn, kn, None, n_ctx, ctx_row)
            gc = N_HEADS_C // N_KV_C
            cfg_c = dict(batch=bn, n_kv_total=N_KV_C, n_kv=1, n_group=gc, out_dim=HEAD_DIM_C,
                         window=False, tq=1024)
            o = _attention(q, k, v, k_c, v_c, None, **cfg_c)
            o_lat, w_o = [o], [c_w_out[j].astype(BF16)]
            if with_ctx:
                o_ctx = [_attention(q_c, None, None, k_c, v_c, None, **cfg_c)]

        rw = jnp.pad(router_w[i], ((0, 0), (0, LANES - N_EXPERTS)))
        rw_hi = rw.astype(BF16)
        rw = jnp.concatenate([rw_hi, (rw - rw_hi.astype(F32)).astype(BF16)], axis=1)
        rb = jnp.concatenate([router_b[i], jnp.full((LANES - N_EXPERTS,), NEG_INF, F32)]).reshape(1, LANES)
        xl, tok_l, idx_l, gate_l = _out_proj(o_lat, w_o, xl, g1, gffn, sc2, sh2, rw, rb, s, None)
        if with_ctx:
            xc, tok_c, idx_c, gate_c = _out_proj(o_ctx, w_o, xc, g1, gffn, sc2, sh2, rw, rb, n_ctx, ctx_row)
            tokens = [tok_l, tok_c]
            eidx = jnp.concatenate([idx_l, idx_c], axis=0)
            gates = jnp.concatenate([gate_l, gate_c], axis=0)
        else:
            tokens, eidx, gates = [tok_l], idx_l, gate_l

        n_tiles = moe_b_in.shape[-1] // PAIR_TILE
        b1e = moe_b_in[i].reshape(N_EXPERTS, n_tiles, LANES, 2).transpose(0, 1, 3, 2)
        b1e = b1e.reshape(N_EXPERTS, 1, -1)
        b2e = moe_b_out[i][:, None, :]
        ys, dest = _moe(tokens, eidx[:, :TOP_K], moe_w_in, b1e, moe_w_out, b2e, i)

        last = i == depth - 1
        xl = _combine(xl, ys, dest, gates, g2, row(final_norm_g) if last else None,
                      row_off=0, rows_per_batch=s, fixed_mod=None)
        if with_ctx:
            xc = _combine(xc, ys, dest, gates, g2, None,
                          row_off=t_lat, rows_per_batch=n_ctx, fixed_mod=ctx_row)
    return xl.reshape(bn, s, d)
```

```python
import functools

import jax
import jax.numpy as jnp
from jax import lax
from jax.experimental import pallas as pl
from jax.experimental.pallas import tpu as pltpu

GRID_W = 64
N_HEADS_A, N_KV_A, HEAD_DIM_A, WINDOW = 8, 2, 64, 128
MLA_HEADS, MLA_Q_RANK, MLA_KV_RANK, MLA_NOPE, MLA_ROPE, MLA_V = 8, 384, 256, 64, 32, 64
N_HEADS_C, N_KV_C, HEAD_DIM_C = 8, 2, 128
N_EXPERTS, TOP_K = 32, 4
MLA_STEP_HEADS = 4
SWIGLU_ALPHA, SWIGLU_LIMIT = 1.702, 7.0
ROPE_THETA, RMS_EPS, NEG_INF = 10000.0, 1e-6, -1e30
LOG2_E = 1.4426950408889634

LANES = 128
SUBLANES = 8
VMEM_LIMIT_BYTES = 56 * 1024 * 1024

F32 = jnp.float32
BF16 = jnp.bfloat16
HIGHEST = lax.Precision.HIGHEST


def _params(*sem):
    return pltpu.CompilerParams(dimension_semantics=sem, vmem_limit_bytes=VMEM_LIMIT_BYTES)


def _dot(a, b):
    return jnp.dot(a, b, preferred_element_type=F32)


def _dot_t(a, b):
    return lax.dot_general(a, b, (((1,), (1,)), ((), ())), preferred_element_type=F32)


def _rms(x, g):
    return x * lax.rsqrt(jnp.mean(x * x, axis=-1, keepdims=True) + RMS_EPS) * g


def _modnorm(x, g, sc, sh):
    return _rms(x, g) * (1.0 + sc) + sh


def _rope_group(x, cos, sin, half, lo):
    lane = lax.broadcasted_iota(jnp.int32, x.shape, 1)
    first = (lane >= lo) & (lane < lo + half)
    rot = jnp.where(first, pltpu.roll(x, LANES - half, 1), pltpu.roll(x, half, 1))
    return x * cos + rot * sin


def _groups(x):
    return [x[:, i * LANES:(i + 1) * LANES] for i in range(x.shape[1] // LANES)]


def _mod_kernel(c_ref, w_ref, b_ref, o_ref):
    c = c_ref[...]
    a = c * jax.nn.sigmoid(c)
    o_ref[0] = jnp.dot(a, w_ref[0], preferred_element_type=F32, precision=HIGHEST) + b_ref[0]


def _modulation(c_all, ada_w, ada_b):
    depth, d, n = ada_w.shape
    r = c_all.shape[0]
    nt = n // 4
    return pl.pallas_call(
        _mod_kernel,
        out_shape=jax.ShapeDtypeStruct((depth, r, n), F32),
        grid=(depth, n // nt),
        in_specs=[
            pl.BlockSpec((r, d), lambda i, j: (0, 0)),
            pl.BlockSpec((1, d, nt), lambda i, j: (i, 0, j)),
            pl.BlockSpec((1, 1, nt), lambda i, j: (i, 0, j)),
        ],
        out_specs=pl.BlockSpec((1, r, nt), lambda i, j: (i, 0, j)),
        compiler_params=_params("arbitrary", "arbitrary"),
        name="modulation",
    )(c_all, ada_w, ada_b.reshape(depth, 1, n))


def _ab_in_kernel(*refs, rope):
    if rope:
        (x_ref, g_ref, sc_ref, sh_ref, w1_ref, qg_ref, wq_ref, kvg_ref, wk_ref, wv_ref,
         ca_ref, sa_ref, cb_ref, sb_ref, qa_ref, ka_ref, va_ref, qb_ref, kb_ref, vb_ref) = refs
    else:
        (x_ref, g_ref, sc_ref, sh_ref, w1_ref, qg_ref, wq_ref, kvg_ref, wk_ref, wv_ref,
         qa_ref, ka_ref, va_ref, qb_ref, kb_ref, vb_ref) = refs
    h = _modnorm(x_ref[...], g_ref[...], sc_ref[0], sh_ref[0]).astype(BF16)
    p = _dot(h, w1_ref[...])
    nqa = N_HEADS_A * LANES
    nka = N_KV_A * LANES
    o = 0
    qa = p[:, o:o + nqa]; o += nqa
    ka = p[:, o:o + nka]; o += nka
    va = p[:, o:o + nka]; o += nka
    cq = p[:, o:o + MLA_Q_RANK]; o += MLA_Q_RANK
    ckv = p[:, o:o + MLA_KV_RANK]; o += MLA_KV_RANK
    krg = p[:, o:o + LANES]

    qb = _dot(_rms(cq, qg_ref[...]).astype(BF16), wq_ref[...])
    ckv_n = _rms(ckv, kvg_ref[...]).astype(BF16)
    kb = _dot(ckv_n, wk_ref[...])
    vb = _dot(ckv_n, wv_ref[...])

    scale_a = HEAD_DIM_A ** -0.5 * LOG2_E
    scale_b = (MLA_NOPE + MLA_ROPE) ** -0.5 * LOG2_E
    if rope:
        ca, sa, cb, sb = ca_ref[...], sa_ref[...], cb_ref[...], sb_ref[...]
        half_a, half_b = HEAD_DIM_A // 2, MLA_ROPE // 2
        qa_g = [_rope_group(t, ca, sa, half_a, 0) * scale_a for t in _groups(qa)]
        ka_g = [_rope_group(t, ca, sa, half_a, 0) for t in _groups(ka)]
        qb_g = [_rope_group(t, cb, sb, half_b, MLA_NOPE) * scale_b for t in _groups(qb)]
        krg = _rope_group(krg, cb, sb, half_b, MLA_NOPE)
    else:
        qa_g = [t * scale_a for t in _groups(qa)]
        ka_g = _groups(ka)
        qb_g = [t * scale_b for t in _groups(qb)]
    kb_g = [t + krg for t in _groups(kb)]
    qa_ref[...] = jnp.concatenate(qa_g, axis=1).astype(BF16)
    ka_ref[...] = jnp.concatenate(ka_g, axis=1).astype(BF16)
    va_ref[...] = va.astype(BF16)
    qb_ref[...] = jnp.concatenate(qb_g, axis=1).astype(BF16)
    kb_ref[...] = jnp.concatenate(kb_g, axis=1).astype(BF16)
    vb_ref[...] = vb.astype(BF16)


def _row_tile(t, pref=512, also=()):
    tm = pref
    while any(n % tm for n in (t, *also)):
        tm //= 2
    return tm


def _stream_tile(t, rows_per_batch, fixed_mod, pref=512, also=()):
    return _row_tile(t if fixed_mod is not None else min(t, rows_per_batch), pref, also)


def _mod_index(rows_per_batch, tm, fixed):
    if fixed is not None:
        return lambda i: (fixed, 0, 0)
    return lambda i: ((i * tm) // rows_per_batch, 0, 0)


def _ab_in(x, g, sc, sh, w, tables, rows_per_batch, fixed_mod):
    t, d = x.shape
    tm = _stream_tile(t, rows_per_batch, fixed_mod)
    rope = tables is not None
    midx = _mod_index(rows_per_batch, tm, fixed_mod)
    full = lambda a: pl.BlockSpec(a.shape, lambda i: (0,) * a.ndim)
    in_specs = [
        pl.BlockSpec((tm, d), lambda i: (i, 0)),
        full(g),
        pl.BlockSpec((1, 1, d), midx),
        pl.BlockSpec((1, 1, d), midx),
    ] + [full(a) for a in w]
    args = [x, g, sc, sh] + list(w)
    if rope:
        nt = rows_per_batch // tm
        for tab in tables:
            in_specs.append(pl.BlockSpec((tm, LANES), lambda i: (i % nt, 0)))
            args.append(tab)
    widths = [N_HEADS_A * LANES, N_KV_A * LANES, N_KV_A * LANES,
              MLA_HEADS * LANES, MLA_HEADS * LANES, MLA_HEADS * LANES]
    return pl.pallas_call(
        functools.partial(_ab_in_kernel, rope=rope),
        out_shape=[jax.ShapeDtypeStruct((t, n), BF16) for n in widths],
        grid=(t // tm,),
        in_specs=in_specs,
        out_specs=[pl.BlockSpec((tm, n), lambda i: (i, 0)) for n in widths],
        compiler_params=_params("arbitrary"),
        name="ab_in_rope" if rope else "ab_in",
    )(*args)


def _c_in_kernel(*refs, rope):
    if rope:
        x_ref, g_ref, sc_ref, sh_ref, w_ref, qn_ref, kn_ref, cc_ref, sc2_ref, q_ref, k_ref, v_ref = refs
    else:
        x_ref, g_ref, sc_ref, sh_ref, w_ref, qn_ref, kn_ref, q_ref, k_ref, v_ref = refs
    h = _modnorm(x_ref[...], g_ref[...], sc_ref[0], sh_ref[0]).astype(BF16)
    p = _dot(h, w_ref[...])
    nq = N_HEADS_C * HEAD_DIM_C
    nk = N_KV_C * HEAD_DIM_C
    q_g = [_rms(t, qn_ref[...]) for t in _groups(p[:, :nq])]
    k_g = [_rms(t, kn_ref[...]) for t in _groups(p[:, nq:nq + nk])]
    scale = HEAD_DIM_C ** -0.5 * LOG2_E
    if rope:
        cc, ss = cc_ref[...], sc2_ref[...]
        q_g = [_rope_group(t, cc, ss, HEAD_DIM_C // 2, 0) for t in q_g]
        k_g = [_rope_group(t, cc, ss, HEAD_DIM_C // 2, 0) for t in k_g]
    q_ref[...] = jnp.concatenate([t * scale for t in q_g], axis=1).astype(BF16)
    k_ref[...] = jnp.concatenate(k_g, axis=1).astype(BF16)
    v_ref[...] = p[:, nq + nk:].astype(BF16)


def _c_in(x, g, sc, sh, w, qn, kn, tables, rows_per_batch, fixed_mod):
    t, d = x.shape
    tm = _stream_tile(t, rows_per_batch, fixed_mod)
    rope = tables is not None
    midx = _mod_index(rows_per_batch, tm, fixed_mod)
    full = lambda a: pl.BlockSpec(a.shape, lambda i: (0,) * a.ndim)
    in_specs = [
        pl.BlockSpec((tm, d), lambda i: (i, 0)),
        full(g),
        pl.BlockSpec((1, 1, d), midx),
        pl.BlockSpec((1, 1, d), midx),
        full(w), full(qn), full(kn),
    ]
    args = [x, g, sc, sh, w, qn, kn]
    if rope:
        nt = rows_per_batch // tm
        for tab in tables:
            in_specs.append(pl.BlockSpec((tm, LANES), lambda i: (i % nt, 0)))
            args.append(tab)
    widths = [N_HEADS_C * HEAD_DIM_C, N_KV_C * HEAD_DIM_C, N_KV_C * HEAD_DIM_C]
    return pl.pallas_call(
        functools.partial(_c_in_kernel, rope=rope),
        out_shape=[jax.ShapeDtypeStruct((t, n), BF16) for n in widths],
        grid=(t // tm,),
        in_specs=in_specs,
        out_specs=[pl.BlockSpec((tm, n), lambda i: (i, 0)) for n in widths],
        compiler_params=_params("arbitrary"),
        name="c_in_rope" if rope else "c_in",
    )(*args)


def _attn_kernel(*refs, n_kv, n_group, out_dim, has_lat, has_sink, window, tq, s_lat, kw):
    refs = list(refs)
    sink_ref = refs.pop(0) if has_sink else None
    q_ref = refs.pop(0)
    if has_lat:
        k1_ref, v1_ref = refs.pop(0), refs.pop(0)
    k2_ref, v2_ref, o_ref = refs
    hk0 = pl.program_id(1) * n_kv
    qi = pl.program_id(2)
    if window:
        ws = pl.multiple_of(jnp.clip(qi * tq - WINDOW, 0, s_lat - kw), LANES)
        rows = pl.ds(ws, kw)
        qpos = qi * tq + lax.broadcasted_iota(jnp.int32, (tq, kw), 0)
        kpos = ws + lax.broadcasted_iota(jnp.int32, (tq, kw), 1)
        ok = jnp.abs(kpos - qpos) <= WINDOW
    else:
        rows = slice(None)
    outs = []
    for j in range(n_kv):
        lanes = slice(j * LANES, (j + 1) * LANES)
        k2, v2 = k2_ref[:, lanes], v2_ref[:, lanes]
        if has_lat:
            k1, v1 = k1_ref[rows, lanes], v1_ref[rows, lanes]
        for g in range(n_group):
            h = j * n_group + g
            q = q_ref[:, h * LANES:(h + 1) * LANES]
            s2 = _dot_t(q, k2)
            m = jnp.max(s2, axis=-1, keepdims=True)
            if has_lat:
                s1 = _dot_t(q, k1)
                if window:
                    s1 = jnp.where(ok, s1, NEG_INF)
                m = jnp.maximum(m, jnp.max(s1, axis=-1, keepdims=True))
            if has_sink:
                sk = sink_ref[hk0 * n_group + h] * LOG2_E
                m = jnp.maximum(m, sk)
            e2 = jnp.exp2(s2 - m)
            den = jnp.sum(e2, axis=-1, keepdims=True)
            acc = _dot(e2.astype(BF16), v2)
            if has_lat:
                e1 = jnp.exp2(s1 - m)
                den = den + jnp.sum(e1, axis=-1, keepdims=True)
                acc = acc + _dot(e1.astype(BF16), v1)
            if has_sink:
                den = den + jnp.exp2(sk - m)
            outs.append(acc / den)
    if out_dim == LANES:
        for h, o in enumerate(outs):
            o_ref[:, h * LANES:(h + 1) * LANES] = o.astype(BF16)
    else:
        lane = lax.broadcasted_iota(jnp.int32, (tq, LANES), 1)
        for p in range(len(outs) // 2):
            pair = jnp.where(lane < out_dim, outs[2 * p], pltpu.roll(outs[2 * p + 1], out_dim, 1))
            o_ref[:, p * LANES:(p + 1) * LANES] = pair.astype(BF16)


def _attention(q, k_lat, v_lat, k_ctx, v_ctx, sink, *, batch, n_kv_total, n_kv, n_group, out_dim,
               window, tq):
    tq_total = q.shape[0] // batch
    tq = min(tq, tq_total)
    n_ctx = k_ctx.shape[0] // batch
    has_lat = k_lat is not None
    has_sink = sink is not None
    s_lat = k_lat.shape[0] // batch if has_lat else 0
    kw = min(tq + 2 * WINDOW, s_lat) if window else 0
    nq = tq_total // tq
    n_heads = n_kv * n_group
    assert out_dim == LANES or (2 * out_dim == LANES and n_heads % 2 == 0)
    in_specs, args = [], []
    if has_sink:
        in_specs.append(pl.BlockSpec(memory_space=pltpu.SMEM))
        args.append(sink)
    in_specs.append(pl.BlockSpec((tq, n_heads * LANES), lambda b, h, i: (b * nq + i, h)))
    args.append(q)
    if has_lat:
        in_specs += [pl.BlockSpec((s_lat, n_kv * LANES), lambda b, h, i: (b, h))] * 2
        args += [k_lat, v_lat]
    in_specs += [pl.BlockSpec((n_ctx, n_kv * LANES), lambda b, h, i: (b, h))] * 2
    args += [k_ctx, v_ctx]
    n_steps = n_kv_total // n_kv
    return pl.pallas_call(
        functools.partial(_attn_kernel, n_kv=n_kv, n_group=n_group, out_dim=out_dim, has_lat=has_lat,
                          has_sink=has_sink, window=window, tq=tq, s_lat=s_lat, kw=kw),
        out_shape=jax.ShapeDtypeStruct((q.shape[0], n_steps * n_heads * out_dim), BF16),
        grid=(batch, n_steps, nq),
        in_specs=in_specs,
        out_specs=pl.BlockSpec((tq, n_heads * out_dim), lambda b, h, i: (b * nq + i, h)),
        compiler_params=_params("arbitrary", "arbitrary", "arbitrary"),
        name="attn_win" if window else ("attn_dense" if has_lat else "attn_ctx"),
    )(*args)


def _out_kernel(*refs, n_parts, tm):
    o_refs = refs[:n_parts]
    w_refs = refs[n_parts:2 * n_parts]
    (x_ref, g1_ref, gn_ref, sc_ref, sh_ref, rw_ref, rb_ref,
     xo_ref, tok_ref, idx_ref, gate_ref) = refs[2 * n_parts:]
    y = _dot(o_refs[0][...], w_refs[0][...])
    for o_r, w_r in zip(o_refs[1:], w_refs[1:]):
        y = y + _dot(o_r[...], w_r[...])
    x = x_ref[...] + g1_ref[0] * y
    xo_ref[...] = x
    tok = _modnorm(x, gn_ref[...], sc_ref[0], sh_ref[0])
    for j in range(tok.shape[1] // LANES):
        tok_ref[pl.ds(j, tm, stride=SUBLANES), :] = tok[:, j * LANES:(j + 1) * LANES]
    tok_hi = tok.astype(BF16)
    tok_lo = (tok - tok_hi.astype(F32)).astype(BF16)
    hh_hl = _dot(tok_hi, rw_ref[...])
    logits = (hh_hl[:, :LANES] + hh_hl[:, LANES:] + _dot(tok_lo, rw_ref[:, :LANES])) + rb_ref[...]
    lane = lax.broadcasted_iota(jnp.int32, logits.shape, 1).astype(F32)
    vals, idxs = [], []
    l = logits
    for _ in range(TOP_K):
        m = jnp.max(l, axis=-1, keepdims=True)
        idx = jnp.min(jnp.where(l == m, lane, float(LANES)), axis=-1, keepdims=True)
        vals.append(m)
        idxs.append(idx)
        l = jnp.where(lane == idx, -jnp.inf, l)
    es = [jnp.exp(v - vals[0]) for v in vals]
    den = es[0]
    for e in es[1:]:
        den = den + e
    lane8 = lax.broadcasted_iota(jnp.int32, (tm, SUBLANES), 1)
    io = jnp.zeros((tm, SUBLANES), jnp.int32)
    go = jnp.zeros((tm, SUBLANES), F32)
    for k in range(TOP_K):
        io = jnp.where(lane8 == k, idxs[k].astype(jnp.int32), io)
        go = jnp.where(lane8 == k, es[k] / den, go)
    idx_ref[...] = io
    gate_ref[...] = go


def _out_proj(o_parts, w_parts, x, g1, gn, sc, sh, rw, rb, rows_per_batch, fixed_mod):
    t, d = x.shape
    tm = _stream_tile(t, rows_per_batch, fixed_mod)
    midx = _mod_index(rows_per_batch, tm, fixed_mod)
    full = lambda a: pl.BlockSpec(a.shape, lambda i: (0,) * a.ndim)
    n_parts = len(o_parts)
    in_specs = ([pl.BlockSpec((tm, o.shape[1]), lambda i: (i, 0)) for o in o_parts]
                + [full(w) for w in w_parts]
                + [pl.BlockSpec((tm, d), lambda i: (i, 0)),
                   pl.BlockSpec((1, 1, d), midx), full(gn),
                   pl.BlockSpec((1, 1, d), midx), pl.BlockSpec((1, 1, d), midx),
                   full(rw), full(rb)])
    return pl.pallas_call(
        functools.partial(_out_kernel, n_parts=n_parts, tm=tm),
        out_shape=[jax.ShapeDtypeStruct((t, d), F32),
                   jax.ShapeDtypeStruct((t * SUBLANES, d // SUBLANES), F32),
                   jax.ShapeDtypeStruct((t, SUBLANES), jnp.int32),
                   jax.ShapeDtypeStruct((t, SUBLANES), F32)],
        grid=(t // tm,),
        in_specs=in_specs,
        out_specs=[pl.BlockSpec((tm, d), lambda i: (i, 0)),
                   pl.BlockSpec((tm * SUBLANES, d // SUBLANES), lambda i: (i, 0)),
                   pl.BlockSpec((tm, SUBLANES), lambda i: (i, 0)),
                   pl.BlockSpec((tm, SUBLANES), lambda i: (i, 0))],
        compiler_params=_params("arbitrary"),
        name="out_proj",
    )(*o_parts, *w_parts, x, g1, gn, sc, sh, rw, rb)


DISPATCH_TILE = 512
PAIR_TILE = 2 * LANES
WEIGHT_SPLIT = 2
ZERO_ROWS = 64


def _row(ref, r):
    return ref.at[pl.ds(pl.multiple_of(r * SUBLANES, SUBLANES), SUBLANES)]


def _dispatch_kernel(*refs, tm, part_tiles, n_experts):
    n_parts = len(part_tiles)
    plo_ref, phi_ref, dest_ref = refs[:3]
    tok_refs = refs[3:3 + n_parts]
    p_ref, w1_ref, w2_ref, xs_ref, o1_ref, o2_ref, zero_ref, st0, st1, sems, zsem = refs[3 + n_parts:]
    i = pl.program_id(0)
    n_tiles = sum(part_tiles)
    stage = (st0, st1)

    @pl.when(i == 0)
    def _():
        zero_ref[...] = jnp.zeros(zero_ref.shape, zero_ref.dtype)

        def fill_range(e, carry):
            lo, hi = plo_ref[e], phi_ref[e]
            levels, pos = [], lo
            for size in (ZERO_ROWS, SUBLANES, 1):
                n = lax.shift_right_logical(hi - pos, size.bit_length() - 1)
                levels.append((size, pos, n))
                pos = pos + n * size

            def copy(size, base, c):
                first = pl.multiple_of((base + c * size) * SUBLANES, SUBLANES)
                return pltpu.make_async_copy(zero_ref.at[pl.ds(0, size * SUBLANES)],
                                             xs_ref.at[pl.ds(first, size * SUBLANES)], zsem)

            for op in ("start", "wait"):
                for size, base, n in levels:
                    lax.fori_loop(0, n, lambda c, cr, op=op, size=size, base=base:
                                  (getattr(copy(size, base, c), op)(), cr)[1], carry)
            return carry

        lax.fori_loop(0, plo_ref.shape[0], fill_range, 0)

    def start_rows(tok_ref, slot):
        stage[slot][...] = tok_ref[...]

        def body(t, carry):
            for u in range(2):
                r = t * 2 + u
                src = _row(stage[slot], r)
                for k in range(TOP_K):
                    pltpu.make_async_copy(src, _row(xs_ref, dest_ref[0, 0, r * TOP_K + k]), sems.at[slot]).start(
                        priority=k % 2)
            return carry

        lax.fori_loop(0, tm // 2, body, 0)

    def wait_rows(slot):
        done = xs_ref.at[pl.ds(0, tm * TOP_K * SUBLANES)]
        pltpu.make_async_copy(done, done, sems.at[slot]).wait()

    first = 0
    for tok_ref, n in zip(tok_refs, part_tiles):
        for slot in range(2):
            pl.when((i >= first) & (i < first + n) & (i % 2 == slot))(
                functools.partial(start_rows, tok_ref, slot))
        first += n

    @pl.when(i < n_experts * WEIGHT_SPLIT)
    def _():
        perm = p_ref[...]
        for j in range(w1_ref.shape[3] // PAIR_TILE):
            cols = slice(j * PAIR_TILE, (j + 1) * PAIR_TILE)
            o1_ref[0, :, cols] = _dot(w1_ref[0, 0, :, cols].astype(BF16), perm).astype(BF16)
        o2_ref[0] = w2_ref[0, 0].astype(BF16)

    for slot in range(2):
        pl.when((i >= 1) & (i <= n_tiles) & ((i - 1) % 2 == slot))(functools.partial(wait_rows, slot))
        pl.when((i == pl.num_programs(0) - 1) & (i < n_tiles) & (i % 2 == slot))(
            functools.partial(wait_rows, slot))


def _dispatch(token_parts, dest, pad_lo, pad_hi, cap, w_in, w_out, layer):
    t = dest.shape[0]
    tm = _row_tile(min(p.shape[0] // SUBLANES for p in token_parts), DISPATCH_TILE)
    part_tiles = tuple(p.shape[0] // SUBLANES // tm for p in token_parts)
    n_tiles = sum(part_tiles)
    assert n_tiles * tm == t
    _, e, d, f2 = w_in.shape
    src = jnp.arange(PAIR_TILE, dtype=jnp.int32)[:, None]
    dst = jnp.arange(PAIR_TILE, dtype=jnp.int32)[None, :]
    perm = (dst == src // 2 + LANES * (src % 2)).astype(BF16)
    in_specs = [pl.BlockSpec((1, 1, tm * TOP_K), lambda i, lo, hi: (jnp.minimum(i, n_tiles - 1), 0, 0),
                             memory_space=pltpu.SMEM)]
    first = 0
    for n in part_tiles:
        in_specs.append(pl.BlockSpec(
            (tm * SUBLANES, LANES), lambda i, lo, hi, first=first, n=n: (jnp.clip(i - first, 0, n - 1), 0)))
        first += n
    ws = WEIGHT_SPLIT
    assert f2 % (ws * PAIR_TILE) == 0 and (f2 // 2) % (ws * SUBLANES * 2) == 0
    share = lambda i: jnp.minimum(i, e * ws - 1)
    in_specs += [pl.BlockSpec((PAIR_TILE, PAIR_TILE), lambda i, lo, hi: (0, 0)),
                 pl.BlockSpec((1, 1, d, f2 // ws), lambda i, lo, hi: (layer, share(i) // ws, 0, share(i) % ws)),
                 pl.BlockSpec((1, 1, f2 // 2 // ws, d),
                              lambda i, lo, hi: (layer, share(i) // ws, share(i) % ws, 0))]
    grid_spec = pltpu.PrefetchScalarGridSpec(
        num_scalar_prefetch=2,
        grid=(max(n_tiles, e * ws),),
        in_specs=in_specs,
        out_specs=[pl.BlockSpec(memory_space=pl.ANY),
                   pl.BlockSpec((1, d, f2 // ws), lambda i, lo, hi: (share(i) // ws, 0, share(i) % ws)),
                   pl.BlockSpec((1, f2 // 2 // ws, d), lambda i, lo, hi: (share(i) // ws, share(i) % ws, 0))],
        scratch_shapes=[pltpu.VMEM((ZERO_ROWS * SUBLANES, LANES), F32),
                        pltpu.VMEM((tm * SUBLANES, LANES), F32), pltpu.VMEM((tm * SUBLANES, LANES), F32),
                        pltpu.SemaphoreType.DMA((2,)), pltpu.SemaphoreType.DMA],
    )
    return pl.pallas_call(
        functools.partial(_dispatch_kernel, tm=tm, part_tiles=part_tiles, n_experts=e),
        out_shape=[jax.ShapeDtypeStruct((cap * SUBLANES, LANES), F32),
                   jax.ShapeDtypeStruct(w_in.shape[1:], BF16), jax.ShapeDtypeStruct(w_out.shape[1:], BF16)],
        grid_spec=grid_spec,
        compiler_params=_params("arbitrary"),
        name="dispatch",
    )(pad_lo, pad_hi, dest.reshape(n_tiles, 1, tm * TOP_K), *token_parts, perm, w_in, w_out)


EXPERT_BLOCK = 512


def _expert_kernel(be_ref, nb_ref, x_ref, w1_ref, b1_ref, w2_ref, b2_ref, o_ref, *, bm):
    del be_ref
    b = pl.program_id(0)
    n_j = w1_ref.shape[1] // LANES

    @pl.when(b < nb_ref[0])
    def _():
        x = jnp.concatenate([x_ref[pl.ds(j, bm, stride=SUBLANES), :] for j in range(n_j)], axis=1)
        u = _dot(x.astype(BF16), w1_ref[0]) + b1_ref[0]
        acts = []
        for j in range(u.shape[1] // PAIR_TILE):
            glu = jnp.minimum(u[:, j * PAIR_TILE:j * PAIR_TILE + LANES], SWIGLU_LIMIT)
            lin = jnp.clip(u[:, j * PAIR_TILE + LANES:(j + 1) * PAIR_TILE], -SWIGLU_LIMIT, SWIGLU_LIMIT)
            acts.append(glu * jax.nn.sigmoid(SWIGLU_ALPHA * glu) * (lin + 1.0))
        a = jnp.concatenate(acts, axis=1)
        y = _dot(a.astype(BF16), w2_ref[0]) + b2_ref[0]
        for j in range(y.shape[1] // LANES):
            o_ref[pl.ds(j, bm, stride=SUBLANES), :] = y[:, j * LANES:(j + 1) * LANES]

    @pl.when(b >= nb_ref[0])
    def _():
        o_ref[...] = jnp.zeros(o_ref.shape, o_ref.dtype)


def _expert_mlp(xs, blk_expert, n_used, w1, b1, w2, b2):
    bm = EXPERT_BLOCK
    nblk = blk_expert.shape[0]
    _, d, f2 = w1.shape
    grid_spec = pltpu.PrefetchScalarGridSpec(
        num_scalar_prefetch=2,
        grid=(nblk,),
        in_specs=[
            pl.BlockSpec((bm * SUBLANES, LANES), lambda b, be, nb: (b, 0)),
            pl.BlockSpec((1, d, f2), lambda b, be, nb: (be[b], 0, 0)),
            pl.BlockSpec((1, 1, f2), lambda b, be, nb: (be[b], 0, 0)),
            pl.BlockSpec((1, f2 // 2, d), lambda b, be, nb: (be[b], 0, 0)),
            pl.BlockSpec((1, 1, d), lambda b, be, nb: (be[b], 0, 0)),
        ],
        out_specs=pl.BlockSpec((bm * SUBLANES, LANES), lambda b, be, nb: (b, 0)),
    )
    return pl.pallas_call(
        functools.partial(_expert_kernel, bm=bm),
        out_shape=jax.ShapeDtypeStruct(xs.shape, F32),
        grid_spec=grid_spec,
        compiler_params=_params("arbitrary"),
        name="expert_mlp",
    )(blk_expert, n_used, xs, w1, b1, w2, b2)


def _combine_kernel(*refs, tm, final):
    if final:
        dest_ref, next_ref, x_ref, ys_ref, gate_ref, g2_ref, fg_ref, o_ref, yb0, yb1, sems = refs
    else:
        dest_ref, next_ref, x_ref, ys_ref, gate_ref, g2_ref, o_ref, yb0, yb1, sems = refs
    i = pl.program_id(0)
    ybufs = (yb0, yb1)

    def start_gather(d_ref, slot):
        def body(t, carry):
            for u in range(2):
                r = t * 2 + u
                for k in range(TOP_K):
                    pltpu.make_async_copy(_row(ys_ref, d_ref[0, 0, r * TOP_K + k]),
                                          _row(ybufs[slot], k * tm + r), sems.at[slot]).start(priority=k % 2)
            return carry

        lax.fori_loop(0, tm // 2, body, 0)

    def finish(slot):
        ybuf = ybufs[slot]
        pltpu.make_async_copy(ybuf, ybuf, sems.at[slot]).wait()
        gate = gate_ref[...]
        parts = []
        for j in range(x_ref.shape[1] // LANES):
            f = gate[:, 0:1] * ybuf[pl.ds(j, tm, stride=SUBLANES), :]
            for k in range(1, TOP_K):
                f = f + gate[:, k:k + 1] * ybuf[pl.ds(k * tm * SUBLANES + j, tm, stride=SUBLANES), :]
            parts.append(f)
        x = x_ref[...] + g2_ref[0] * jnp.concatenate(parts, axis=1)
        if final:
            x = _rms(x, fg_ref[...])
        o_ref[...] = x

    @pl.when(i == 0)
    def _():
        start_gather(dest_ref, 0)

    for slot in range(2):
        @pl.when(i % 2 == slot)
        def _(slot=slot):
            @pl.when(i + 1 < pl.num_programs(0))
            def _():
                start_gather(next_ref, 1 - slot)

            finish(slot)


def _combine(x, ys, dest, gates, g2, final_g, *, row_off, rows_per_batch, fixed_mod):
    t, d = x.shape
    tm = _stream_tile(t, rows_per_batch, fixed_mod, also=(dest.shape[0], row_off))
    midx = _mod_index(rows_per_batch, tm, fixed_mod)
    final = final_g is not None
    off = row_off // tm
    t_all = dest.shape[0]
    n_steps = t // tm
    dest_tiles = dest.reshape(t_all // tm, 1, tm * TOP_K)
    in_specs = [pl.BlockSpec((1, 1, tm * TOP_K), lambda i: (off + i, 0, 0), memory_space=pltpu.SMEM),
                pl.BlockSpec((1, 1, tm * TOP_K), lambda i: (off + jnp.minimum(i + 1, n_steps - 1), 0, 0),
                             memory_space=pltpu.SMEM),
                pl.BlockSpec((tm, d), lambda i: (i, 0)),
                pl.BlockSpec(memory_space=pl.ANY),
                pl.BlockSpec((tm, SUBLANES), lambda i: (off + i, 0)),
                pl.BlockSpec((1, 1, d), midx)]
    args = [dest_tiles, dest_tiles, x, ys, gates, g2]
    if final:
        in_specs.append(pl.BlockSpec(final_g.shape, lambda i: (0, 0)))
        args.append(final_g)
    return pl.pallas_call(
        functools.partial(_combine_kernel, tm=tm, final=final),
        out_shape=jax.ShapeDtypeStruct((t, d), F32),
        grid=(n_steps,),
        in_specs=in_specs,
        out_specs=pl.BlockSpec((tm, d), lambda i: (i, 0)),
        scratch_shapes=[pltpu.VMEM((TOP_K * tm * SUBLANES, LANES), F32),
                        pltpu.VMEM((TOP_K * tm * SUBLANES, LANES), F32),
                        pltpu.SemaphoreType.DMA((2,))],
        compiler_params=_params("arbitrary"),
        name="combine_final" if final else "combine",
    )(*args)


def _pad_heads(w, n_heads, axis):
    shape = list(w.shape)
    hd = shape[axis] // n_heads
    w = w.reshape(shape[:axis] + [n_heads, hd] + shape[axis + 1:])
    pad = [(0, 0)] * w.ndim
    pad[axis + 1] = (0, LANES - hd)
    w = jnp.pad(w, pad)
    shape[axis] = n_heads * LANES
    return w.reshape(shape)


def _rope_tables(s, rot_dim, lo):
    pos = jnp.arange(s, dtype=jnp.int32)
    rows, cols = (pos // GRID_W).astype(F32), (pos % GRID_W).astype(F32)
    quarter = rot_dim // 4
    inv = ROPE_THETA ** (-jnp.arange(quarter, dtype=F32) / quarter)
    ang = jnp.concatenate([rows[:, None] * inv, cols[:, None] * inv], axis=-1)
    cos, sin = jnp.cos(ang), jnp.sin(ang)
    hi = LANES - lo - rot_dim
    cos_t = jnp.concatenate([jnp.ones((s, lo), F32), cos, cos, jnp.ones((s, hi), F32)], axis=1)
    sin_t = jnp.concatenate([jnp.zeros((s, lo), F32), -sin, sin, jnp.zeros((s, hi), F32)], axis=1)
    return cos_t, sin_t


def _moe_plan(eidx, bm):
    t = eidx.shape[0]
    ids = jnp.arange(N_EXPERTS, dtype=jnp.int32)
    sel = eidx[:, :, None] == ids
    hit = sel.astype(jnp.int32).sum(1)
    chunk = _row_tile(t, 256)
    tri = jnp.tril(jnp.ones((chunk, chunk), F32))
    within = jnp.einsum("ij,cjk->cik", tri, hit.astype(F32).reshape(t // chunk, chunk, N_EXPERTS),
                        precision=HIGHEST)
    totals = within[:, -1, :]
    incl = (within + (jnp.cumsum(totals, axis=0) - totals)[:, None, :]).reshape(t, N_EXPERTS)
    incl = incl.astype(jnp.int32)
    counts = incl[-1]
    rank = jnp.where(sel, (incl - hit)[:, None, :], 0).sum(-1)
    padded = (counts + bm - 1) // bm * bm
    pends = jnp.cumsum(padded)
    pstarts = pends - padded
    dest = jnp.where(sel, pstarts, 0).sum(-1) + rank
    n_assign = t * TOP_K
    nblk = -(-(n_assign + N_EXPERTS * (bm - 1)) // bm)
    blk_start = jnp.arange(nblk, dtype=jnp.int32) * bm
    blk_expert = jnp.minimum((blk_start[:, None] >= pends[None, :]).sum(1), N_EXPERTS - 1)
    n_used = (pends[-1] // bm).astype(jnp.int32).reshape(1)
    pad_lo = jnp.concatenate([pstarts + counts, pends[-1:]]).astype(jnp.int32)
    pad_hi = jnp.concatenate([pends, jnp.full((1,), nblk * bm, pends.dtype)]).astype(jnp.int32)
    return dest.astype(jnp.int32), blk_expert.astype(jnp.int32), n_used, pad_lo, pad_hi


def _moe(token_parts, eidx, w_in, b1, w_out, b2, layer):
    dest, blk_expert, n_used, pad_lo, pad_hi = _moe_plan(eidx, EXPERT_BLOCK)
    xs, w1, w2 = _dispatch(token_parts, dest, pad_lo, pad_hi, blk_expert.shape[0] * EXPERT_BLOCK,
                           w_in, w_out, layer)
    return _expert_mlp(xs, blk_expert, n_used, w1, b1, w2, b2), dest


def kernel(x, c, ctx, c_ctx, ada_w, ada_b, norm_mix_g, norm_ffn_g, ab_w_in, mla_q_norm_g, mla_wq_b,
           mla_kv_norm_g, mla_wkv_b, swa_sink, ab_w_out, c_w_in, c_q_norm_g, c_k_norm_g, c_w_out,
           router_w, router_b, moe_w_in, moe_b_in, moe_w_out, moe_b_out, final_norm_g):
    bn, s, d = x.shape
    n_ctx = ctx.shape[1]
    depth = ada_w.shape[0]
    t_lat, t_ctx = bn * s, bn * n_ctx
    row = lambda v: v.reshape(1, -1)

    n_mod = -(-(bn + 1) // SUBLANES) * SUBLANES
    c_all = jnp.concatenate([c, c_ctx[None], jnp.zeros((n_mod - bn - 1, d), F32)], axis=0)
    mod = _modulation(c_all, ada_w, ada_b).reshape(depth, n_mod, 6, 1, d)
    ctx_row = bn

    tab_a = _rope_tables(s, HEAD_DIM_A, 0)
    tab_b = _rope_tables(s, MLA_ROPE, MLA_NOPE)
    tab_c = _rope_tables(s, HEAD_DIM_C, 0)

    xl = x.reshape(t_lat, d)
    xc = ctx.reshape(t_ctx, d)
    for i in range(depth):
        with_ctx = i < depth - 1
        j = i // 2
        sh1, sc1, g1, sh2, sc2, g2 = [mod[i, :, m] for m in range(6)]
        gmix, gffn = row(norm_mix_g[i]), row(norm_ffn_g[i])
        if i % 2 == 0:
            w_in = ab_w_in[j]
            sizes = [N_HEADS_A * HEAD_DIM_A, N_KV_A * HEAD_DIM_A, N_KV_A * HEAD_DIM_A,
                     MLA_Q_RANK, MLA_KV_RANK, MLA_ROPE]
            offs = [0]
            for n in sizes:
                offs.append(offs[-1] + n)
            cols = [w_in[:, offs[m]:offs[m + 1]] for m in range(6)]
            krg = jnp.pad(cols[5], ((0, 0), (MLA_NOPE, LANES - MLA_NOPE - MLA_ROPE)))
            w1 = jnp.concatenate([_pad_heads(cols[0], N_HEADS_A, 1), _pad_heads(cols[1], N_KV_A, 1),
                                  _pad_heads(cols[2], N_KV_A, 1), cols[3], cols[4], krg], axis=1).astype(BF16)
            wq = _pad_heads(mla_wq_b[j], MLA_HEADS, 1).astype(BF16)
            wkv = mla_wkv_b[j].reshape(MLA_KV_RANK, MLA_HEADS, MLA_NOPE + MLA_V)
            wk = _pad_heads(wkv[:, :, :MLA_NOPE].reshape(MLA_KV_RANK, -1), MLA_HEADS, 1).astype(BF16)
            wv = _pad_heads(wkv[:, :, MLA_NOPE:].reshape(MLA_KV_RANK, -1), MLA_HEADS, 1).astype(BF16)
            ws = [w1, row(mla_q_norm_g[j]), wq, row(mla_kv_norm_g[j]), wk, wv]
            na = N_HEADS_A * HEAD_DIM_A
            wo_a = ab_w_out[j][:na].astype(BF16)
            wo_b = ab_w_out[j][na:].astype(BF16)
            sink = swa_sink[j]

            qa, ka, va, qb, kb, vb = _ab_in(xl, gmix, sc1, sh1, ws, tab_a + tab_b, s, None)
            qa_c, ka_c, va_c, qb_c, kb_c, vb_c = _ab_in(xc, gmix, sc1, sh1, ws, None, n_ctx, ctx_row)
            ga = N_HEADS_A // N_KV_A
            cfg_a = dict(batch=bn, n_kv_total=N_KV_A, n_kv=N_KV_A, n_group=ga, out_dim=HEAD_DIM_A)
            cfg_b = dict(batch=bn, n_kv_total=MLA_HEADS, n_kv=MLA_STEP_HEADS, n_group=1, out_dim=MLA_V)
            oa = _attention(qa, ka, va, ka_c, va_c, sink, window=True, tq=512, **cfg_a)
            ob = _attention(qb, kb, vb, kb_c, vb_c, None, window=False, tq=1024, **cfg_b)
            o_lat, w_o = [oa, ob], [wo_a, wo_b]
            if with_ctx:
                oa_c = _attention(qa_c, None, None, ka_c, va_c, sink, window=False, tq=256, **cfg_a)
                ob_c = _attention(qb_c, None, None, kb_c, vb_c, None, window=False, tq=256, **cfg_b)
                o_ctx = [oa_c, ob_c]
        else:
            w_in = c_w_in[j].astype(BF16)
            qn, kn = row(c_q_norm_g[j]), row(c_k_norm_g[j])
            q, k, v = _c_in(xl, gmix, sc1, sh1, w_in, qn, kn, tab_c, s, None)
            q_c, k_c, v_c = _c_in(xc, gmix, sc1, sh1, w_in, qn, kn, None, n_ctx, ctx_row)
            gc = N_HEADS_C // N_KV_C
            cfg_c = dict(batch=bn, n_kv_total=N_KV_C, n_kv=1, n_group=gc, out_dim=HEAD_DIM_C,
                         window=False, tq=1024)
            o = _attention(q, k, v, k_c, v_c, None, **cfg_c)
            o_lat, w_o = [o], [c_w_out[j].astype(BF16)]
            if with_ctx:
                o_ctx = [_attention(q_c, None, None, k_c, v_c, None, **cfg_c)]

        rw = jnp.pad(router_w[i], ((0, 0), (0, LANES - N_EXPERTS)))
        rw_hi = rw.astype(BF16)
        rw = jnp.concatenate([rw_hi, (rw - rw_hi.astype(F32)).astype(BF16)], axis=1)
        rb = jnp.concatenate([router_b[i], jnp.full((LANES - N_EXPERTS,), NEG_INF, F32)]).reshape(1, LANES)
        xl, tok_l, idx_l, gate_l = _out_proj(o_lat, w_o, xl, g1, gffn, sc2, sh2, rw, rb, s, None)
        if with_ctx:
            xc, tok_c, idx_c, gate_c = _out_proj(o_ctx, w_o, xc, g1, gffn, sc2, sh2, rw, rb, n_ctx, ctx_row)
            tokens = [tok_l, tok_c]
            eidx = jnp.concatenate([idx_l, idx_c], axis=0)
            gates = jnp.concatenate([gate_l, gate_c], axis=0)
        else:
            tokens, eidx, gates = [tok_l], idx_l, gate_l

        n_tiles = moe_b_in.shape[-1] // PAIR_TILE
        b1e = moe_b_in[i].reshape(N_EXPERTS, n_tiles, LANES, 2).transpose(0, 1, 3, 2)
        b1e = b1e.reshape(N_EXPERTS, 1, -1)
        b2e = moe_b_out[i][:, None, :]
        ys, dest = _moe(tokens, eidx[:, :TOP_K], moe_w_in, b1e, moe_w_out, b2e, i)

        last = i == depth - 1
        xl = _combine(xl, ys, dest, gates, g2, row(final_norm_g) if last else None,
                      row_off=0, rows_per_batch=s, fixed_mod=None)
        if with_ctx:
            xc = _combine(xc, ys, dest, gates, g2, None,
                          row_off=t_lat, rows_per_batch=n_ctx, fixed_mod=ctx_row)
    return xl.reshape(bn, s, d)
```
